```python
import math
import jax, jax.numpy as jnp
from jax import lax
import numpy as np

D_MODEL = 4096
BATCH = 2
SEQ = 8192
DEPTH = 2

GRID_W = 64
CTX_LEN = 256
HEAD_DIM = 128
N_Q_HEADS = 24
N_KV_HEADS = 8
Q_PER_KV = N_Q_HEADS // N_KV_HEADS
ATTN_WIDTH = N_Q_HEADS * HEAD_DIM
KV_WIDTH = N_KV_HEADS * HEAD_DIM
WINDOW = 128
BLOCK = 128
ROPE_BASE = 10000.0
AXIS_ROPE_DIM = HEAD_DIM // 2
SCALE = HEAD_DIM ** -0.5
SSM_WIDTH = D_MODEL - ATTN_WIDTH
SSM_GROUP = 16
SSM_GROUPS = SSM_WIDTH // SSM_GROUP
SSM_STATE = 64
MIX_WIDTH = ATTN_WIDTH + SSM_WIDTH
IN_COLS = ATTN_WIDTH + 2 * KV_WIDTH + SSM_WIDTH
D_FF = 11008
N_EXPERTS = 8
TOP_K = 2
D_EXPERT = 4096
N_DENSE = (DEPTH + 1) // 2
N_MOE = DEPTH // 2
N_MOD = 6
EPS = 1e-6
NEG_INF = -1e30
DT_MIN = 1e-3
DT_MAX = 1e-1

kernel_name = "hybrid_swa_s5_moe_diffusion_block"


def rms_norm(x, g):
    xf = x.astype(jnp.float32)
    y = xf * lax.rsqrt(jnp.mean(xf * xf, axis=-1, keepdims=True) + EPS)
    return (y * g.astype(jnp.float32)).astype(x.dtype)


def modulate(x, g, shift, scale):
    return rms_norm(x, g) * (1.0 + scale) + shift


def ada_modulation(cond, w_mod, b_mod):
    m = jax.nn.silu(cond) @ w_mod + b_mod
    return jnp.split(m, N_MOD, axis=-1)


def axial_rope_tables(n_tok):
    rows = n_tok // GRID_W
    row = jnp.broadcast_to(jnp.arange(rows)[:, None], (rows, GRID_W)).reshape(-1).astype(jnp.float32)
    col = jnp.broadcast_to(jnp.arange(GRID_W)[None, :], (rows, GRID_W)).reshape(-1).astype(jnp.float32)
    inv_freq = ROPE_BASE ** (-jnp.arange(0, AXIS_ROPE_DIM, 2, dtype=jnp.float32) / AXIS_ROPE_DIM)
    ang_r = row[:, None] * inv_freq
    ang_c = col[:, None] * inv_freq
    ang = jnp.concatenate([ang_r, ang_r, ang_c, ang_c], axis=-1)
    return jnp.cos(ang), jnp.sin(ang)


def _rotate_half(x):
    x1, x2 = jnp.split(x, 2, axis=-1)
    return jnp.concatenate([-x2, x1], axis=-1)


def apply_axial_rope(x, cos, sin):
    xr, xc = jnp.split(x, 2, axis=-1)
    rot = jnp.concatenate([_rotate_half(xr), _rotate_half(xc)], axis=-1)
    return x * cos[:, None, :].astype(x.dtype) + rot * sin[:, None, :].astype(x.dtype)


def latent_attention(q, k, v, k_ctx, v_ctx, sink, cos, sin):
    bsz, n_tok = q.shape[:2]
    nb = n_tok // BLOCK
    n_ctx = k_ctx.shape[1]
    qr = apply_axial_rope(q, cos, sin).reshape(bsz, nb, BLOCK, N_KV_HEADS, Q_PER_KV, HEAD_DIM)
    qp = q.reshape(bsz, nb, BLOCK, N_KV_HEADS, Q_PER_KV, HEAD_DIM)
    kr = apply_axial_rope(k, cos, sin)

    def band(t):
        tb = t.reshape(bsz, nb, BLOCK, N_KV_HEADS, HEAD_DIM)
        tp = jnp.pad(tb, ((0, 0), (1, 1), (0, 0), (0, 0), (0, 0)))
        return jnp.concatenate([tp[:, :-2], tp[:, 1:-1], tp[:, 2:]], axis=2)

    kw, vw = band(kr), band(v)
    qpos = jnp.arange(nb)[:, None] * BLOCK + jnp.arange(BLOCK)[None, :]
    kpos = (jnp.arange(nb)[:, None] - 1) * BLOCK + jnp.arange(3 * BLOCK)[None, :]
    valid = ((kpos[:, None, :] >= 0) & (kpos[:, None, :] < n_tok)
             & (jnp.abs(qpos[:, :, None] - kpos[:, None, :]) <= WINDOW))
    s_w = jnp.einsum('bnqkgd,bnskd->bnkgqs', qr, kw).astype(jnp.float32) * SCALE
    s_w = jnp.where(valid[None, :, None, None], s_w, NEG_INF)
    s_c = jnp.einsum('bnqkgd,bskd->bnkgqs', qp, k_ctx).astype(jnp.float32) * SCALE
    s_sink = jnp.broadcast_to(sink.astype(jnp.float32).reshape(N_KV_HEADS, Q_PER_KV, 1, 1),
                              s_w.shape[:-1] + (1,))
    p = jax.nn.softmax(jnp.concatenate([s_w, s_c, s_sink], axis=-1), axis=-1)
    p_w = p[..., :3 * BLOCK].astype(v.dtype)
    p_c = p[..., 3 * BLOCK:3 * BLOCK + n_ctx].astype(v.dtype)
    o = (jnp.einsum('bnkgqs,bnskd->bnqkgd', p_w, vw)
         + jnp.einsum('bnkgqs,bskd->bnqkgd', p_c, v_ctx))
    return o.reshape(bsz, n_tok, ATTN_WIDTH)


def context_attention(q, k, v, sink):
    bsz, n_ctx = q.shape[:2]
    qg = q.reshape(bsz, n_ctx, N_KV_HEADS, Q_PER_KV, HEAD_DIM)
    s = jnp.einsum('bqkgd,bskd->bkgqs', qg, k).astype(jnp.float32) * SCALE
    s_sink = jnp.broadcast_to(sink.astype(jnp.float32).reshape(N_KV_HEADS, Q_PER_KV, 1, 1),
                              s.shape[:-1] + (1,))
    p = jax.nn.softmax(jnp.concatenate([s, s_sink], axis=-1), axis=-1)[..., :n_ctx]
    o = jnp.einsum('bkgqs,bskd->bqkgd', p.astype(v.dtype), v)
    return o.reshape(bsz, n_ctx, ATTN_WIDTH)


def s5_discretize(a_re, a_im, log_dt, b_re, b_im):
    lam = lax.complex(a_re.astype(jnp.float32), a_im.astype(jnp.float32))
    dt = jnp.exp(log_dt.astype(jnp.float32))[..., None]
    a_bar = jnp.exp(lam * dt)
    b_bar = ((a_bar - 1.0) / lam)[..., None] * lax.complex(b_re.astype(jnp.float32), b_im.astype(jnp.float32))
    return a_bar, b_bar


def _linear_recurrence_op(e1, e2):
    a1, b1 = e1
    a2, b2 = e2
    return a1 * a2, a2 * b1 + b2


def s5_states(u, a_bar, b_bar, h0_fwd, h0_bwd):
    bsz, n_tok, _ = u.shape
    ug = u.astype(jnp.float32).reshape(bsz, n_tok, SSM_GROUPS, SSM_GROUP)
    states = []
    for direction, (reverse, h0) in enumerate(((False, h0_fwd), (True, h0_bwd))):
        bu = jnp.einsum('blgc,gpc->blgp', ug, b_bar[direction])
        if h0 is not None:
            bu = bu.at[:, -1 if reverse else 0].add(a_bar[direction] * h0)
        a = jnp.broadcast_to(a_bar[direction], bu.shape)
        _, h = lax.associative_scan(_linear_recurrence_op, (a, bu), reverse=reverse, axis=1)
        states.append(h)
    return states[0], states[1]


def s5_readout(u, h_fwd, h_bwd, c_re, c_im, d_skip, w_glu):
    bsz, n_tok, _ = u.shape
    c_mat = lax.complex(c_re.astype(jnp.float32), c_im.astype(jnp.float32))
    y = (jnp.real(jnp.einsum('blgp,gcp->blgc', h_fwd, c_mat[0]))
         + jnp.real(jnp.einsum('blgp,gcp->blgc', h_bwd, c_mat[1]))).reshape(bsz, n_tok, SSM_WIDTH)
    y = y + d_skip.astype(jnp.float32) * u.astype(jnp.float32)
    y = jax.nn.gelu(y).astype(u.dtype)
    return y * jax.nn.sigmoid(y @ w_glu)


def merge_groups(o_attn, o_ssm, g_oa, g_os, w_out):
    return jnp.concatenate([rms_norm(o_attn, g_oa), rms_norm(o_ssm, g_os)], axis=-1) @ w_out


def hybrid_mixer(hx, hc, cos, sin, w_in, sink, a_re, a_im, log_dt, b_re, b_im, c_re, c_im,
                 d_skip, w_glu, g_oa, g_os, w_out, last):
    bsz, n_tok, _ = hx.shape
    n_ctx = hc.shape[1]
    px = hx @ w_in
    qx = px[..., :ATTN_WIDTH].reshape(bsz, n_tok, N_Q_HEADS, HEAD_DIM)
    kx = px[..., ATTN_WIDTH:ATTN_WIDTH + KV_WIDTH].reshape(bsz, n_tok, N_KV_HEADS, HEAD_DIM)
    vx = px[..., ATTN_WIDTH + KV_WIDTH:ATTN_WIDTH + 2 * KV_WIDTH].reshape(bsz, n_tok, N_KV_HEADS, HEAD_DIM)
    ux = px[..., ATTN_WIDTH + 2 * KV_WIDTH:]
    pc = hc @ w_in[:, ATTN_WIDTH:]
    kc = pc[..., :KV_WIDTH].reshape(bsz, n_ctx, N_KV_HEADS, HEAD_DIM)
    vc = pc[..., KV_WIDTH:2 * KV_WIDTH].reshape(bsz, n_ctx, N_KV_HEADS, HEAD_DIM)
    uc = pc[..., 2 * KV_WIDTH:]
    a_bar, b_bar = s5_discretize(a_re, a_im, log_dt, b_re, b_im)
    hcf, hcb = s5_states(uc, a_bar, b_bar, None, None)
    hxf, hxb = s5_states(ux, a_bar, b_bar, hcf[:, -1], hcb[:, 0])
    o_x = merge_groups(latent_attention(qx, kx, vx, kc, vc, sink, cos, sin),
                       s5_readout(ux, hxf, hxb, c_re, c_im, d_skip, w_glu), g_oa, g_os, w_out)
    if last:
        return o_x, None
    qc = (hc @ w_in[:, :ATTN_WIDTH]).reshape(bsz, n_ctx, N_Q_HEADS, HEAD_DIM)
    o_c = merge_groups(context_attention(qc, kc, vc, sink),
                       s5_readout(uc, hcf, hcb, c_re, c_im, d_skip, w_glu), g_oa, g_os, w_out)
    return o_x, o_c


def swiglu(h, w_g, w_u, w_d):
    return (jax.nn.silu(h @ w_g) * (h @ w_u)) @ w_d


def moe_ffn(h, w_router, w_g, w_u, w_d):
    logits = (h @ w_router).astype(jnp.float32)
    top_v, top_i = lax.top_k(logits, TOP_K)
    top_w = jax.nn.softmax(top_v, axis=-1)
    gates = jnp.sum(jax.nn.one_hot(top_i, N_EXPERTS, dtype=jnp.float32) * top_w[..., None], axis=-2)
    gates = gates.astype(h.dtype)
    out = jnp.zeros_like(h)
    for e in range(N_EXPERTS):
        out = out + gates[..., e:e + 1] * swiglu(h, w_g[e], w_u[e], w_d[e])
    return out


def channel_mixer(h, layer, w_ff_gate, w_ff_up, w_ff_down, w_router, w_exp_gate, w_exp_up, w_exp_down):
    j = layer // 2
    if layer % 2 == 0:
        return swiglu(h, w_ff_gate[j], w_ff_up[j], w_ff_down[j])
    return moe_ffn(h, w_router[j], w_exp_gate[j], w_exp_up[j], w_exp_down[j])


def setup_inputs(seed: int = 0) -> dict:
    key = jax.random.key(seed)
    ks = jax.random.split(key, 32)
    f32 = jnp.float32

    def nrm(i, shape, scale):
        return jax.random.normal(ks[i], shape, f32) * scale

    n_idx = jnp.arange(SSM_STATE, dtype=f32)
    ssm_shape = (DEPTH, 2, SSM_GROUPS, SSM_STATE)
    return {
        "x": nrm(0, (BATCH, SEQ, D_MODEL), 1.0),
        "c": nrm(1, (BATCH, D_MODEL), 1.0),
        "ctx": nrm(2, (BATCH, CTX_LEN, D_MODEL), 1.0),
        "c_ctx": nrm(3, (D_MODEL,), 1.0),
        "w_mod": nrm(4, (DEPTH, D_MODEL, N_MOD * D_MODEL), 0.5 * D_MODEL ** -0.5),
        "b_mod": nrm(5, (DEPTH, N_MOD * D_MODEL), 0.01),
        "g_attn_norm": 1.0 + nrm(6, (DEPTH, D_MODEL), 0.02),
        "w_in": nrm(7, (DEPTH, D_MODEL, IN_COLS), D_MODEL ** -0.5),
        "attn_sink": nrm(8, (DEPTH, N_Q_HEADS), 0.5),
        "ssm_a_re": -0.5 * jnp.exp(nrm(9, ssm_shape, 0.01)),
        "ssm_a_im": math.pi * n_idx + nrm(10, ssm_shape, 0.01),
        "ssm_log_dt": jax.random.uniform(ks[11], (DEPTH, 2, SSM_GROUPS), f32,
                                         math.log(DT_MIN), math.log(DT_MAX)),
        "ssm_b_re": nrm(12, (DEPTH, 2, SSM_GROUPS, SSM_STATE, SSM_GROUP), (2 * SSM_GROUP) ** -0.5),
        "ssm_b_im": nrm(13, (DEPTH, 2, SSM_GROUPS, SSM_STATE, SSM_GROUP), (2 * SSM_GROUP) ** -0.5),
        "ssm_c_re": nrm(14, (DEPTH, 2, SSM_GROUPS, SSM_GROUP, SSM_STATE), (2 * SSM_STATE) ** -0.5),
        "ssm_c_im": nrm(15, (DEPTH, 2, SSM_GROUPS, SSM_GROUP, SSM_STATE), (2 * SSM_STATE) ** -0.5),
        "ssm_d": nrm(16, (DEPTH, SSM_WIDTH), 1.0),
        "w_glu": nrm(17, (DEPTH, SSM_WIDTH, SSM_WIDTH), SSM_WIDTH ** -0.5),
        "g_out_attn": 1.0 + nrm(18, (DEPTH, ATTN_WIDTH), 0.02),
        "g_out_ssm": 1.0 + nrm(19, (DEPTH, SSM_WIDTH), 0.02),
        "w_out": nrm(20, (DEPTH, MIX_WIDTH, D_MODEL), MIX_WIDTH ** -0.5),
        "g_ffn_norm": 1.0 + nrm(21, (DEPTH, D_MODEL), 0.02),
        "w_ff_gate": nrm(22, (N_DENSE, D_MODEL, D_FF), D_MODEL ** -0.5),
        "w_ff_up": nrm(23, (N_DENSE, D_MODEL, D_FF), D_MODEL ** -0.5),
        "w_ff_down": nrm(24, (N_DENSE, D_FF, D_MODEL), D_FF ** -0.5),
        "w_router": nrm(25, (N_MOE, D_MODEL, N_EXPERTS), D_MODEL ** -0.5),
        "w_exp_gate": nrm(26, (N_MOE, N_EXPERTS, D_MODEL, D_EXPERT), D_MODEL ** -0.5),
        "w_exp_up": nrm(27, (N_MOE, N_EXPERTS, D_MODEL, D_EXPERT), D_MODEL ** -0.5),
        "w_exp_down": nrm(28, (N_MOE, N_EXPERTS, D_EXPERT, D_MODEL), D_EXPERT ** -0.5),
        "g_final": 1.0 + nrm(29, (D_MODEL,), 0.02),
    }


def reference(x, c, ctx, c_ctx, w_mod, b_mod, g_attn_norm, w_in, attn_sink, ssm_a_re, ssm_a_im,
              ssm_log_dt, ssm_b_re, ssm_b_im, ssm_c_re, ssm_c_im, ssm_d, w_glu, g_out_attn,
              g_out_ssm, w_out, g_ffn_norm, w_ff_gate, w_ff_up, w_ff_down, w_router, w_exp_gate,
              w_exp_up, w_exp_down, g_final):
    n_tok = x.shape[1]
    cos, sin = axial_rope_tables(n_tok)
    xc = ctx
    for i in range(DEPTH):
        last = i == DEPTH - 1
        mx = [t[:, None, :] for t in ada_modulation(c, w_mod[i], b_mod[i])]
        mc = ada_modulation(c_ctx, w_mod[i], b_mod[i])
        hx = modulate(x, g_attn_norm[i], mx[0], mx[1])
        hc = modulate(xc, g_attn_norm[i], mc[0], mc[1])
        o_x, o_c = hybrid_mixer(hx, hc, cos, sin, w_in[i], attn_sink[i], ssm_a_re[i], ssm_a_im[i],
                                ssm_log_dt[i], ssm_b_re[i], ssm_b_im[i], ssm_c_re[i], ssm_c_im[i],
                                ssm_d[i], w_glu[i], g_out_attn[i], g_out_ssm[i], w_out[i], last)
        x = x + mx[2] * o_x
        x = x + mx[5] * channel_mixer(modulate(x, g_ffn_norm[i], mx[3], mx[4]), i, w_ff_gate, w_ff_up,
                                      w_ff_down, w_router, w_exp_gate, w_exp_up, w_exp_down)
        if not last:
            xc = xc + mc[2] * o_c
            xc = xc + mc[5] * channel_mixer(modulate(xc, g_ffn_norm[i], mc[3], mc[4]), i, w_ff_gate,
                                            w_ff_up, w_ff_down, w_router, w_exp_gate, w_exp_up,
                                            w_exp_down)
    return rms_norm(x, g_final)
```

```python
import functools
import math

import jax
import jax.numpy as jnp
from jax import lax
from jax.experimental import pallas as pl
from jax.experimental.pallas import tpu as pltpu

F32 = jnp.float32
BF16 = jnp.bfloat16

HEAD_DIM = 128
BLOCK = 128
Q_PER_KV = 3
GRID_W = 64
ROPE_BASE = 10000.0
SSM_GROUP = 16
N_MOD = 6
TOP_K = 2
EPS = 1e-6
NEG_INF = -1e30

LANES = 128
SUBLANES = 8
VMEM_BYTES_V7X = 64 * 1024 * 1024
VMEM_CAP = VMEM_BYTES_V7X - 8 * 1024 * 1024

SSM_CHUNK = 16


def _params(sem, est_bytes):
    limit = int(min(VMEM_CAP, max(32 * 1024 * 1024, est_bytes * 5 // 4)))
    return pltpu.CompilerParams(dimension_semantics=sem, vmem_limit_bytes=limit)


def _nbytes(shape, dtype):
    return math.prod(shape) * jnp.dtype(dtype).itemsize


def _mod_row(i, tm, seq, batch):
    return jnp.minimum((i * tm) // seq, batch)


def _mod_kernel(cb_ref, w_ref, b_ref, o_ref, *, n_rows, tn):
    k_dim = w_ref.shape[0]
    n_chunks = tn // LANES

    def body(kb, accs):
        k0 = pl.multiple_of(kb * SUBLANES, SUBLANES)
        wv = w_ref[pl.ds(k0, SUBLANES), :]
        new = []
        for r in range(n_rows):
            cv = cb_ref[r, pl.ds(k0, SUBLANES), :]
            sv = cv * jax.nn.sigmoid(cv)
            for c in range(n_chunks):
                new.append(accs[r * n_chunks + c] + wv[:, c * LANES:(c + 1) * LANES] * sv)
        return tuple(new)

    init = tuple(jnp.zeros((SUBLANES, LANES), F32) for _ in range(n_rows * n_chunks))
    accs = lax.fori_loop(0, k_dim // SUBLANES, body, init, unroll=4)
    o_ref[...] = jnp.zeros_like(o_ref)
    for r in range(n_rows):
        row = jnp.concatenate(
            [jnp.sum(accs[r * n_chunks + c], axis=0, keepdims=True) for c in range(n_chunks)], axis=1)
        o_ref[r:r + 1, :] = row + b_ref[...]


def ada_modulation(cond_rows, w_mod, b_mod):
    n_rows, d = cond_rows.shape
    n = w_mod.shape[1]
    tn = 512
    cb = jnp.broadcast_to(cond_rows[:, :, None], (n_rows, d, LANES))
    est = 2 * (_nbytes((d, tn), F32) + _nbytes((n_rows, d, LANES), F32))
    out = pl.pallas_call(
        functools.partial(_mod_kernel, n_rows=n_rows, tn=tn),
        grid=(n // tn,),
        in_specs=[pl.BlockSpec((n_rows, d, LANES), lambda j: (0, 0, 0)),
                  pl.BlockSpec((d, tn), lambda j: (0, j)),
                  pl.BlockSpec((1, tn), lambda j: (0, j))],
        out_specs=pl.BlockSpec((SUBLANES, tn), lambda j: (0, j)),
        out_shape=jax.ShapeDtypeStruct((SUBLANES, n), F32),
        compiler_params=_params(("parallel",), est),
        name="ada_modulation",
    )(cb, w_mod, b_mod.reshape(1, n))
    return out.reshape(SUBLANES, 1, n)


def _split3(v):
    hi = v.astype(BF16)
    r1 = v - hi.astype(F32)
    mid = r1.astype(BF16)
    lo = (r1 - mid.astype(F32)).astype(BF16)
    return hi, mid, lo


def _normmod_kernel(x_ref, g_ref, shift_ref, scale_ref, *rest, with_router):
    xf = x_ref[...]
    y = xf * lax.rsqrt(jnp.mean(xf * xf, axis=-1, keepdims=True) + EPS)
    h = (y * g_ref[...]) * (1.0 + scale_ref[...]) + shift_ref[...]
    if with_router:
        wr_ref, o_ref, lg_ref = rest
        hs = _split3(h)
        ws = _split3(wr_ref[...])
        acc = None
        for a, b in ((0, 0), (0, 1), (1, 0), (1, 1), (0, 2), (2, 0)):
            t = jnp.dot(hs[a], ws[b], preferred_element_type=F32)
            acc = t if acc is None else acc + t
        lg_ref[...] = acc
    else:
        (o_ref,) = rest
    o_ref[...] = h.astype(BF16)


def norm_modulate(x, g, mod, shift_idx, scale_idx, *, rows, seq, batch, w_router=None):
    d = x.shape[1]
    tm = 256
    with_router = w_router is not None
    in_specs = [pl.BlockSpec((tm, d), lambda i: (i, 0)),
                pl.BlockSpec((1, d), lambda i: (0, 0)),
                pl.BlockSpec((None, 1, d), lambda i: (_mod_row(i, tm, seq, batch), 0, shift_idx)),
                pl.BlockSpec((None, 1, d), lambda i: (_mod_row(i, tm, seq, batch), 0, scale_idx))]
    args = [x, g.reshape(1, d), mod, mod]
    out_specs = [pl.BlockSpec((tm, d), lambda i: (i, 0))]
    out_shape = [jax.ShapeDtypeStruct((rows, d), BF16)]
    if with_router:
        in_specs.append(pl.BlockSpec((d, LANES), lambda i: (0, 0)))
        args.append(w_router)
        out_specs.append(pl.BlockSpec((tm, LANES), lambda i: (i, 0)))
        out_shape.append(jax.ShapeDtypeStruct((rows, LANES), F32))
    est = 2 * (_nbytes((tm, d), F32) + _nbytes((tm, d), BF16)) + 8 * _nbytes((tm, d), F32)
    outs = pl.pallas_call(
        functools.partial(_normmod_kernel, with_router=with_router),
        grid=(pl.cdiv(rows, tm),),
        in_specs=in_specs, out_specs=out_specs, out_shape=out_shape,
        compiler_params=_params(("parallel",), est),
        name="norm_modulate_router" if with_router else "norm_modulate",
    )(*args)
    return outs if with_router else outs[0]


def _mm_kernel(x_ref, w_ref, o_ref):
    o_ref[...] = jnp.dot(x_ref[...], w_ref[...], preferred_element_type=F32).astype(o_ref.dtype)


def _mm_swiglu_kernel(x_ref, wg_ref, wu_ref, o_ref):
    x = x_ref[...]
    a = jnp.dot(x, wg_ref[...], preferred_element_type=F32)
    b = jnp.dot(x, wu_ref[...], preferred_element_type=F32)
    o_ref[...] = (a * jax.nn.sigmoid(a) * b).astype(o_ref.dtype)


def _mm_resid_kernel(x_ref, w_ref, r_ref, gate_ref, o_ref):
    acc = jnp.dot(x_ref[...], w_ref[...], preferred_element_type=F32)
    o_ref[...] = r_ref[...] + gate_ref[...] * acc


def _mm_glu_kernel(y_ref, w_ref, yt_ref, o_ref):
    z = jnp.dot(y_ref[...].astype(BF16), w_ref[...], preferred_element_type=F32)
    o_ref[...] = yt_ref[...] * jax.nn.sigmoid(z)


def matmul(x, w, *, rows, tm, tn, out_dtype):
    k, n = w.shape
    est = 2 * (_nbytes((tm, k), x.dtype) + _nbytes((k, tn), w.dtype) + _nbytes((tm, tn), out_dtype))
    return pl.pallas_call(
        _mm_kernel,
        grid=(pl.cdiv(rows, tm), pl.cdiv(n, tn)),
        in_specs=[pl.BlockSpec((tm, k), lambda i, j: (i, 0)),
                  pl.BlockSpec((k, tn), lambda i, j: (0, j))],
        out_specs=pl.BlockSpec((tm, tn), lambda i, j: (i, j)),
        out_shape=jax.ShapeDtypeStruct((rows, n), out_dtype),
        compiler_params=_params(("parallel", "parallel"), est),
        name="matmul",
    )(x, w)


def matmul_swiglu(x, wg, wu, *, rows, tm, tn):
    k, n = wg.shape
    est = 2 * (_nbytes((tm, k), BF16) + 2 * _nbytes((k, tn), BF16) + _nbytes((tm, tn), BF16)) \
        + 3 * _nbytes((tm, tn), F32)
    return pl.pallas_call(
        _mm_swiglu_kernel,
        grid=(pl.cdiv(rows, tm), pl.cdiv(n, tn)),
        in_specs=[pl.BlockSpec((tm, k), lambda i, j: (i, 0)),
                  pl.BlockSpec((k, tn), lambda i, j: (0, j)),
                  pl.BlockSpec((k, tn), lambda i, j: (0, j))],
        out_specs=pl.BlockSpec((tm, tn), lambda i, j: (i, j)),
        out_shape=jax.ShapeDtypeStruct((rows, n), BF16),
        compiler_params=_params(("parallel", "parallel"), est),
        name="matmul_swiglu",
    )(x, wg, wu)


def matmul_gated_residual(x, w, resid, mod, gate_idx, *, rows, tm, tn, seq, batch, weight_outer):
    k, n = w.shape
    nj = n // tn
    if weight_outer:
        grid = (nj, pl.cdiv(rows, tm))
        ij = lambda a, b: (b, a)
    else:
        grid = (pl.cdiv(rows, tm), nj)
        ij = lambda a, b: (a, b)

    def gate_map(a, b):
        i, j = ij(a, b)
        return (_mod_row(i, tm, seq, batch), 0, gate_idx * nj + j)

    est = 2 * (_nbytes((tm, k), BF16) + _nbytes((k, tn), BF16) + 2 * _nbytes((tm, tn), F32))
    return pl.pallas_call(
        _mm_resid_kernel,
        grid=grid,
        in_specs=[pl.BlockSpec((tm, k), lambda a, b: (ij(a, b)[0], 0)),
                  pl.BlockSpec((k, tn), lambda a, b: (0, ij(a, b)[1])),
                  pl.BlockSpec((tm, tn), lambda a, b: ij(a, b)),
                  pl.BlockSpec((None, 1, tn), gate_map)],
        out_specs=pl.BlockSpec((tm, tn), lambda a, b: ij(a, b)),
        out_shape=jax.ShapeDtypeStruct((rows, n), F32),
        compiler_params=_params(("parallel", "parallel"), est),
        name="matmul_gated_residual",
    )(x, w, resid, mod)


def matmul_glu(y, w, *, rows, tm, tn):
    k, n = w.shape
    est = 2 * (_nbytes((tm, k), F32) + _nbytes((k, tn), BF16) + _nbytes((tm, tn), F32))
    return pl.pallas_call(
        _mm_glu_kernel,
        grid=(pl.cdiv(rows, tm), n // tn),
        in_specs=[pl.BlockSpec((tm, k), lambda i, j: (i, 0)),
                  pl.BlockSpec((k, tn), lambda i, j: (0, j)),
                  pl.BlockSpec((tm, tn), lambda i, j: (i, j))],
        out_specs=pl.BlockSpec((tm, tn), lambda i, j: (i, j)),
        out_shape=jax.ShapeDtypeStruct((rows, n), F32),
        compiler_params=_params(("parallel", "parallel"), est),
        name="matmul_glu",
    )(y, w, y)


def _rope(x, cos, sin_lo, sin_hi):
    return (x * cos + pltpu.roll(x, HEAD_DIM - HEAD_DIM // 4, 1) * sin_lo
            + pltpu.roll(x, HEAD_DIM // 4, 1) * sin_hi)


def _nt_dot(a, b):
    return lax.dot_general(a, b, (((1,), (1,)), ((), ())), preferred_element_type=F32)


def _attn_kernel(sink_ref, q_ref, *rest, window, n_blocks, scale):
    if window:
        (kp_ref, k0_ref, kn_ref, vp_ref, v0_ref, vn_ref, kc_ref, vc_ref,
         cp_ref, c0_ref, cn_ref, ap_ref, a0_ref, an_ref, bp_ref, b0_ref, bn_ref, o_ref) = rest
    else:
        kc_ref, vc_ref, o_ref = rest
    n = pl.program_id(1)
    kvh = pl.program_id(2)
    rows = Q_PER_KV * BLOCK

    q = q_ref[...]
    qs = [q[:, g * HEAD_DIM:(g + 1) * HEAD_DIM] for g in range(Q_PER_KV)]
    q_plain = jnp.concatenate(qs, axis=0).astype(BF16)
    sink = jnp.concatenate(
        [jnp.full((BLOCK, 1), sink_ref[kvh * Q_PER_KV + g], F32) for g in range(Q_PER_KV)], axis=0)

    s_c = _nt_dot(q_plain, kc_ref[...].astype(BF16)) * scale
    m = jnp.maximum(jnp.max(s_c, axis=1, keepdims=True), sink)
    if window:
        c0, a0, b0 = c0_ref[...], a0_ref[...], b0_ref[...]
        q_rot = jnp.concatenate([_rope(x, c0, a0, b0) for x in qs], axis=0).astype(BF16)
        k_win = jnp.concatenate([_rope(kp_ref[...], cp_ref[...], ap_ref[...], bp_ref[...]),
                                 _rope(k0_ref[...], c0, a0, b0),
                                 _rope(kn_ref[...], cn_ref[...], an_ref[...], bn_ref[...])],
                                axis=0).astype(BF16)
        s_w = _nt_dot(q_rot, k_win) * scale
        row = lax.broadcasted_iota(jnp.int32, (rows, 3 * BLOCK), 0) & (BLOCK - 1)
        col = lax.broadcasted_iota(jnp.int32, (rows, 3 * BLOCK), 1)
        valid = (col >= row) & (col <= row + 2 * BLOCK)
        valid &= (col >= BLOCK) | (n > 0)
        valid &= (col < 2 * BLOCK) | (n < n_blocks - 1)
        s_w = jnp.where(valid, s_w, NEG_INF)
        m = jnp.maximum(m, jnp.max(s_w, axis=1, keepdims=True))
    p_c = jnp.exp(s_c - m)
    den = jnp.sum(p_c, axis=1, keepdims=True) + jnp.exp(sink - m)
    o = jnp.dot(p_c.astype(BF16), vc_ref[...].astype(BF16), preferred_element_type=F32)
    if window:
        p_w = jnp.exp(s_w - m)
        den = den + jnp.sum(p_w, axis=1, keepdims=True)
        v_win = jnp.concatenate([vp_ref[...], v0_ref[...], vn_ref[...]], axis=0).astype(BF16)
        o = o + jnp.dot(p_w.astype(BF16), v_win, preferred_element_type=F32)
    o = o / den
    for g in range(Q_PER_KV):
        o_ref[:, g * HEAD_DIM:(g + 1) * HEAD_DIM] = o[g * BLOCK:(g + 1) * BLOCK]


def latent_attention(p, sink, rope_tabs, *, batch, seq, n_ctx, n_kv):
    rows_total = p.shape[0]
    nb = seq // BLOCK
    qw = Q_PER_KV * HEAD_DIM
    k_col0 = n_kv * Q_PER_KV
    v_col0 = k_col0 + n_kv
    ctx_blk0 = batch * seq // n_ctx
    cos, sin_lo, sin_hi = rope_tabs

    def prev(n):
        return jnp.maximum(n - 1, 0)

    def nxt(n):
        return jnp.minimum(n + 1, nb - 1)

    def kv_spec(col0, shift):
        return pl.BlockSpec((BLOCK, HEAD_DIM), lambda b, n, h, s: (b * nb + shift(n), col0 + h))

    def tab_spec(shift):
        return pl.BlockSpec((BLOCK, HEAD_DIM), lambda b, n, h, s: (shift(n), 0))

    same = lambda n: n
    in_specs = [pl.BlockSpec((BLOCK, qw), lambda b, n, h, s: (b * nb + n, h)),
                kv_spec(k_col0, prev), kv_spec(k_col0, same), kv_spec(k_col0, nxt),
                kv_spec(v_col0, prev), kv_spec(v_col0, same), kv_spec(v_col0, nxt),
                pl.BlockSpec((n_ctx, HEAD_DIM), lambda b, n, h, s: (ctx_blk0 + b, k_col0 + h)),
                pl.BlockSpec((n_ctx, HEAD_DIM), lambda b, n, h, s: (ctx_blk0 + b, v_col0 + h))]
    in_specs += [tab_spec(prev), tab_spec(same), tab_spec(nxt)] * 3
    args = [p] * 9 + [cos] * 3 + [sin_lo] * 3 + [sin_hi] * 3
    grid_spec = pltpu.PrefetchScalarGridSpec(
        num_scalar_prefetch=1, grid=(batch, nb, n_kv), in_specs=in_specs,
        out_specs=pl.BlockSpec((BLOCK, qw), lambda b, n, h, s: (b * nb + n, h)))
    return pl.pallas_call(
        functools.partial(_attn_kernel, window=True, n_blocks=nb, scale=HEAD_DIM ** -0.5),
        grid_spec=grid_spec,
        out_shape=jax.ShapeDtypeStruct((rows_total, n_kv * qw), F32),
        compiler_params=_params(("parallel", "parallel", "parallel"), 8 * 1024 * 1024),
        name="latent_attention",
    )(sink, *args)


def context_attention(p, o_attn, sink, *, batch, seq, n_ctx, n_kv):
    qw = Q_PER_KV * HEAD_DIM
    k_col0 = n_kv * Q_PER_KV
    v_col0 = k_col0 + n_kv
    ctx_blk0 = batch * seq // n_ctx
    qb = n_ctx // BLOCK
    q_blk0 = batch * seq // BLOCK
    in_specs = [pl.BlockSpec((BLOCK, qw), lambda b, n, h, s: (q_blk0 + b * qb + n, h)),
                pl.BlockSpec((n_ctx, HEAD_DIM), lambda b, n, h, s: (ctx_blk0 + b, k_col0 + h)),
                pl.BlockSpec((n_ctx, HEAD_DIM), lambda b, n, h, s: (ctx_blk0 + b, v_col0 + h)),
                pl.BlockSpec(memory_space=pl.ANY)]
    grid_spec = pltpu.PrefetchScalarGridSpec(
        num_scalar_prefetch=1, grid=(batch, qb, n_kv), in_specs=in_specs,
        out_specs=pl.BlockSpec((BLOCK, qw), lambda b, n, h, s: (q_blk0 + b * qb + n, h)))

    def kern(sink_ref, q_ref, kc_ref, vc_ref, alias_ref, o_ref):
        del alias_ref
        _attn_kernel(sink_ref, q_ref, kc_ref, vc_ref, o_ref, window=False, n_blocks=qb,
                     scale=HEAD_DIM ** -0.5)

    return pl.pallas_call(
        kern, grid_spec=grid_spec,
        out_shape=jax.ShapeDtypeStruct(o_attn.shape, F32),
        input_output_aliases={4: 0},
        compiler_params=_params(("parallel", "parallel", "parallel"), 8 * 1024 * 1024),
        name="context_attention",
    )(sink, p, p, p, o_attn)


def _ssm_tables(a_re, a_im, log_dt, b_re, b_im, c_re, c_im):
    hp = lax.Precision.HIGHEST
    t_len = SSM_CHUNK
    g_per = LANES // SSM_GROUP
    n_groups, n_state = a_re.shape[1:]
    n_lb = n_groups // g_per
    dt = jnp.exp(log_dt.astype(F32))[..., None]
    lam_re, lam_im = a_re.astype(F32), a_im.astype(F32)
    mag = jnp.exp(lam_re * dt)
    abar = lax.complex(mag * jnp.cos(lam_im * dt), mag * jnp.sin(lam_im * dt))
    lam = lax.complex(lam_re, lam_im)
    bbar = ((abar - 1.0) / lam)[..., None] * lax.complex(b_re.astype(F32), b_im.astype(F32))
    cmat = lax.complex(c_re.astype(F32), c_im.astype(F32))

    def powers(d, exps):
        e = jnp.asarray(exps, F32)[:, None, None]
        m = jnp.exp(lam_re[d] * dt[d] * e)
        ph = lam_im[d] * dt[d] * e
        return lax.complex(m * jnp.cos(ph), m * jnp.sin(ph))

    ar = list(range(t_len + 1))
    pw_f, pw_b = powers(0, ar), powers(1, ar)

    kf = jnp.real(jnp.einsum('gcp,tgp,gpi->tgci', cmat[0], pw_f[:t_len], bbar[0], precision=hp))
    kb = jnp.real(jnp.einsum('gcp,tgp,gpi->tgci', cmat[1], pw_b[:t_len], bbar[1], precision=hp))
    lag = jnp.arange(t_len)[None, :] - jnp.arange(t_len)[:, None]
    k_jl = jnp.where((lag >= 0)[:, :, None, None, None], kf[jnp.clip(lag, 0)], 0.0) \
        + jnp.where((lag <= 0)[:, :, None, None, None], kb[jnp.clip(-lag, 0)], 0.0)
    eye = jnp.eye(g_per, dtype=F32)

    def blockdiag(t):
        lead = t.shape[:-3]
        a, b = t.shape[-2:]
        t = t.reshape(lead + (n_lb, g_per, a, b))
        t = jnp.einsum('...ngab,gh->...ngahb', t, eye)
        t = t.reshape(lead + (n_lb, g_per * a, g_per * b))
        return jnp.moveaxis(t, len(lead), 0)

    m_intra = blockdiag(jnp.swapaxes(k_jl, -1, -2))
    m_intra = m_intra.transpose(0, 1, 3, 2, 4).reshape(n_lb, t_len * LANES, t_len * LANES)

    def e_part(d, pw_sel):
        coef = pw_sel[..., None] * bbar[d][None]
        coef = jnp.swapaxes(coef, -1, -2)
        re = blockdiag(jnp.real(coef))
        im = blockdiag(jnp.imag(coef))
        return jnp.concatenate([re, im], axis=-1).reshape(n_lb, t_len * LANES, 2 * g_per * n_state)

    e_mat = jnp.concatenate([e_part(0, pw_f[:t_len][::-1]), e_part(1, pw_b[:t_len])], axis=-1)

    def f_part(d, pw_sel):
        coef = cmat[d][None] * pw_sel[:, :, None, :]
        coef = jnp.swapaxes(coef, -1, -2)
        re = blockdiag(jnp.real(coef))
        im = blockdiag(-jnp.imag(coef))
        m = jnp.concatenate([re, im], axis=-2)
        return m.transpose(0, 2, 1, 3).reshape(n_lb, 2 * g_per * n_state, t_len * LANES)

    f_mat = jnp.concatenate([f_part(0, pw_f[1:]), f_part(1, pw_b[1:][::-1])], axis=-2)
    w_out = jnp.concatenate([m_intra, f_mat], axis=-2)

    def lay(v):
        return v.reshape(n_lb, g_per * n_state)

    a_chunk = jnp.concatenate([lay(jnp.real(pw_f[t_len])), lay(jnp.imag(pw_f[t_len])),
                               lay(jnp.real(pw_b[t_len])), lay(jnp.imag(pw_b[t_len]))], axis=-1)
    return e_mat.astype(BF16), w_out.astype(BF16), a_chunk.reshape(1, -1)


def _ssm_local_kernel(*refs):
    x_refs, e_ref, s_ref = refs[:SSM_CHUNK], refs[SSM_CHUNK], refs[SSM_CHUNK + 1]
    xs = jnp.concatenate([r[...].astype(BF16) for r in x_refs], axis=1)
    s_ref[...] = jnp.dot(xs, e_ref[...], preferred_element_type=F32)


def _ssm_scan_kernel(s_ref, a_ref, h_ref, *, batch, nc_x, nc_c):
    sw = s_ref.shape[1] // 4
    a = a_ref[...]
    afr, afi, abr, abi = (a[:, k * sw:(k + 1) * sw] for k in range(4))
    ctx0 = batch * nc_x

    def step(row_f, row_b, st):
        new = []
        for (row, ar, ai, off, (hr, hi)) in ((row_f, afr, afi, 0, st[0]), (row_b, abr, abi, 2 * sw, st[1])):
            h_ref[pl.ds(row, 1), off:off + sw] = hr
            h_ref[pl.ds(row, 1), off + sw:off + 2 * sw] = hi
            sr = s_ref[pl.ds(row, 1), off:off + sw]
            si = s_ref[pl.ds(row, 1), off + sw:off + 2 * sw]
            new.append((ar * hr - ai * hi + sr, ar * hi + ai * hr + si))
        return tuple(new)

    zero = jnp.zeros((1, sw), F32)
    for b in range(batch):
        st = ((zero, zero), (zero, zero))
        c_base = ctx0 + b * nc_c
        st = lax.fori_loop(0, nc_c, lambda t, s: step(c_base + t, c_base + nc_c - 1 - t, s), st)
        x_base = b * nc_x
        lax.fori_loop(0, nc_x, lambda t, s: step(x_base + t, x_base + nc_x - 1 - t, s), st)


def _ssm_out_kernel(*refs):
    t = SSM_CHUNK
    x_refs, h_ref, w_ref, d_ref, o_ref = refs[:t], refs[t], refs[t + 1], refs[t + 2], refs[t + 3]
    xs = [r[...] for r in x_refs]
    lhs = jnp.concatenate([x.astype(BF16) for x in xs] + [h_ref[...].astype(BF16)], axis=1)
    y = jnp.dot(lhs, w_ref[...], preferred_element_type=F32)
    d = d_ref[...]
    for l in range(t):
        o_ref[:, l * LANES:(l + 1) * LANES] = jax.nn.gelu(y[:, l * LANES:(l + 1) * LANES] + d * xs[l])


def s5_mix(p, tables, d_skip, *, batch, seq, n_ctx, u_col0, ssm_width):
    e_mat, w_out, a_chunk = tables
    rows, in_cols = p.shape
    t = SSM_CHUNK
    n_lb = ssm_width // LANES
    rc = rows // t
    p2 = p.reshape(rc, t * in_cols)
    sdim = e_mat.shape[-1]
    tmc = max(m for m in range(SUBLANES, min(rc, 264) + 1, SUBLANES) if rc % m == 0)
    cb_in = in_cols // LANES
    cb_u = u_col0 // LANES

    x_specs = [pl.BlockSpec((tmc, LANES), lambda g, r, j=j: (r, j * cb_in + cb_u + g)) for j in range(t)]
    est_x = 2 * t * _nbytes((tmc, LANES), F32)
    s_loc = pl.pallas_call(
        _ssm_local_kernel,
        grid=(n_lb, rc // tmc),
        in_specs=x_specs + [pl.BlockSpec((None, t * LANES, sdim), lambda g, r: (g, 0, 0))],
        out_specs=pl.BlockSpec((tmc, sdim), lambda g, r: (r, g)),
        out_shape=jax.ShapeDtypeStruct((rc, n_lb * sdim), F32),
        compiler_params=_params(("parallel", "parallel"),
                                est_x + 2 * _nbytes((t * LANES, sdim), BF16) + 3 * _nbytes((tmc, sdim), F32)),
        name="s5_local_states",
    )(*([p2] * t), e_mat)

    h_in = pl.pallas_call(
        functools.partial(_ssm_scan_kernel, batch=batch, nc_x=seq // t, nc_c=n_ctx // t),
        grid=(n_lb,),
        in_specs=[pl.BlockSpec((rc, sdim), lambda g: (0, g)),
                  pl.BlockSpec((1, sdim), lambda g: (0, g))],
        out_specs=pl.BlockSpec((rc, sdim), lambda g: (0, g)),
        out_shape=jax.ShapeDtypeStruct((rc, n_lb * sdim), F32),
        compiler_params=_params(("parallel",), 4 * _nbytes((rc, sdim), F32)),
        name="s5_chunk_scan",
    )(s_loc, a_chunk)

    y2 = pl.pallas_call(
        _ssm_out_kernel,
        grid=(n_lb, rc // tmc),
        in_specs=x_specs + [pl.BlockSpec((tmc, sdim), lambda g, r: (r, g)),
                            pl.BlockSpec((None, t * LANES + sdim, t * LANES), lambda g, r: (g, 0, 0)),
                            pl.BlockSpec((1, LANES), lambda g, r: (0, g))],
        out_specs=pl.BlockSpec((tmc, t * LANES), lambda g, r: (r, g)),
        out_shape=jax.ShapeDtypeStruct((rc, n_lb * t * LANES), F32),
        compiler_params=_params(("parallel", "parallel"),
                                2 * est_x + 2 * _nbytes((t * LANES + sdim, t * LANES), BF16)
                                + 4 * _nbytes((tmc, sdim), F32)),
        name="s5_readout",
    )(*([p2] * t), h_in, w_out, d_skip.reshape(1, ssm_width))
    return y2.reshape(rc, n_lb, t, LANES).transpose(0, 2, 1, 3).reshape(rows, ssm_width)


def _rms(x, g):
    return x * lax.rsqrt(jnp.mean(x * x, axis=-1, keepdims=True) + EPS) * g


def _merge_norm_kernel(oa_ref, os_ref, ga_ref, gs_ref, o_ref):
    wa = oa_ref.shape[1]
    o_ref[:, :wa] = _rms(oa_ref[...], ga_ref[...]).astype(BF16)
    o_ref[:, wa:] = _rms(os_ref[...], gs_ref[...]).astype(BF16)


def merge_norm(o_attn, o_ssm, g_oa, g_os, *, rows):
    wa, ws = o_attn.shape[1], o_ssm.shape[1]
    tm = 256
    est = 4 * _nbytes((tm, wa + ws), F32)
    return pl.pallas_call(
        _merge_norm_kernel,
        grid=(pl.cdiv(rows, tm),),
        in_specs=[pl.BlockSpec((tm, wa), lambda i: (i, 0)),
                  pl.BlockSpec((tm, ws), lambda i: (i, 0)),
                  pl.BlockSpec((1, wa), lambda i: (0, 0)),
                  pl.BlockSpec((1, ws), lambda i: (0, 0))],
        out_specs=pl.BlockSpec((tm, wa + ws), lambda i: (i, 0)),
        out_shape=jax.ShapeDtypeStruct((rows, wa + ws), BF16),
        compiler_params=_params(("parallel",), est),
        name="merge_norm",
    )(o_attn, o_ssm, g_oa.reshape(1, wa), g_os.reshape(1, ws))


def _final_norm_kernel(x_ref, g_ref, o_ref):
    o_ref[...] = _rms(x_ref[...], g_ref[...])


def final_norm(x, g, *, rows):
    d = x.shape[1]
    tm = 256
    return pl.pallas_call(
        _final_norm_kernel,
        grid=(pl.cdiv(rows, tm),),
        in_specs=[pl.BlockSpec((tm, d), lambda i: (i, 0)), pl.BlockSpec((1, d), lambda i: (0, 0))],
        out_specs=pl.BlockSpec((tm, d), lambda i: (i, 0)),
        out_shape=jax.ShapeDtypeStruct((rows, d), F32),
        compiler_params=_params(("parallel",), 6 * _nbytes((tm, d), F32)),
        name="final_norm",
    )(x, g.reshape(1, d))


def _moe_up_kernel(te_ref, nu_ref, x_ref, wg_ref, wu_ref, o_ref):
    t = pl.program_id(1)

    @pl.when(t < nu_ref[0])
    def _():
        x = x_ref[...]
        a = jnp.dot(x, wg_ref[...], preferred_element_type=F32)
        b = jnp.dot(x, wu_ref[...], preferred_element_type=F32)
        o_ref[...] = (a * jax.nn.sigmoid(a) * b).astype(o_ref.dtype)

    @pl.when(t >= nu_ref[0])
    def _():
        o_ref[...] = jnp.zeros_like(o_ref)


def _moe_down_kernel(te_ref, nu_ref, h_ref, wd_ref, rw_ref, o_ref):
    t = pl.program_id(1)

    @pl.when(t < nu_ref[0])
    def _():
        acc = jnp.dot(h_ref[...], wd_ref[...], preferred_element_type=F32)
        o_ref[...] = acc * rw_ref[...]

    @pl.when(t >= nu_ref[0])
    def _():
        o_ref[...] = jnp.zeros_like(o_ref)


def moe_experts(xg, row_w, tile_expert, n_used, w_g, w_u, w_d, *, tmr, tn):
    r_rows, d = xg.shape
    d_e = w_g.shape[2]
    n_tiles = r_rows // tmr
    est = 2 * (_nbytes((tmr, d), BF16) + 2 * _nbytes((d, tn), BF16) + _nbytes((tmr, tn), BF16)) \
        + 3 * _nbytes((tmr, tn), F32)
    hidden = pl.pallas_call(
        _moe_up_kernel,
        grid_spec=pltpu.PrefetchScalarGridSpec(
            num_scalar_prefetch=2, grid=(d_e // tn, n_tiles),
            in_specs=[pl.BlockSpec((tmr, d), lambda j, t, te, nu: (t, 0)),
                      pl.BlockSpec((None, d, tn), lambda j, t, te, nu: (te[t], 0, j)),
                      pl.BlockSpec((None, d, tn), lambda j, t, te, nu: (te[t], 0, j))],
            out_specs=pl.BlockSpec((tmr, tn), lambda j, t, te, nu: (t, j))),
        out_shape=jax.ShapeDtypeStruct((r_rows, d_e), BF16),
        compiler_params=_params(("parallel", "parallel"), est),
        name="moe_gate_up",
    )(tile_expert, n_used, xg, w_g, w_u)
    est = 2 * (_nbytes((tmr, d_e), BF16) + _nbytes((d_e, tn), BF16) + _nbytes((tmr, tn), F32)) \
        + 2 * _nbytes((tmr, tn), F32)
    return pl.pallas_call(
        _moe_down_kernel,
        grid_spec=pltpu.PrefetchScalarGridSpec(
            num_scalar_prefetch=2, grid=(d // tn, n_tiles),
            in_specs=[pl.BlockSpec((tmr, d_e), lambda j, t, te, nu: (t, 0)),
                      pl.BlockSpec((None, d_e, tn), lambda j, t, te, nu: (te[t], 0, j)),
                      pl.BlockSpec((tmr, 1), lambda j, t, te, nu: (t, 0))],
            out_specs=pl.BlockSpec((tmr, tn), lambda j, t, te, nu: (t, j))),
        out_shape=jax.ShapeDtypeStruct((r_rows, d), F32),
        compiler_params=_params(("parallel", "parallel"), est),
        name="moe_down",
    )(tile_expert, n_used, hidden, w_d, row_w)


def _moe_combine_kernel(x_ref, y1_ref, y2_ref, gate_ref, o_ref):
    o_ref[...] = x_ref[...] + gate_ref[...] * (y1_ref[...] + y2_ref[...])


def moe_route(logits, n_experts, tmr):
    n_tok = logits.shape[0]
    lg = logits[:, :n_experts]
    top_v, top_i = lax.top_k(lg, TOP_K)
    top_w = jax.nn.softmax(top_v, axis=-1)
    flat_e = top_i.reshape(-1)
    onehot = (flat_e[:, None] == jnp.arange(n_experts)[None, :]).astype(jnp.int32)
    rank = jnp.take_along_axis(jnp.cumsum(onehot, axis=0) - onehot, flat_e[:, None], axis=1)[:, 0]
    counts = jnp.sum(onehot, axis=0)
    tiles_per = (counts + tmr - 1) // tmr
    tile_start = jnp.cumsum(tiles_per) - tiles_per
    dest = tile_start[flat_e] * tmr + rank
    n_tiles = (n_tok * TOP_K) // tmr + n_experts
    r_rows = n_tiles * tmr
    src_tok = jnp.zeros((r_rows,), jnp.int32).at[dest].set(jnp.arange(n_tok * TOP_K, dtype=jnp.int32) // TOP_K)
    row_w = jnp.zeros((r_rows,), F32).at[dest].set(top_w.reshape(-1))
    tile_ids = jnp.arange(n_tiles, dtype=jnp.int32)
    tile_expert = jnp.sum((tile_ids[:, None] >= jnp.cumsum(tiles_per)[None, :]).astype(jnp.int32), axis=1)
    n_used = jnp.sum(tiles_per).astype(jnp.int32)
    tile_expert = jnp.where(tile_ids < n_used, tile_expert, tile_expert[jnp.maximum(n_used - 1, 0)])
    tile_expert = jnp.minimum(tile_expert, n_experts - 1).astype(jnp.int32)
    return src_tok, row_w.reshape(r_rows, 1), tile_expert, n_used.reshape(1), dest.reshape(n_tok, TOP_K)


def moe_ffn_residual(x, h, logits, mod, gate_idx, w_g, w_u, w_d, *, rows, seq, batch):
    d = x.shape[1]
    n_experts = w_g.shape[0]
    tmr = 512 if rows * TOP_K >= 8 * 512 * n_experts else 128
    tn = 512
    src_tok, row_w, tile_expert, n_used, dest = moe_route(logits, n_experts, tmr)
    xg = jnp.take(h, src_tok, axis=0)
    y = moe_experts(xg, row_w, tile_expert, n_used, w_g, w_u, w_d, tmr=tmr, tn=tn)
    y1 = jnp.take(y, dest[:, 0], axis=0)
    y2 = jnp.take(y, dest[:, 1], axis=0)
    tm = 256
    return pl.pallas_call(
        _moe_combine_kernel,
        grid=(rows // tm,),
        in_specs=[pl.BlockSpec((tm, d), lambda i: (i, 0)),
                  pl.BlockSpec((tm, d), lambda i: (i, 0)),
                  pl.BlockSpec((tm, d), lambda i: (i, 0)),
                  pl.BlockSpec((None, 1, d), lambda i: (_mod_row(i, tm, seq, batch), 0, gate_idx))],
        out_specs=pl.BlockSpec((tm, d), lambda i: (i, 0)),
        out_shape=jax.ShapeDtypeStruct((rows, d), F32),
        compiler_params=_params(("parallel",), 8 * _nbytes((tm, d), F32)),
        name="moe_combine",
    )(x, y1, y2, mod)


def _rope_tables(seq):
    rows = seq // GRID_W
    half = HEAD_DIM // 2
    row = jnp.broadcast_to(jnp.arange(rows)[:, None], (rows, GRID_W)).reshape(-1).astype(F32)
    col = jnp.broadcast_to(jnp.arange(GRID_W)[None, :], (rows, GRID_W)).reshape(-1).astype(F32)
    inv_freq = ROPE_BASE ** (-jnp.arange(0, half, 2, dtype=F32) / half)
    ang_r = row[:, None] * inv_freq
    ang_c = col[:, None] * inv_freq
    ang = jnp.concatenate([ang_r, ang_r, ang_c, ang_c], axis=-1)
    cos, sin = jnp.cos(ang), jnp.sin(ang)
    first = (jnp.arange(HEAD_DIM) % half) < half // 2
    return cos, jnp.where(first, -sin, 0.0), jnp.where(first, 0.0, sin)


def kernel(x, c, ctx, c_ctx, w_mod, b_mod, g_attn_norm, w_in, attn_sink, ssm_a_re, ssm_a_im, ssm_log_dt,
           ssm_b_re, ssm_b_im, ssm_c_re, ssm_c_im, ssm_d, w_glu, g_out_attn, g_out_ssm, w_out, g_ffn_norm,
           w_ff_gate, w_ff_up, w_ff_down, w_router, w_exp_gate, w_exp_up, w_exp_down, g_final):
    batch, seq, d = x.shape
    n_ctx = ctx.shape[1]
    depth = w_in.shape[0]
    n_q = attn_sink.shape[1]
    n_kv = n_q // Q_PER_KV
    attn_w = n_q * HEAD_DIM
    kv_w = n_kv * HEAD_DIM
    ssm_w = d - attn_w
    u_col0 = attn_w + 2 * kv_w
    rows_x = batch * seq
    rows_all = rows_x + batch * n_ctx
    big = rows_x >= 8192
    tm_big = 1024 if big else 256
    tm_mid = 512 if big else 256
    tn = 512 if big else 256
    geo = dict(seq=seq, batch=batch)

    xs = jnp.concatenate([x.reshape(rows_x, d), ctx.reshape(batch * n_ctx, d)], axis=0)
    cond = jnp.concatenate([c, c_ctx[None, :]], axis=0)
    rope_tabs = _rope_tables(seq)

    for i in range(depth):
        last = i == depth - 1
        rows_out = rows_x if last else rows_all
        mod = ada_modulation(cond, w_mod[i], b_mod[i])
        h = norm_modulate(xs, g_attn_norm[i], mod, 0, 1, rows=rows_all, **geo)
        p = matmul(h, w_in[i].astype(BF16), rows=rows_all, tm=tm_big, tn=tn, out_dtype=F32)
        o_attn = latent_attention(p, attn_sink[i], rope_tabs, batch=batch, seq=seq, n_ctx=n_ctx, n_kv=n_kv)
        if not last:
            o_attn = context_attention(p, o_attn, attn_sink[i], batch=batch, seq=seq, n_ctx=n_ctx, n_kv=n_kv)
        tables = _ssm_tables(ssm_a_re[i], ssm_a_im[i], ssm_log_dt[i], ssm_b_re[i], ssm_b_im[i],
                             ssm_c_re[i], ssm_c_im[i])
        y = s5_mix(p, tables, ssm_d[i], batch=batch, seq=seq, n_ctx=n_ctx, u_col0=u_col0, ssm_width=ssm_w)
        o_ssm = matmul_glu(y, w_glu[i].astype(BF16), rows=rows_out, tm=tm_big, tn=tn)
        cat = merge_norm(o_attn, o_ssm, g_out_attn[i], g_out_ssm[i], rows=rows_out)
        xs = matmul_gated_residual(cat, w_out[i].astype(BF16), xs, mod, 2, rows=rows_out, tm=tm_big, tn=tn,
                                   weight_outer=False, **geo)
        j = i // 2
        if i % 2 == 0:
            h2 = norm_modulate(xs, g_ffn_norm[i], mod, 3, 4, rows=rows_out, **geo)
            hid = matmul_swiglu(h2, w_ff_gate[j].astype(BF16), w_ff_up[j].astype(BF16),
                                rows=rows_out, tm=tm_big, tn=tn)
            xs = matmul_gated_residual(hid, w_ff_down[j].astype(BF16), xs, mod, 5, rows=rows_out, tm=tm_mid,
                                       tn=tn, weight_outer=True, **geo)
        else:
            n_e = w_router.shape[2]
            wr = jnp.zeros((d, LANES), F32).at[:, :n_e].set(w_router[j])
            h2, logits = norm_modulate(xs, g_ffn_norm[i], mod, 3, 4, rows=rows_out, w_router=wr, **geo)
            xs = moe_ffn_residual(xs, h2, logits, mod, 5, w_exp_gate[j].astype(BF16), w_exp_up[j].astype(BF16),
                                  w_exp_down[j].astype(BF16), rows=rows_out, seq=seq, batch=batch)
    return final_norm(xs, g_final, rows=rows_x).reshape(batch, seq, d)
```

```python
import functools
import math

import jax
import jax.numpy as jnp
from jax import lax
from jax.experimental import pallas as pl
from jax.experimental.pallas import tpu as pltpu

F32 = jnp.float32
BF16 = jnp.bfloat16

HEAD_DIM = 128
BLOCK = 128
Q_PER_KV = 3
GRID_W = 64
ROPE_BASE = 10000.0
SSM_GROUP = 16
N_MOD = 6
TOP_K = 2
EPS = 1e-6
NEG_INF = -1e30

LANES = 128
SUBLANES = 8
VMEM_BYTES_V7X = 64 * 1024 * 1024
VMEM_CAP = VMEM_BYTES_V7X - 8 * 1024 * 1024

SSM_CHUNK = 16


def _params(sem, est_bytes):
    limit = int(min(VMEM_CAP, max(32 * 1024 * 1024, est_bytes * 5 // 4)))
    return pltpu.CompilerParams(dimension_semantics=sem, vmem_limit_bytes=limit)


def _nbytes(shape, dtype):
    return math.prod(shape) * jnp.dtype(dtype).itemsize


def _mod_row(i, tm, seq, batch):
    return jnp.minimum((i * tm) // seq, batch)


def _mod_kernel(cb_ref, w_ref, b_ref, o_ref, *, n_rows, tn):
    k_dim = w_ref.shape[0]
    n_chunks = tn // LANES

    def body(kb, accs):
        k0 = pl.multiple_of(kb * SUBLANES, SUBLANES)
        wv = w_ref[pl.ds(k0, SUBLANES), :]
        new = []
        for r in range(n_rows):
            cv = cb_ref[r, pl.ds(k0, SUBLANES), :]
            sv = cv * jax.nn.sigmoid(cv)
            for c in range(n_chunks):
                new.append(accs[r * n_chunks + c] + wv[:, c * LANES:(c + 1) * LANES] * sv)
        return tuple(new)

    init = tuple(jnp.zeros((SUBLANES, LANES), F32) for _ in range(n_rows * n_chunks))
    accs = lax.fori_loop(0, k_dim // SUBLANES, body, init, unroll=4)
    o_ref[...] = jnp.zeros_like(o_ref)
    for r in range(n_rows):
        row = jnp.concatenate(
            [jnp.sum(accs[r * n_chunks + c], axis=0, keepdims=True) for c in range(n_chunks)], axis=1)
        o_ref[r:r + 1, :] = row + b_ref[...]


def ada_modulation(cond_rows, w_mod, b_mod, layer):
    n_rows, d = cond_rows.shape
    depth, _, n = w_mod.shape
    tn = 512
    cb = jnp.broadcast_to(cond_rows[:, :, None], (n_rows, d, LANES))
    est = 2 * (_nbytes((d, tn), F32) + _nbytes((n_rows, d, LANES), F32))
    out = pl.pallas_call(
        functools.partial(_mod_kernel, n_rows=n_rows, tn=tn),
        grid=(n // tn,),
        in_specs=[pl.BlockSpec((n_rows, d, LANES), lambda j: (0, 0, 0)),
                  pl.BlockSpec((None, d, tn), lambda j: (layer, 0, j)),
                  pl.BlockSpec((None, 1, tn), lambda j: (layer, 0, j))],
        out_specs=pl.BlockSpec((SUBLANES, tn), lambda j: (0, j)),
        out_shape=jax.ShapeDtypeStruct((SUBLANES, n), F32),
        compiler_params=_params(("parallel",), est),
        name="ada_modulation",
    )(cb, w_mod, b_mod.reshape(depth, 1, n))
    return out.reshape(SUBLANES, 1, n)


def _split3(v):
    hi = v.astype(BF16)
    r1 = v - hi.astype(F32)
    mid = r1.astype(BF16)
    lo = (r1 - mid.astype(F32)).astype(BF16)
    return hi, mid, lo


def _normmod_kernel(x_ref, g_ref, shift_ref, scale_ref, *rest, with_router):
    xf = x_ref[...]
    y = xf * lax.rsqrt(jnp.mean(xf * xf, axis=-1, keepdims=True) + EPS)
    h = (y * g_ref[...]) * (1.0 + scale_ref[...]) + shift_ref[...]
    if with_router:
        wr_ref, o_ref, lg_ref = rest
        hs = _split3(h)
        ws = _split3(wr_ref[...])
        acc = None
        for a, b in ((0, 0), (0, 1), (1, 0), (1, 1), (0, 2), (2, 0)):
            t = jnp.dot(hs[a], ws[b], preferred_element_type=F32)
            acc = t if acc is None else acc + t
        lg_ref[...] = acc
    else:
        (o_ref,) = rest
    o_ref[...] = h.astype(BF16)


def norm_modulate(x, g, mod, shift_idx, scale_idx, *, rows, seq, batch, w_router=None):
    d = x.shape[1]
    tm = 256
    with_router = w_router is not None
    in_specs = [pl.BlockSpec((tm, d), lambda i: (i, 0)),
                pl.BlockSpec((1, d), lambda i: (0, 0)),
                pl.BlockSpec((None, 1, d), lambda i: (_mod_row(i, tm, seq, batch), 0, shift_idx)),
                pl.BlockSpec((None, 1, d), lambda i: (_mod_row(i, tm, seq, batch), 0, scale_idx))]
    args = [x, g.reshape(1, d), mod, mod]
    out_specs = [pl.BlockSpec((tm, d), lambda i: (i, 0))]
    out_shape = [jax.ShapeDtypeStruct((rows, d), BF16)]
    if with_router:
        in_specs.append(pl.BlockSpec((d, LANES), lambda i: (0, 0)))
        args.append(w_router)
        out_specs.append(pl.BlockSpec((tm, LANES), lambda i: (i, 0)))
        out_shape.append(jax.ShapeDtypeStruct((rows, LANES), F32))
    est = 2 * (_nbytes((tm, d), F32) + _nbytes((tm, d), BF16)) + 8 * _nbytes((tm, d), F32)
    outs = pl.pallas_call(
        functools.partial(_normmod_kernel, with_router=with_router),
        grid=(pl.cdiv(rows, tm),),
        in_specs=in_specs, out_specs=out_specs, out_shape=out_shape,
        compiler_params=_params(("parallel",), est),
        name="norm_modulate_router" if with_router else "norm_modulate",
    )(*args)
    return outs if with_router else outs[0]


def _mm_kernel(x_ref, w_ref, o_ref):
    o_ref[...] = jnp.dot(x_ref[...], w_ref[...], preferred_element_type=F32).astype(o_ref.dtype)


def _mm_swiglu_kernel(x_ref, wg_ref, wu_ref, o_ref):
    x = x_ref[...]
    a = jnp.dot(x, wg_ref[...], preferred_element_type=F32)
    b = jnp.dot(x, wu_ref[...], preferred_element_type=F32)
    o_ref[...] = (a * jax.nn.sigmoid(a) * b).astype(o_ref.dtype)


def _mm_resid_kernel(x_ref, w_ref, r_ref, gate_ref, o_ref):
    acc = jnp.dot(x_ref[...], w_ref[...], preferred_element_type=F32)
    o_ref[...] = r_ref[...] + gate_ref[...] * acc


def _mm_glu_kernel(y_ref, w_ref, yt_ref, o_ref):
    z = jnp.dot(y_ref[...].astype(BF16), w_ref[...], preferred_element_type=F32)
    o_ref[...] = yt_ref[...] * jax.nn.sigmoid(z)


def matmul(x, w, *, rows, tm, tn, out_dtype):
    k, n = w.shape
    est = 2 * (_nbytes((tm, k), x.dtype) + _nbytes((k, tn), w.dtype) + _nbytes((tm, tn), out_dtype))
    return pl.pallas_call(
        _mm_kernel,
        grid=(pl.cdiv(rows, tm), pl.cdiv(n, tn)),
        in_specs=[pl.BlockSpec((tm, k), lambda i, j: (i, 0)),
                  pl.BlockSpec((k, tn), lambda i, j: (0, j))],
        out_specs=pl.BlockSpec((tm, tn), lambda i, j: (i, j)),
        out_shape=jax.ShapeDtypeStruct((rows, n), out_dtype),
        compiler_params=_params(("parallel", "parallel"), est),
        name="matmul",
    )(x, w)


def _rope(x, cos, sin_lo, sin_hi):
    return (x * cos + pltpu.roll(x, HEAD_DIM - HEAD_DIM // 4, 1) * sin_lo
            + pltpu.roll(x, HEAD_DIM // 4, 1) * sin_hi)


def _mm_rope_kernel(x_ref, w_ref, cos_ref, slo_ref, shi_ref, *o_refs, n_rope_tiles, scale, both, sub):
    j = pl.program_id(1)
    tm, tn = o_refs[0].shape
    for r0 in range(0, tm, sub):
        acc = jnp.dot(x_ref[r0:r0 + sub, :], w_ref[...], preferred_element_type=F32)
        if scale != 1.0:
            acc = acc * scale
        cos, slo, shi = (r[r0:r0 + sub, :] for r in (cos_ref, slo_ref, shi_ref))
        rot = jnp.concatenate([_rope(acc[:, h * HEAD_DIM:(h + 1) * HEAD_DIM], cos, slo, shi)
                               for h in range(tn // HEAD_DIM)], axis=1)
        if both:
            o_refs[0][r0:r0 + sub, :] = rot.astype(BF16)
            o_refs[1][r0:r0 + sub, :] = acc.astype(BF16)
        else:
            o_refs[0][r0:r0 + sub, :] = jnp.where(j < n_rope_tiles, rot, acc).astype(BF16)


def matmul_rope(x, w, tabs, *, rows, rows_x, seq, tm, tn, n_rope_tiles, scale, both):
    k, n = w.shape
    per_seq = seq // tm

    def tab_map(i, j):
        return (jnp.where(i * tm < rows_x, i % per_seq, per_seq), 0)

    n_out = 2 if both else 1
    est = 2 * (_nbytes((tm, k), BF16) + _nbytes((k, tn), BF16) + n_out * _nbytes((tm, tn), BF16)
               + 3 * _nbytes((tm, HEAD_DIM), F32)) + 4 * _nbytes((tm, tn), F32)
    out = pl.pallas_call(
        functools.partial(_mm_rope_kernel, n_rope_tiles=n_rope_tiles, scale=scale, both=both, sub=min(tm, 256)),
        grid=(pl.cdiv(rows, tm), n // tn),
        in_specs=[pl.BlockSpec((tm, k), lambda i, j: (i, 0)),
                  pl.BlockSpec((k, tn), lambda i, j: (0, j))] + [pl.BlockSpec((tm, HEAD_DIM), tab_map)] * 3,
        out_specs=[pl.BlockSpec((tm, tn), lambda i, j: (i, j))] * n_out,
        out_shape=[jax.ShapeDtypeStruct((rows, n), BF16)] * n_out,
        compiler_params=_params(("parallel", "parallel"), est),
        name="matmul_rope",
    )(x, w, *tabs)
    return out if both else out[0]


def matmul_swiglu(x, wg, wu, *, rows, tm, tn):
    k, n = wg.shape
    est = 2 * (_nbytes((tm, k), BF16) + 2 * _nbytes((k, tn), BF16) + _nbytes((tm, tn), BF16)) \
        + 3 * _nbytes((tm, tn), F32)
    return pl.pallas_call(
        _mm_swiglu_kernel,
        grid=(pl.cdiv(rows, tm), pl.cdiv(n, tn)),
        in_specs=[pl.BlockSpec((tm, k), lambda i, j: (i, 0)),
                  pl.BlockSpec((k, tn), lambda i, j: (0, j)),
                  pl.BlockSpec((k, tn), lambda i, j: (0, j))],
        out_specs=pl.BlockSpec((tm, tn), lambda i, j: (i, j)),
        out_shape=jax.ShapeDtypeStruct((rows, n), BF16),
        compiler_params=_params(("parallel", "parallel"), est),
        name="matmul_swiglu",
    )(x, wg, wu)


def matmul_gated_residual(x, w, resid, mod, gate_idx, *, rows, tm, tn, seq, batch, weight_outer):
    k, n = w.shape
    nj = n // tn
    if weight_outer:
        grid = (nj, pl.cdiv(rows, tm))
        ij = lambda a, b: (b, a)
    else:
        grid = (pl.cdiv(rows, tm), nj)
        ij = lambda a, b: (a, b)

    def gate_map(a, b):
        i, j = ij(a, b)
        return (_mod_row(i, tm, seq, batch), 0, gate_idx * nj + j)

    est = 2 * (_nbytes((tm, k), BF16) + _nbytes((k, tn), BF16) + 2 * _nbytes((tm, tn), F32))
    return pl.pallas_call(
        _mm_resid_kernel,
        grid=grid,
        in_specs=[pl.BlockSpec((tm, k), lambda a, b: (ij(a, b)[0], 0)),
                  pl.BlockSpec((k, tn), lambda a, b: (0, ij(a, b)[1])),
                  pl.BlockSpec((tm, tn), lambda a, b: ij(a, b)),
                  pl.BlockSpec((None, 1, tn), gate_map)],
        out_specs=pl.BlockSpec((tm, tn), lambda a, b: ij(a, b)),
        out_shape=jax.ShapeDtypeStruct((rows, n), F32),
        compiler_params=_params(("parallel", "parallel"), est),
        name="matmul_gated_residual",
    )(x, w, resid, mod)


def matmul_glu(y, w, *, rows, tm, tn):
    k, n = w.shape
    est = 2 * (_nbytes((tm, k), F32) + _nbytes((k, tn), BF16) + 2 * _nbytes((tm, tn), F32))
    return pl.pallas_call(
        _mm_glu_kernel,
        grid=(pl.cdiv(rows, tm), n // tn),
        in_specs=[pl.BlockSpec((tm, k), lambda i, j: (i, 0)),
                  pl.BlockSpec((k, tn), lambda i, j: (0, j)),
                  pl.BlockSpec((tm, tn), lambda i, j: (i, j))],
        out_specs=pl.BlockSpec((tm, tn), lambda i, j: (i, j)),
        out_shape=jax.ShapeDtypeStruct((rows, n), F32),
        compiler_params=_params(("parallel", "parallel"), est),
        name="matmul_glu",
    )(y, w, y)


def _nt_dot(a, b):
    return lax.dot_general(a, b, (((1,), (1,)), ((), ())), preferred_element_type=F32)


def _attn_kernel(sink_ref, qr_ref, qp_ref, kp_ref, k0_ref, kn_ref, vp_ref, v0_ref, vn_ref, kc_ref, vc_ref,
                 bias_ref, o_ref, *, n_blocks, ctx_queries, hp):
    n = pl.program_id(1)
    qw = Q_PER_KV * HEAD_DIM

    def stack_heads(ref, hh):
        return jnp.concatenate([ref[:, hh * qw + g * HEAD_DIM:hh * qw + (g + 1) * HEAD_DIM]
                                for g in range(Q_PER_KV)], axis=0)

    def head(ref, hh):
        return ref[:, hh * HEAD_DIM:(hh + 1) * HEAD_DIM]

    def run_head(window, hh):
        kvh = pl.program_id(2) * hp + hh
        sink = jnp.concatenate(
            [jnp.full((BLOCK, 1), sink_ref[kvh * Q_PER_KV + g], F32) for g in range(Q_PER_KV)], axis=0)
        s_c = _nt_dot(stack_heads(qp_ref, hh), head(kc_ref, hh))
        m = jnp.maximum(jnp.max(s_c, axis=1, keepdims=True), sink)
        if window:
            k_win = jnp.concatenate([head(kp_ref, hh), head(k0_ref, hh), head(kn_ref, hh)], axis=0)
            s_w = _nt_dot(stack_heads(qr_ref, hh), k_win) + bias_ref[...]
            m = jnp.maximum(m, jnp.max(s_w, axis=1, keepdims=True))
        p_c = jnp.exp(s_c - m)
        den = jnp.sum(p_c, axis=1, keepdims=True) + jnp.exp(sink - m)
        o = jnp.dot(p_c.astype(BF16), head(vc_ref, hh), preferred_element_type=F32)
        if window:
            p_w = jnp.exp(s_w - m)
            den = den + jnp.sum(p_w, axis=1, keepdims=True)
            v_win = jnp.concatenate([head(vp_ref, hh), head(v0_ref, hh), head(vn_ref, hh)], axis=0)
            o = o + jnp.dot(p_w.astype(BF16), v_win, preferred_element_type=F32)
        o = o / den
        for g in range(Q_PER_KV):
            o_ref[:, hh * qw + g * HEAD_DIM:hh * qw + (g + 1) * HEAD_DIM] = o[g * BLOCK:(g + 1) * BLOCK]

    def run(window):
        for hh in range(hp):
            run_head(window, hh)

    if ctx_queries:
        pl.when(n < n_blocks)(lambda: run(True))
        pl.when(n >= n_blocks)(lambda: run(False))
    else:
        run(True)


def _window_bias(n_blocks):
    rows = Q_PER_KV * BLOCK
    row = (jnp.arange(rows) % BLOCK)[:, None]
    col = jnp.arange(3 * BLOCK)[None, :]
    band = (col >= row) & (col <= row + 2 * BLOCK)
    variants = []
    for code in range(4):
        ok = band
        if code & 1:
            ok = ok & (col >= BLOCK)
        if code & 2:
            ok = ok & (col < 2 * BLOCK)
        variants.append(jnp.where(ok, 0.0, NEG_INF).astype(F32))
    return jnp.stack(variants)


def attention(q_rot, q_plain, kv, sink, *, batch, seq, n_ctx, n_kv, ctx_queries):
    nb = seq // BLOCK
    qb = n_ctx // BLOCK if ctx_queries else 0
    hp = max(m for m in (4, 2, 1) if n_kv % m == 0)
    n_hg = n_kv // hp
    qw = hp * Q_PER_KV * HEAD_DIM
    kw = hp * HEAD_DIM
    rows_out = batch * seq + (batch * n_ctx if ctx_queries else 0)
    ctx_blk0 = batch * seq // n_ctx
    q_blk0 = batch * seq // BLOCK

    def q_map(b, n, h, s):
        return (jnp.where(n < nb, b * nb + n, q_blk0 + b * qb + (n - nb)), h)

    def kv_spec(col0, shift):
        return pl.BlockSpec((BLOCK, kw),
                            lambda b, n, h, s: (b * nb + jnp.clip(n + shift, 0, nb - 1), col0 + h))

    def bias_map(b, n, h, s):
        return ((n == 0).astype(jnp.int32) + 2 * (n == nb - 1).astype(jnp.int32), 0, 0)

    in_specs = [pl.BlockSpec((BLOCK, qw), q_map), pl.BlockSpec((BLOCK, qw), q_map),
                kv_spec(0, -1), kv_spec(0, 0), kv_spec(0, 1),
                kv_spec(n_hg, -1), kv_spec(n_hg, 0), kv_spec(n_hg, 1),
                pl.BlockSpec((n_ctx, kw), lambda b, n, h, s: (ctx_blk0 + b, h)),
                pl.BlockSpec((n_ctx, kw), lambda b, n, h, s: (ctx_blk0 + b, n_hg + h)),
                pl.BlockSpec((None, Q_PER_KV * BLOCK, 3 * BLOCK), bias_map)]
    grid_spec = pltpu.PrefetchScalarGridSpec(
        num_scalar_prefetch=1, grid=(batch, nb + qb, n_hg), in_specs=in_specs,
        out_specs=pl.BlockSpec((BLOCK, qw), q_map))
    return pl.pallas_call(
        functools.partial(_attn_kernel, n_blocks=nb, ctx_queries=ctx_queries, hp=hp),
        grid_spec=grid_spec,
        out_shape=jax.ShapeDtypeStruct((rows_out, n_hg * qw), F32),
        compiler_params=_params(("parallel", "parallel", "parallel"), 8 * 1024 * 1024),
        name="attention",
    )(sink, q_rot, q_plain, *([kv] * 8), _window_bias(nb))


def _ssm_tables(a_re, a_im, log_dt, b_re, b_im, c_re, c_im):
    hp = lax.Precision.HIGHEST
    t_len = SSM_CHUNK
    g_per = LANES // SSM_GROUP
    n_groups, n_state = a_re.shape[1:]
    n_lb = n_groups // g_per
    dt = jnp.exp(log_dt.astype(F32))[..., None]
    lam_re, lam_im = a_re.astype(F32), a_im.astype(F32)
    mag = jnp.exp(lam_re * dt)
    abar = lax.complex(mag * jnp.cos(lam_im * dt), mag * jnp.sin(lam_im * dt))
    lam = lax.complex(lam_re, lam_im)
    bbar = ((abar - 1.0) / lam)[..., None] * lax.complex(b_re.astype(F32), b_im.astype(F32))
    cmat = lax.complex(c_re.astype(F32), c_im.astype(F32))

    def powers(d, exps):
        e = jnp.asarray(exps, F32)[:, None, None]
        m = jnp.exp(lam_re[d] * dt[d] * e)
        ph = lam_im[d] * dt[d] * e
        return lax.complex(m * jnp.cos(ph), m * jnp.sin(ph))

    ar = list(range(t_len + 1))
    pw_f, pw_b = powers(0, ar), powers(1, ar)

    kf = jnp.real(jnp.einsum('gcp,tgp,gpi->tgci', cmat[0], pw_f[:t_len], bbar[0], precision=hp))
    kb = jnp.real(jnp.einsum('gcp,tgp,gpi->tgci', cmat[1], pw_b[:t_len], bbar[1], precision=hp))
    lag = jnp.arange(2 * t_len - 1) - (t_len - 1)
    k_lag = jnp.where((lag >= 0)[:, None, None, None], kf[jnp.clip(lag, 0)], 0.0) \
        + jnp.where((lag <= 0)[:, None, None, None], kb[jnp.clip(-lag, 0)], 0.0)
    k_lag = jnp.swapaxes(k_lag, -1, -2).reshape(2 * t_len - 1, n_lb, g_per, SSM_GROUP, SSM_GROUP)
    eye = jnp.eye(g_per, dtype=F32)
    k_lag = jnp.einsum('mngab,gh->nmgahb', k_lag, eye).reshape(n_lb, 2 * t_len - 1, LANES, LANES)

    def e_coef(d, pw_sel):
        return jnp.swapaxes(pw_sel[..., None] * bbar[d][None], -1, -2)

    ef, eb = e_coef(0, powers(0, [t_len - 1 - j for j in range(t_len)])), e_coef(1, pw_b[:t_len])
    quarters = jnp.stack([jnp.real(ef), jnp.imag(ef), jnp.real(eb), jnp.imag(eb)], axis=-2)
    quarters = jnp.tile(quarters, LANES // n_state)
    e_c = quarters.reshape(t_len, n_lb, LANES, 4 * LANES).transpose(1, 0, 2, 3)

    def f_coef(d, pw_sel):
        return cmat[d][None] * pw_sel[:, :, None, :]

    ff, fb = f_coef(0, pw_f[1:]), f_coef(1, powers(1, [t_len - l for l in range(t_len)]))
    f_c = jnp.stack([jnp.real(ff), -jnp.imag(ff), jnp.real(fb), -jnp.imag(fb)])
    f_c = f_c.reshape(4, t_len, n_lb, g_per, SSM_GROUP, n_state).transpose(2, 0, 5, 1, 3, 4)
    f_c = f_c.reshape(n_lb, 4 * n_state, t_len * LANES)

    def lay(v):
        return v.reshape(n_lb, g_per * n_state)

    a_chunk = jnp.concatenate([lay(jnp.real(pw_f[t_len])), lay(jnp.imag(pw_f[t_len])),
                               lay(jnp.real(pw_b[t_len])), lay(jnp.imag(pw_b[t_len]))], axis=-1)
    return k_lag.astype(BF16), e_c, f_c, a_chunk.reshape(1, -1), n_state


def _chunk_inputs(x_ref, tmc):
    return [x_ref[pl.ds(j, tmc, stride=SSM_CHUNK), :] for j in range(SSM_CHUNK)]


def _ssm_local_kernel(x_ref, ec_ref, s_ref, e_scr, *, n_state):
    tmc = s_ref.shape[0]
    qw = e_scr.shape[1] // 4

    @pl.when(pl.program_id(1) == 0)
    def _():
        row_g = lax.broadcasted_iota(jnp.int32, (LANES, qw), 0) // SSM_GROUP
        col_g = lax.broadcasted_iota(jnp.int32, (LANES, qw), 1) // n_state
        same = row_g == col_g
        for j in range(SSM_CHUNK):
            for q in range(4):
                v = ec_ref[j, :, q * LANES:(q + 1) * LANES]
                tiled = jnp.concatenate([v] * (qw // LANES), axis=1)
                e_scr[j * LANES:(j + 1) * LANES, q * qw:(q + 1) * qw] = jnp.where(same, tiled, 0.0).astype(BF16)

    xs = jnp.concatenate([x.astype(BF16) for x in _chunk_inputs(x_ref, tmc)], axis=1)
    s_ref[...] = jnp.dot(xs, e_scr[...], preferred_element_type=F32)


def _ssm_scan_kernel(s_ref, a_ref, h_ref, *, batch, nc_x, nc_c):
    sw = s_ref.shape[1] // 4
    a = a_ref[...]
    afr, afi, abr, abi = (a[:, k * sw:(k + 1) * sw] for k in range(4))
    ctx0 = batch * nc_x

    def step(row_f, row_b, st):
        new = []
        for (row, ar, ai, off, (hr, hi)) in ((row_f, afr, afi, 0, st[0]), (row_b, abr, abi, 2 * sw, st[1])):
            h_ref[pl.ds(row, 1), off:off + sw] = hr
            h_ref[pl.ds(row, 1), off + sw:off + 2 * sw] = hi
            sr = s_ref[pl.ds(row, 1), off:off + sw]
            si = s_ref[pl.ds(row, 1), off + sw:off + 2 * sw]
            new.append((ar * hr - ai * hi + sr, ar * hi + ai * hr + si))
        return tuple(new)

    zero = jnp.zeros((1, sw), F32)
    for b in range(batch):
        st = ((zero, zero), (zero, zero))
        c_base = ctx0 + b * nc_c
        st = lax.fori_loop(0, nc_c, lambda t, s: step(c_base + t, c_base + nc_c - 1 - t, s), st)
        x_base = b * nc_x
        lax.fori_loop(0, nc_x, lambda t, s: step(x_base + t, x_base + nc_x - 1 - t, s), st)


def _ssm_out_kernel(x_ref, h_ref, kl_ref, fc_ref, d_ref, o_ref, w_scr, *, n_state):
    t = SSM_CHUNK
    tmc = h_ref.shape[0]
    sdim = h_ref.shape[1]
    qw = sdim // 4

    @pl.when(pl.program_id(1) == 0)
    def _():
        for j in range(t):
            for l in range(t):
                w_scr[j * LANES:(j + 1) * LANES, l * LANES:(l + 1) * LANES] = kl_ref[l - j + t - 1]
        lane_g = (lax.broadcasted_iota(jnp.int32, (n_state, t * LANES), 1) % LANES) // SSM_GROUP
        for q in range(4):
            v = fc_ref[q * n_state:(q + 1) * n_state, :]
            for g in range(qw // n_state):
                r0 = t * LANES + q * qw + g * n_state
                w_scr[r0:r0 + n_state, :] = jnp.where(lane_g == g, v, 0.0).astype(BF16)

    xs = _chunk_inputs(x_ref, tmc)
    lhs = jnp.concatenate([x.astype(BF16) for x in xs] + [h_ref[...].astype(BF16)], axis=1)
    y = jnp.dot(lhs, w_scr[...], preferred_element_type=F32)
    d = d_ref[...]
    for l in range(t):
        o_ref[pl.ds(l, tmc, stride=t), :] = jax.nn.gelu(y[:, l * LANES:(l + 1) * LANES] + d * xs[l])


def s5_mix(u, tables, d_skip, *, batch, seq, n_ctx):
    k_lag, e_c, f_c, a_chunk, n_state = tables
    rows, ssm_width = u.shape
    t = SSM_CHUNK
    n_lb = ssm_width // LANES
    rc = rows // t
    sdim = 4 * (LANES // SSM_GROUP) * n_state
    tmc = max(m for m in range(SUBLANES, min(rc, 264) + 1, SUBLANES) if rc % m == 0)
    x_spec = pl.BlockSpec((tmc * t, LANES), lambda g, r: (r, g))
    x_bytes = _nbytes((tmc * t, LANES), F32)

    s_loc = pl.pallas_call(
        functools.partial(_ssm_local_kernel, n_state=n_state),
        grid=(n_lb, rc // tmc),
        in_specs=[x_spec, pl.BlockSpec((None, t, LANES, 4 * LANES), lambda g, r: (g, 0, 0, 0))],
        out_specs=pl.BlockSpec((tmc, sdim), lambda g, r: (r, g)),
        out_shape=jax.ShapeDtypeStruct((rc, n_lb * sdim), F32),
        scratch_shapes=[pltpu.VMEM((t * LANES, sdim), BF16)],
        compiler_params=_params(("arbitrary", "arbitrary"),
                                2 * x_bytes + 2 * _nbytes((t, LANES, 4 * LANES), F32)
                                + _nbytes((t * LANES, sdim), BF16) + 4 * _nbytes((tmc, sdim), F32)),
        name="s5_local_states",
    )(u, e_c)

    h_in = pl.pallas_call(
        functools.partial(_ssm_scan_kernel, batch=batch, nc_x=seq // t, nc_c=n_ctx // t),
        grid=(n_lb,),
        in_specs=[pl.BlockSpec((rc, sdim), lambda g: (0, g)),
                  pl.BlockSpec((1, sdim), lambda g: (0, g))],
        out_specs=pl.BlockSpec((rc, sdim), lambda g: (0, g)),
        out_shape=jax.ShapeDtypeStruct((rc, n_lb * sdim), F32),
        compiler_params=_params(("parallel",), 4 * _nbytes((rc, sdim), F32)),
        name="s5_chunk_scan",
    )(s_loc, a_chunk)

    return pl.pallas_call(
        functools.partial(_ssm_out_kernel, n_state=n_state),
        grid=(n_lb, rc // tmc),
        in_specs=[x_spec,
                  pl.BlockSpec((tmc, sdim), lambda g, r: (r, g)),
                  pl.BlockSpec((None, 2 * t - 1, LANES, LANES), lambda g, r: (g, 0, 0, 0)),
                  pl.BlockSpec((None, 4 * n_state, t * LANES), lambda g, r: (g, 0, 0)),
                  pl.BlockSpec((1, LANES), lambda g, r: (0, g))],
        out_specs=x_spec,
        out_shape=jax.ShapeDtypeStruct((rows, ssm_width), F32),
        scratch_shapes=[pltpu.VMEM((t * LANES + sdim, t * LANES), BF16)],
        compiler_params=_params(("arbitrary", "arbitrary"),
                                4 * x_bytes + 2 * _nbytes((tmc, sdim), F32)
                                + 2 * _nbytes((2 * t - 1, LANES, LANES), BF16)
                                + 2 * _nbytes((4 * n_state, t * LANES), F32)
                                + _nbytes((t * LANES + sdim, t * LANES), BF16) + 4 * _nbytes((tmc, sdim), F32)),
        name="s5_readout",
    )(u, h_in, k_lag, f_c, d_skip.reshape(1, ssm_width))


def _rms(x, g):
    return x * lax.rsqrt(jnp.mean(x * x, axis=-1, keepdims=True) + EPS) * g


def _merge_norm_kernel(oa_ref, os_ref, ga_ref, gs_ref, o_ref):
    wa = oa_ref.shape[1]
    o_ref[:, :wa] = _rms(oa_ref[...], ga_ref[...]).astype(BF16)
    o_ref[:, wa:] = _rms(os_ref[...], gs_ref[...]).astype(BF16)


def merge_norm(o_attn, o_ssm, g_oa, g_os, *, rows):
    wa, ws = o_attn.shape[1], o_ssm.shape[1]
    tm = 256
    est = 4 * _nbytes((tm, wa + ws), F32)
    return pl.pallas_call(
        _merge_norm_kernel,
        grid=(pl.cdiv(rows, tm),),
        in_specs=[pl.BlockSpec((tm, wa), lambda i: (i, 0)),
                  pl.BlockSpec((tm, ws), lambda i: (i, 0)),
                  pl.BlockSpec((1, wa), lambda i: (0, 0)),
                  pl.BlockSpec((1, ws), lambda i: (0, 0))],
        out_specs=pl.BlockSpec((tm, wa + ws), lambda i: (i, 0)),
        out_shape=jax.ShapeDtypeStruct((rows, wa + ws), BF16),
        compiler_params=_params(("parallel",), est),
        name="merge_norm",
    )(o_attn, o_ssm, g_oa.reshape(1, wa), g_os.reshape(1, ws))


def _final_norm_kernel(x_ref, g_ref, o_ref):
    o_ref[...] = _rms(x_ref[...], g_ref[...])


def final_norm(x, g, *, rows):
    d = x.shape[1]
    tm = 256
    return pl.pallas_call(
        _final_norm_kernel,
        grid=(pl.cdiv(rows, tm),),
        in_specs=[pl.BlockSpec((tm, d), lambda i: (i, 0)), pl.BlockSpec((1, d), lambda i: (0, 0))],
        out_specs=pl.BlockSpec((tm, d), lambda i: (i, 0)),
        out_shape=jax.ShapeDtypeStruct((rows, d), F32),
        compiler_params=_params(("parallel",), 6 * _nbytes((tm, d), F32)),
        name="final_norm",
    )(x, g.reshape(1, d))


def _new_expert(te_ref, t):
    return jnp.logical_or(t == 0, te_ref[t] != te_ref[jnp.maximum(t - 1, 0)])


def _moe_up_kernel(te_ref, nu_ref, x_ref, wg_ref, wu_ref, o_ref, wg_scr, wu_scr):
    t = pl.program_id(1)

    @pl.when(_new_expert(te_ref, t))
    def _():
        wg_scr[...] = wg_ref[...].astype(BF16)
        wu_scr[...] = wu_ref[...].astype(BF16)

    @pl.when(t < nu_ref[0])
    def _():
        x = x_ref[...]
        a = jnp.dot(x, wg_scr[...], preferred_element_type=F32)
        b = jnp.dot(x, wu_scr[...], preferred_element_type=F32)
        o_ref[...] = (a * jax.nn.sigmoid(a) * b).astype(o_ref.dtype)

    @pl.when(t >= nu_ref[0])
    def _():
        o_ref[...] = jnp.zeros_like(o_ref)


def _moe_down_kernel(te_ref, nu_ref, h_ref, wd_ref, rw_ref, o_ref, wd_scr):
    t = pl.program_id(1)

    @pl.when(_new_expert(te_ref, t))
    def _():
        wd_scr[...] = wd_ref[...].astype(BF16)

    @pl.when(t < nu_ref[0])
    def _():
        acc = jnp.dot(h_ref[...], wd_scr[...], preferred_element_type=F32)
        o_ref[...] = acc * rw_ref[...]

    @pl.when(t >= nu_ref[0])
    def _():
        o_ref[...] = jnp.zeros_like(o_ref)


def moe_experts(xg, row_w, tile_expert, n_used, w_g, w_u, w_d, e_base, *, tmr, tn):
    r_rows, d = xg.shape
    d_e = w_g.shape[2]
    n_tiles = r_rows // tmr
    est = 2 * (_nbytes((tmr, d), BF16) + 2 * _nbytes((d, tn), F32) + _nbytes((tmr, tn), BF16)) \
        + 2 * _nbytes((d, tn), BF16) + 3 * _nbytes((tmr, tn), F32)
    hidden = pl.pallas_call(
        _moe_up_kernel,
        grid_spec=pltpu.PrefetchScalarGridSpec(
            num_scalar_prefetch=2, grid=(d_e // tn, n_tiles),
            in_specs=[pl.BlockSpec((tmr, d), lambda j, t, te, nu: (t, 0)),
                      pl.BlockSpec((None, d, tn), lambda j, t, te, nu: (e_base + te[t], 0, j)),
                      pl.BlockSpec((None, d, tn), lambda j, t, te, nu: (e_base + te[t], 0, j))],
            out_specs=pl.BlockSpec((tmr, tn), lambda j, t, te, nu: (t, j)),
            scratch_shapes=[pltpu.VMEM((d, tn), BF16), pltpu.VMEM((d, tn), BF16)]),
        out_shape=jax.ShapeDtypeStruct((r_rows, d_e), BF16),
        compiler_params=_params(("arbitrary", "arbitrary"), est),
        name="moe_gate_up",
    )(tile_expert, n_used, xg, w_g, w_u)
    est = 2 * (_nbytes((tmr, d_e), BF16) + _nbytes((d_e, tn), F32) + _nbytes((tmr, tn), F32)) \
        + _nbytes((d_e, tn), BF16) + 2 * _nbytes((tmr, tn), F32)
    return pl.pallas_call(
        _moe_down_kernel,
        grid_spec=pltpu.PrefetchScalarGridSpec(
            num_scalar_prefetch=2, grid=(d // tn, n_tiles),
            in_specs=[pl.BlockSpec((tmr, d_e), lambda j, t, te, nu: (t, 0)),
                      pl.BlockSpec((None, d_e, tn), lambda j, t, te, nu: (e_base + te[t], 0, j)),
                      pl.BlockSpec((tmr, 1), lambda j, t, te, nu: (t, 0))],
            out_specs=pl.BlockSpec((tmr, tn), lambda j, t, te, nu: (t, j)),
            scratch_shapes=[pltpu.VMEM((d_e, tn), BF16)]),
        out_shape=jax.ShapeDtypeStruct((r_rows, d), F32),
        compiler_params=_params(("arbitrary", "arbitrary"), est),
        name="moe_down",
    )(tile_expert, n_used, hidden, w_d, row_w)


def _moe_combine_kernel(x_ref, y1_ref, y2_ref, gate_ref, o_ref):
    o_ref[...] = x_ref[...] + gate_ref[...] * (y1_ref[...] + y2_ref[...])


def moe_route(logits, n_experts, tmr):
    n_tok = logits.shape[0]
    lg = logits[:, :n_experts]
    top_v, top_i = lax.top_k(lg, TOP_K)
    top_w = jax.nn.softmax(top_v, axis=-1)
    flat_e = top_i.reshape(-1)
    onehot = (flat_e[:, None] == jnp.arange(n_experts)[None, :]).astype(jnp.int32)
    rank = jnp.take_along_axis(jnp.cumsum(onehot, axis=0) - onehot, flat_e[:, None], axis=1)[:, 0]
    counts = jnp.sum(onehot, axis=0)
    tiles_per = (counts + tmr - 1) // tmr
    tile_start = jnp.cumsum(tiles_per) - tiles_per
    dest = tile_start[flat_e] * tmr + rank
    n_tiles = (n_tok * TOP_K) // tmr + n_experts
    r_rows = n_tiles * tmr
    src_tok = jnp.zeros((r_rows,), jnp.int32).at[dest].set(jnp.arange(n_tok * TOP_K, dtype=jnp.int32) // TOP_K)
    row_w = jnp.zeros((r_rows,), F32).at[dest].set(top_w.reshape(-1))
    tile_ids = jnp.arange(n_tiles, dtype=jnp.int32)
    tile_expert = jnp.sum((tile_ids[:, None] >= jnp.cumsum(tiles_per)[None, :]).astype(jnp.int32), axis=1)
    n_used = jnp.sum(tiles_per).astype(jnp.int32)
    tile_expert = jnp.where(tile_ids < n_used, tile_expert, tile_expert[jnp.maximum(n_used - 1, 0)])
    tile_expert = jnp.minimum(tile_expert, n_experts - 1).astype(jnp.int32)
    return src_tok, row_w.reshape(r_rows, 1), tile_expert, n_used.reshape(1), dest.reshape(n_tok, TOP_K)


def moe_ffn_residual(x, h, logits, mod, gate_idx, w_g, w_u, w_d, layer_j, *, rows, seq, batch):
    d = x.shape[1]
    n_moe, n_experts, _, d_e = w_g.shape
    tmr = 512 if rows * TOP_K >= 8 * 512 * n_experts else 128
    tn = 512 if d_e % 512 == 0 and d >= 4096 else 256
    src_tok, row_w, tile_expert, n_used, dest = moe_route(logits, n_experts, tmr)
    xg = jnp.take(h, src_tok, axis=0, mode="clip")
    y = moe_experts(xg, row_w, tile_expert, n_used,
                    w_g.reshape(n_moe * n_experts, d, d_e), w_u.reshape(n_moe * n_experts, d, d_e),
                    w_d.reshape(n_moe * n_experts, d_e, d), layer_j * n_experts, tmr=tmr, tn=tn)
    y1 = jnp.take(y, dest[:, 0], axis=0, mode="clip")
    y2 = jnp.take(y, dest[:, 1], axis=0, mode="clip")
    tm = 256
    return pl.pallas_call(
        _moe_combine_kernel,
        grid=(rows // tm,),
        in_specs=[pl.BlockSpec((tm, d), lambda i: (i, 0)),
                  pl.BlockSpec((tm, d), lambda i: (i, 0)),
                  pl.BlockSpec((tm, d), lambda i: (i, 0)),
                  pl.BlockSpec((None, 1, d), lambda i: (_mod_row(i, tm, seq, batch), 0, gate_idx))],
        out_specs=pl.BlockSpec((tm, d), lambda i: (i, 0)),
        out_shape=jax.ShapeDtypeStruct((rows, d), F32),
        compiler_params=_params(("parallel",), 8 * _nbytes((tm, d), F32)),
        name="moe_combine",
    )(x, y1, y2, mod)


def _rope_tables(seq, pad_rows):
    rows = seq // GRID_W
    half = HEAD_DIM // 2
    row = jnp.broadcast_to(jnp.arange(rows)[:, None], (rows, GRID_W)).reshape(-1).astype(F32)
    col = jnp.broadcast_to(jnp.arange(GRID_W)[None, :], (rows, GRID_W)).reshape(-1).astype(F32)
    inv_freq = ROPE_BASE ** (-jnp.arange(0, half, 2, dtype=F32) / half)
    ang_r = row[:, None] * inv_freq
    ang_c = col[:, None] * inv_freq
    ang = jnp.concatenate([ang_r, ang_r, ang_c, ang_c], axis=-1)
    cos, sin = jnp.cos(ang), jnp.sin(ang)
    first = (jnp.arange(HEAD_DIM) % half) < half // 2
    ones = jnp.ones((pad_rows, HEAD_DIM), F32)
    zeros = jnp.zeros((pad_rows, HEAD_DIM), F32)
    return (jnp.concatenate([cos, ones]), jnp.concatenate([jnp.where(first, -sin, 0.0), zeros]),
            jnp.concatenate([jnp.where(first, 0.0, sin), zeros]))


def kernel(x, c, ctx, c_ctx, w_mod, b_mod, g_attn_norm, w_in, attn_sink, ssm_a_re, ssm_a_im, ssm_log_dt,
           ssm_b_re, ssm_b_im, ssm_c_re, ssm_c_im, ssm_d, w_glu, g_out_attn, g_out_ssm, w_out, g_ffn_norm,
           w_ff_gate, w_ff_up, w_ff_down, w_router, w_exp_gate, w_exp_up, w_exp_down, g_final):
    batch, seq, d = x.shape
    n_ctx = ctx.shape[1]
    depth = w_in.shape[0]
    n_q = attn_sink.shape[1]
    n_kv = n_q // Q_PER_KV
    attn_w = n_q * HEAD_DIM
    kv_w = n_kv * HEAD_DIM
    rows_x = batch * seq
    rows_all = rows_x + batch * n_ctx
    big = rows_x >= 8192
    tm_big = 1024 if big else 256
    tm_mid = 512 if big else 256
    tn = 512 if big else 256
    geo = dict(seq=seq, batch=batch)

    xs = jnp.concatenate([x.reshape(rows_x, d), ctx.reshape(batch * n_ctx, d)], axis=0)
    cond = jnp.concatenate([c, c_ctx[None, :]], axis=0)
    rope_tabs = _rope_tables(seq, tm_big)
    rope_geo = dict(rows=rows_all, rows_x=rows_x, seq=seq, tm=tm_big, tn=tn)

    for i in range(depth):
        last = i == depth - 1
        rows_out = rows_x if last else rows_all
        mod = ada_modulation(cond, w_mod, b_mod, i)
        h = norm_modulate(xs, g_attn_norm[i], mod, 0, 1, rows=rows_all, **geo)
        w_in_b = w_in[i].astype(BF16)
        q_rot, q_plain = matmul_rope(h, w_in_b[:, :attn_w], rope_tabs, n_rope_tiles=attn_w // tn,
                                     scale=HEAD_DIM ** -0.5, both=True, **rope_geo)
        kv = matmul_rope(h, w_in_b[:, attn_w:attn_w + 2 * kv_w], rope_tabs, n_rope_tiles=kv_w // tn,
                         scale=1.0, both=False, **rope_geo)
        u = matmul(h, w_in_b[:, attn_w + 2 * kv_w:], rows=rows_all, tm=tm_big, tn=tn, out_dtype=F32)
        o_attn = attention(q_rot, q_plain, kv, attn_sink[i], batch=batch, seq=seq, n_ctx=n_ctx, n_kv=n_kv,
                           ctx_queries=not last)
        tables = _ssm_tables(ssm_a_re[i], ssm_a_im[i], ssm_log_dt[i], ssm_b_re[i], ssm_b_im[i],
                             ssm_c_re[i], ssm_c_im[i])
        y = s5_mix(u, tables, ssm_d[i], batch=batch, seq=seq, n_ctx=n_ctx)
        o_ssm = matmul_glu(y, w_glu[i].astype(BF16), rows=rows_out, tm=tm_big, tn=tn)
        cat = merge_norm(o_attn, o_ssm, g_out_attn[i], g_out_ssm[i], rows=rows_out)
        xs = matmul_gated_residual(cat, w_out[i].astype(BF16), xs, mod, 2, rows=rows_out, tm=tm_big, tn=tn,
                                   weight_outer=False, **geo)
        j = i // 2
        if i % 2 == 0:
            h2 = norm_modulate(xs, g_ffn_norm[i], mod, 3, 4, rows=rows_out, **geo)
            hid = matmul_swiglu(h2, w_ff_gate[j].astype(BF16), w_ff_up[j].astype(BF16),
                                rows=rows_out, tm=tm_big, tn=tn)
            xs = matmul_gated_residual(hid, w_ff_down[j].astype(BF16), xs, mod, 5, rows=rows_out, tm=tm_mid,
                                       tn=tn, weight_outer=True, **geo)
        else:
            n_e = w_router.shape[2]
            wr = jnp.zeros((d, LANES), F32).at[:, :n_e].set(w_router[j])
            h2, logits = norm_modulate(xs, g_ffn_norm[i], mod, 3, 4, rows=rows_out, w_router=wr, **geo)
            xs = moe_ffn_residual(xs, h2, logits, mod, 5, w_exp_gate, w_exp_up, w_exp_down, j,
                                  rows=rows_out, seq=seq, batch=batch)
    return final_norm(xs, g_final, rows=rows_x).reshape(batch, seq, d)
```

```python
import functools
import math

import jax
import jax.numpy as jnp
from jax import lax
from jax.experimental import pallas as pl
from jax.experimental.pallas import tpu as pltpu

F32 = jnp.float32
BF16 = jnp.bfloat16

HEAD_DIM = 128
BLOCK = 128
Q_PER_KV = 3
GRID_W = 64
ROPE_BASE = 10000.0
SSM_GROUP = 16
N_MOD = 6
TOP_K = 2
EPS = 1e-6
NEG_INF = -1e30

LANES = 128
SUBLANES = 8
VMEM_BYTES_V7X = 64 * 1024 * 1024
VMEM_CAP = VMEM_BYTES_V7X - 8 * 1024 * 1024

SSM_CHUNK = 16


def _params(sem, est_bytes):
    limit = int(min(VMEM_CAP, max(32 * 1024 * 1024, est_bytes * 5 // 4)))
    return pltpu.CompilerParams(dimension_semantics=sem, vmem_limit_bytes=limit)


def _nbytes(shape, dtype):
    return math.prod(shape) * jnp.dtype(dtype).itemsize


def _mod_row(i, tm, seq, batch):
    return jnp.minimum((i * tm) // seq, batch)


def _mod_kernel(cb_ref, w_ref, b_ref, o_ref, *, n_rows, tn):
    k_dim = w_ref.shape[0]
    n_chunks = tn // LANES

    def body(kb, accs):
        k0 = pl.multiple_of(kb * SUBLANES, SUBLANES)
        wv = w_ref[pl.ds(k0, SUBLANES), :]
        new = []
        for r in range(n_rows):
            cv = cb_ref[r, pl.ds(k0, SUBLANES), :]
            sv = cv * jax.nn.sigmoid(cv)
            for c in range(n_chunks):
                new.append(accs[r * n_chunks + c] + wv[:, c * LANES:(c + 1) * LANES] * sv)
        return tuple(new)

    init = tuple(jnp.zeros((SUBLANES, LANES), F32) for _ in range(n_rows * n_chunks))
    accs = lax.fori_loop(0, k_dim // SUBLANES, body, init, unroll=4)
    o_ref[...] = jnp.zeros_like(o_ref)
    for r in range(n_rows):
        row = jnp.concatenate(
            [jnp.sum(accs[r * n_chunks + c], axis=0, keepdims=True) for c in range(n_chunks)], axis=1)
        o_ref[r:r + 1, :] = row + b_ref[...]


def ada_modulation(cond_rows, w_mod, b_mod, layer):
    n_rows, d = cond_rows.shape
    depth, _, n = w_mod.shape
    tn = 512
    cb = jnp.broadcast_to(cond_rows[:, :, None], (n_rows, d, LANES))
    est = 2 * (_nbytes((d, tn), F32) + _nbytes((n_rows, d, LANES), F32))
    out = pl.pallas_call(
        functools.partial(_mod_kernel, n_rows=n_rows, tn=tn),
        grid=(n // tn,),
        in_specs=[pl.BlockSpec((n_rows, d, LANES), lambda j: (0, 0, 0)),
                  pl.BlockSpec((None, d, tn), lambda j: (layer, 0, j)),
                  pl.BlockSpec((None, 1, tn), lambda j: (layer, 0, j))],
        out_specs=pl.BlockSpec((SUBLANES, tn), lambda j: (0, j)),
        out_shape=jax.ShapeDtypeStruct((SUBLANES, n), F32),
        compiler_params=_params(("parallel",), est),
        name="ada_modulation",
    )(cb, w_mod, b_mod.reshape(depth, 1, n))
    return out.reshape(SUBLANES, 1, n)


def _split3(v):
    hi = v.astype(BF16)
    r1 = v - hi.astype(F32)
    mid = r1.astype(BF16)
    lo = (r1 - mid.astype(F32)).astype(BF16)
    return hi, mid, lo


def _normmod_kernel(x_ref, g_ref, shift_ref, scale_ref, *rest, with_router):
    xf = x_ref[...]
    y = xf * lax.rsqrt(jnp.mean(xf * xf, axis=-1, keepdims=True) + EPS)
    h = (y * g_ref[...]) * (1.0 + scale_ref[...]) + shift_ref[...]
    if with_router:
        wr_ref, o_ref, lg_ref = rest
        hs = _split3(h)
        ws = _split3(wr_ref[...])
        acc = None
        for a, b in ((0, 0), (0, 1), (1, 0), (1, 1), (0, 2), (2, 0)):
            t = jnp.dot(hs[a], ws[b], preferred_element_type=F32)
            acc = t if acc is None else acc + t
        lg_ref[...] = acc
    else:
        (o_ref,) = rest
    o_ref[...] = h.astype(BF16)


def norm_modulate(x, g, mod, shift_idx, scale_idx, *, rows, seq, batch, w_router=None):
    d = x.shape[1]
    tm = 256
    with_router = w_router is not None
    in_specs = [pl.BlockSpec((tm, d), lambda i: (i, 0)),
                pl.BlockSpec((1, d), lambda i: (0, 0)),
                pl.BlockSpec((None, 1, d), lambda i: (_mod_row(i, tm, seq, batch), 0, shift_idx)),
                pl.BlockSpec((None, 1, d), lambda i: (_mod_row(i, tm, seq, batch), 0, scale_idx))]
    args = [x, g.reshape(1, d), mod, mod]
    out_specs = [pl.BlockSpec((tm, d), lambda i: (i, 0))]
    out_shape = [jax.ShapeDtypeStruct((rows, d), BF16)]
    if with_router:
        in_specs.append(pl.BlockSpec((d, LANES), lambda i: (0, 0)))
        args.append(w_router)
        out_specs.append(pl.BlockSpec((tm, LANES), lambda i: (i, 0)))
        out_shape.append(jax.ShapeDtypeStruct((rows, LANES), F32))
    est = 2 * (_nbytes((tm, d), F32) + _nbytes((tm, d), BF16)) + 8 * _nbytes((tm, d), F32)
    outs = pl.pallas_call(
        functools.partial(_normmod_kernel, with_router=with_router),
        grid=(pl.cdiv(rows, tm),),
        in_specs=in_specs, out_specs=out_specs, out_shape=out_shape,
        compiler_params=_params(("parallel",), est),
        name="norm_modulate_router" if with_router else "norm_modulate",
    )(*args)
    return outs if with_router else outs[0]


def _mm_kernel(x_ref, w_ref, o_ref):
    o_ref[...] = jnp.dot(x_ref[...], w_ref[...], preferred_element_type=F32).astype(o_ref.dtype)


def _mm_swiglu_kernel(x_ref, wg_ref, wu_ref, o_ref):
    x = x_ref[...]
    a = jnp.dot(x, wg_ref[...], preferred_element_type=F32)
    b = jnp.dot(x, wu_ref[...], preferred_element_type=F32)
    o_ref[...] = (a * jax.nn.sigmoid(a) * b).astype(o_ref.dtype)


def _rms(x, g):
    return x * lax.rsqrt(jnp.mean(x * x, axis=-1, keepdims=True) + EPS) * g


def _mm_resid_kernel(*refs, n_parts):
    x_refs, w_refs = refs[:n_parts], refs[n_parts:2 * n_parts]
    r_ref, gate_ref, o_ref = refs[2 * n_parts:]
    acc = None
    for x_ref, w_ref in zip(x_refs, w_refs):
        t = jnp.dot(x_ref[...], w_ref[...], preferred_element_type=F32)
        acc = t if acc is None else acc + t
    o_ref[...] = r_ref[...] + gate_ref[...] * acc


def _mm_glu_norm_kernel(y_ref, w_ref, g_ref, o_ref):
    y = y_ref[...]
    z = jnp.dot(y.astype(BF16), w_ref[...], preferred_element_type=F32)
    o_ref[...] = _rms(y * jax.nn.sigmoid(z), g_ref[...]).astype(BF16)


def matmul(x, w, *, rows, tm, tn, out_dtype):
    k, n = w.shape
    est = 2 * (_nbytes((tm, k), x.dtype) + _nbytes((k, tn), w.dtype) + _nbytes((tm, tn), out_dtype))
    return pl.pallas_call(
        _mm_kernel,
        grid=(pl.cdiv(rows, tm), pl.cdiv(n, tn)),
        in_specs=[pl.BlockSpec((tm, k), lambda i, j: (i, 0)),
                  pl.BlockSpec((k, tn), lambda i, j: (0, j))],
        out_specs=pl.BlockSpec((tm, tn), lambda i, j: (i, j)),
        out_shape=jax.ShapeDtypeStruct((rows, n), out_dtype),
        compiler_params=_params(("parallel", "parallel"), est),
        name="matmul",
    )(x, w)


def _rope(x, cos, sin_lo, sin_hi):
    return (x * cos + pltpu.roll(x, HEAD_DIM - HEAD_DIM // 4, 1) * sin_lo
            + pltpu.roll(x, HEAD_DIM // 4, 1) * sin_hi)


def _mm_rope_kernel(x_ref, w_ref, cos_ref, slo_ref, shi_ref, *o_refs, n_rope_tiles, scale, both, sub):
    j = pl.program_id(1)
    tm, tn = o_refs[0].shape
    for r0 in range(0, tm, sub):
        acc = jnp.dot(x_ref[r0:r0 + sub, :], w_ref[...], preferred_element_type=F32)
        if scale != 1.0:
            acc = acc * scale
        cos, slo, shi = (r[r0:r0 + sub, :] for r in (cos_ref, slo_ref, shi_ref))
        rot = jnp.concatenate([_rope(acc[:, h * HEAD_DIM:(h + 1) * HEAD_DIM], cos, slo, shi)
                               for h in range(tn // HEAD_DIM)], axis=1)
        if both:
            o_refs[0][r0:r0 + sub, :] = rot.astype(BF16)
            o_refs[1][r0:r0 + sub, :] = acc.astype(BF16)
        else:
            o_refs[0][r0:r0 + sub, :] = jnp.where(j < n_rope_tiles, rot, acc).astype(BF16)


def matmul_rope(x, w, tabs, *, rows, rows_x, seq, tm, tn, n_rope_tiles, scale, both):
    k, n = w.shape
    per_seq = seq // tm

    def tab_map(i, j):
        return (jnp.where(i * tm < rows_x, i % per_seq, per_seq), 0)

    n_out = 2 if both else 1
    est = 2 * (_nbytes((tm, k), BF16) + _nbytes((k, tn), BF16) + n_out * _nbytes((tm, tn), BF16)
               + 3 * _nbytes((tm, HEAD_DIM), F32)) + 4 * _nbytes((tm, tn), F32)
    out = pl.pallas_call(
        functools.partial(_mm_rope_kernel, n_rope_tiles=n_rope_tiles, scale=scale, both=both, sub=min(tm, 256)),
        grid=(pl.cdiv(rows, tm), n // tn),
        in_specs=[pl.BlockSpec((tm, k), lambda i, j: (i, 0)),
                  pl.BlockSpec((k, tn), lambda i, j: (0, j))] + [pl.BlockSpec((tm, HEAD_DIM), tab_map)] * 3,
        out_specs=[pl.BlockSpec((tm, tn), lambda i, j: (i, j))] * n_out,
        out_shape=[jax.ShapeDtypeStruct((rows, n), BF16)] * n_out,
        compiler_params=_params(("parallel", "parallel"), est),
        name="matmul_rope",
    )(x, w, *tabs)
    return out if both else out[0]


def matmul_swiglu(x, wg, wu, *, rows, tm, tn):
    k, n = wg.shape
    est = 2 * (_nbytes((tm, k), BF16) + 2 * _nbytes((k, tn), BF16) + _nbytes((tm, tn), BF16)) \
        + 3 * _nbytes((tm, tn), F32)
    return pl.pallas_call(
        _mm_swiglu_kernel,
        grid=(pl.cdiv(rows, tm), pl.cdiv(n, tn)),
        in_specs=[pl.BlockSpec((tm, k), lambda i, j: (i, 0)),
                  pl.BlockSpec((k, tn), lambda i, j: (0, j)),
                  pl.BlockSpec((k, tn), lambda i, j: (0, j))],
        out_specs=pl.BlockSpec((tm, tn), lambda i, j: (i, j)),
        out_shape=jax.ShapeDtypeStruct((rows, n), BF16),
        compiler_params=_params(("parallel", "parallel"), est),
        name="matmul_swiglu",
    )(x, wg, wu)


def matmul_gated_residual(xw, resid, mod, gate_idx, *, rows, tm, tn, seq, batch, weight_outer):
    xs_, ws_ = [p[0] for p in xw], [p[1] for p in xw]
    n = ws_[0].shape[1]
    k = sum(w.shape[0] for w in ws_)
    nj = n // tn
    if weight_outer:
        grid = (nj, pl.cdiv(rows, tm))
        ij = lambda a, b: (b, a)
    else:
        grid = (pl.cdiv(rows, tm), nj)
        ij = lambda a, b: (a, b)

    def gate_map(a, b):
        i, j = ij(a, b)
        return (_mod_row(i, tm, seq, batch), 0, gate_idx * nj + j)

    est = 2 * (_nbytes((tm, k), BF16) + _nbytes((k, tn), BF16) + 2 * _nbytes((tm, tn), F32))
    return pl.pallas_call(
        functools.partial(_mm_resid_kernel, n_parts=len(xw)),
        grid=grid,
        in_specs=[pl.BlockSpec((tm, x.shape[1]), lambda a, b: (ij(a, b)[0], 0)) for x in xs_]
        + [pl.BlockSpec((w.shape[0], tn), lambda a, b: (0, ij(a, b)[1])) for w in ws_]
        + [pl.BlockSpec((tm, tn), lambda a, b: ij(a, b)),
           pl.BlockSpec((None, 1, tn), gate_map)],
        out_specs=pl.BlockSpec((tm, tn), lambda a, b: ij(a, b)),
        out_shape=jax.ShapeDtypeStruct((rows, n), F32),
        compiler_params=_params(("parallel", "parallel"), est),
        name="matmul_gated_residual",
    )(*xs_, *ws_, resid, mod)


def matmul_glu_norm(y, w, g, *, rows, tm):
    k, n = w.shape
    est = 2 * (_nbytes((tm, k), F32) + _nbytes((k, n), BF16) + _nbytes((tm, n), BF16)) + 4 * _nbytes((tm, n), F32)
    return pl.pallas_call(
        _mm_glu_norm_kernel,
        grid=(pl.cdiv(rows, tm),),
        in_specs=[pl.BlockSpec((tm, k), lambda i: (i, 0)),
                  pl.BlockSpec((k, n), lambda i: (0, 0)),
                  pl.BlockSpec((1, n), lambda i: (0, 0))],
        out_specs=pl.BlockSpec((tm, n), lambda i: (i, 0)),
        out_shape=jax.ShapeDtypeStruct((rows, n), BF16),
        compiler_params=_params(("parallel",), est),
        name="matmul_glu_norm",
    )(y, w, g.reshape(1, n))


def _nt_dot(a, b):
    return lax.dot_general(a, b, (((1,), (1,)), ((), ())), preferred_element_type=F32)


def _attn_kernel(sink_ref, qr_ref, qp_ref, kp_ref, k0_ref, kn_ref, vp_ref, v0_ref, vn_ref, kc_ref, vc_ref,
                 bias_ref, g_ref, o_ref, o_scr, *, n_blocks, ctx_queries, n_kv):
    n = pl.program_id(1)
    qw = Q_PER_KV * HEAD_DIM

    def stack_heads(ref, hh):
        return jnp.concatenate([ref[:, hh * qw + g * HEAD_DIM:hh * qw + (g + 1) * HEAD_DIM]
                                for g in range(Q_PER_KV)], axis=0)

    def head(ref, hh):
        return ref[:, hh * HEAD_DIM:(hh + 1) * HEAD_DIM]

    def run_head(window, hh):
        sink = jnp.concatenate(
            [jnp.full((BLOCK, 1), sink_ref[hh * Q_PER_KV + g], F32) for g in range(Q_PER_KV)], axis=0)
        s_c = _nt_dot(stack_heads(qp_ref, hh), head(kc_ref, hh))
        m = jnp.maximum(jnp.max(s_c, axis=1, keepdims=True), sink)
        if window:
            k_win = jnp.concatenate([head(kp_ref, hh), head(k0_ref, hh), head(kn_ref, hh)], axis=0)
            s_w = _nt_dot(stack_heads(qr_ref, hh), k_win) + bias_ref[...]
            m = jnp.maximum(m, jnp.max(s_w, axis=1, keepdims=True))
        p_c = jnp.exp(s_c - m)
        den = jnp.sum(p_c, axis=1, keepdims=True) + jnp.exp(sink - m)
        o = jnp.dot(p_c.astype(BF16), head(vc_ref, hh), preferred_element_type=F32)
        if window:
            p_w = jnp.exp(s_w - m)
            den = den + jnp.sum(p_w, axis=1, keepdims=True)
            v_win = jnp.concatenate([head(vp_ref, hh), head(v0_ref, hh), head(vn_ref, hh)], axis=0)
            o = o + jnp.dot(p_w.astype(BF16), v_win, preferred_element_type=F32)
        o = o / den
        for g in range(Q_PER_KV):
            o_scr[:, hh * qw + g * HEAD_DIM:hh * qw + (g + 1) * HEAD_DIM] = o[g * BLOCK:(g + 1) * BLOCK]

    def run(window):
        for hh in range(n_kv):
            run_head(window, hh)
        o_ref[...] = _rms(o_scr[...], g_ref[...]).astype(BF16)

    if ctx_queries:
        pl.when(n < n_blocks)(lambda: run(True))
        pl.when(n >= n_blocks)(lambda: run(False))
    else:
        run(True)


def _window_bias(n_blocks):
    rows = Q_PER_KV * BLOCK
    row = (jnp.arange(rows) % BLOCK)[:, None]
    col = jnp.arange(3 * BLOCK)[None, :]
    band = (col >= row) & (col <= row + 2 * BLOCK)
    variants = []
    for code in range(4):
        ok = band
        if code & 1:
            ok = ok & (col >= BLOCK)
        if code & 2:
            ok = ok & (col < 2 * BLOCK)
        variants.append(jnp.where(ok, 0.0, NEG_INF).astype(F32))
    return jnp.stack(variants)


def attention(q_rot, q_plain, kv, sink, g_out, *, batch, seq, n_ctx, n_kv, ctx_queries):
    nb = seq // BLOCK
    qb = n_ctx // BLOCK if ctx_queries else 0
    qw = n_kv * Q_PER_KV * HEAD_DIM
    kw = n_kv * HEAD_DIM
    rows_out = batch * seq + (batch * n_ctx if ctx_queries else 0)
    ctx_blk0 = batch * seq // n_ctx
    q_blk0 = batch * seq // BLOCK

    def q_map(b, n, s):
        return (jnp.where(n < nb, b * nb + n, q_blk0 + b * qb + (n - nb)), 0)

    def kv_spec(col, shift):
        return pl.BlockSpec((BLOCK, kw), lambda b, n, s: (b * nb + jnp.clip(n + shift, 0, nb - 1), col))

    def bias_map(b, n, s):
        return ((n == 0).astype(jnp.int32) + 2 * (n == nb - 1).astype(jnp.int32), 0, 0)

    in_specs = [pl.BlockSpec((BLOCK, qw), q_map), pl.BlockSpec((BLOCK, qw), q_map),
                kv_spec(0, -1), kv_spec(0, 0), kv_spec(0, 1),
                kv_spec(1, -1), kv_spec(1, 0), kv_spec(1, 1),
                pl.BlockSpec((n_ctx, kw), lambda b, n, s: (ctx_blk0 + b, 0)),
                pl.BlockSpec((n_ctx, kw), lambda b, n, s: (ctx_blk0 + b, 1)),
                pl.BlockSpec((None, Q_PER_KV * BLOCK, 3 * BLOCK), bias_map),
                pl.BlockSpec((1, qw), lambda b, n, s: (0, 0))]
    grid_spec = pltpu.PrefetchScalarGridSpec(
        num_scalar_prefetch=1, grid=(batch, nb + qb), in_specs=in_specs,
        out_specs=pl.BlockSpec((BLOCK, qw), q_map),
        scratch_shapes=[pltpu.VMEM((BLOCK, qw), F32)])
    return pl.pallas_call(
        functools.partial(_attn_kernel, n_blocks=nb, ctx_queries=ctx_queries, n_kv=n_kv),
        grid_spec=grid_spec,
        out_shape=jax.ShapeDtypeStruct((rows_out, qw), BF16),
        compiler_params=_params(("parallel", "parallel"), 16 * 1024 * 1024),
        name="attention",
    )(sink, q_rot, q_plain, *([kv] * 8), _window_bias(nb), g_out.reshape(1, qw))


def _ssm_tables(a_re, a_im, log_dt, b_re, b_im, c_re, c_im):
    hp = lax.Precision.HIGHEST
    t_len = SSM_CHUNK
    g_per = LANES // SSM_GROUP
    n_groups, n_state = a_re.shape[1:]
    n_lb = n_groups // g_per
    dt = jnp.exp(log_dt.astype(F32))[..., None]
    lam_re, lam_im = a_re.astype(F32), a_im.astype(F32)
    mag = jnp.exp(lam_re * dt)
    abar = lax.complex(mag * jnp.cos(lam_im * dt), mag * jnp.sin(lam_im * dt))
    lam = lax.complex(lam_re, lam_im)
    bbar = ((abar - 1.0) / lam)[..., None] * lax.complex(b_re.astype(F32), b_im.astype(F32))
    cmat = lax.complex(c_re.astype(F32), c_im.astype(F32))

    def powers(d, exps):
        e = jnp.asarray(exps, F32)[:, None, None]
        m = jnp.exp(lam_re[d] * dt[d] * e)
        ph = lam_im[d] * dt[d] * e
        return lax.complex(m * jnp.cos(ph), m * jnp.sin(ph))

    ar = list(range(t_len + 1))
    pw_f, pw_b = powers(0, ar), powers(1, ar)

    kf = jnp.real(jnp.einsum('gcp,tgp,gpi->tgci', cmat[0], pw_f[:t_len], bbar[0], precision=hp))
    kb = jnp.real(jnp.einsum('gcp,tgp,gpi->tgci', cmat[1], pw_b[:t_len], bbar[1], precision=hp))
    lag = jnp.arange(2 * t_len - 1) - (t_len - 1)
    k_lag = jnp.where((lag >= 0)[:, None, None, None], kf[jnp.clip(lag, 0)], 0.0) \
        + jnp.where((lag <= 0)[:, None, None, None], kb[jnp.clip(-lag, 0)], 0.0)
    k_lag = jnp.swapaxes(k_lag, -1, -2).reshape(2 * t_len - 1, n_lb, g_per, SSM_GROUP, SSM_GROUP)
    eye = jnp.eye(g_per, dtype=F32)
    k_lag = jnp.einsum('mngab,gh->nmgahb', k_lag, eye).reshape(n_lb, 2 * t_len - 1, LANES, LANES)

    rep = LANES // n_state

    def e_coef(d, pw_sel):
        pw_rep = jnp.concatenate([pw_sel] * rep, axis=-1)
        bb_rep = jnp.concatenate([bbar[d]] * rep, axis=-2)
        return jnp.swapaxes(pw_rep[..., None] * bb_rep[None], -1, -2)

    ef, eb = e_coef(0, powers(0, [t_len - 1 - j for j in range(t_len)])), e_coef(1, pw_b[:t_len])
    quarters = jnp.stack([jnp.real(ef), jnp.imag(ef), jnp.real(eb), jnp.imag(eb)], axis=-2)
    e_c = quarters.reshape(t_len, n_lb, LANES, 4 * LANES).transpose(1, 0, 2, 3)

    def f_coef(d, pw_sel):
        return cmat[d][None] * pw_sel[:, :, None, :]

    ff, fb = f_coef(0, pw_f[1:]), f_coef(1, powers(1, [t_len - l for l in range(t_len)]))
    f_c = jnp.stack([jnp.real(ff), -jnp.imag(ff), jnp.real(fb), -jnp.imag(fb)])
    f_c = f_c.reshape(4, t_len, n_lb, g_per, SSM_GROUP, n_state).transpose(2, 0, 5, 1, 3, 4)
    f_c = f_c.reshape(n_lb, 4 * n_state, t_len * LANES)

    def lay(v):
        return v.reshape(n_lb, g_per * n_state)

    a_chunk = jnp.concatenate([lay(jnp.real(pw_f[t_len])), lay(jnp.imag(pw_f[t_len])),
                               lay(jnp.real(pw_b[t_len])), lay(jnp.imag(pw_b[t_len]))], axis=-1)
    return k_lag.astype(BF16), e_c, f_c, a_chunk.reshape(1, -1)


def _chunk_inputs(x_ref, tmc):
    return [x_ref[pl.ds(j, tmc, stride=SSM_CHUNK), :] for j in range(SSM_CHUNK)]


def _ssm_local_kernel(x_ref, ec_ref, s_ref, e_scr, *, n_state):
    tmc = s_ref.shape[0]
    qw = e_scr.shape[1] // 4

    @pl.when(pl.program_id(1) == 0)
    def _():
        row_g = lax.broadcasted_iota(jnp.int32, (LANES, qw), 0) // SSM_GROUP
        col_g = lax.broadcasted_iota(jnp.int32, (LANES, qw), 1) // n_state
        same = row_g == col_g
        for j in range(SSM_CHUNK):
            for q in range(4):
                v = ec_ref[j, :, q * LANES:(q + 1) * LANES]
                tiled = jnp.concatenate([v] * (qw // LANES), axis=1)
                e_scr[j * LANES:(j + 1) * LANES, q * qw:(q + 1) * qw] = jnp.where(same, tiled, 0.0).astype(BF16)

    xs = jnp.concatenate([x.astype(BF16) for x in _chunk_inputs(x_ref, tmc)], axis=1)
    s_ref[...] = jnp.dot(xs, e_scr[...], preferred_element_type=F32)


def _ssm_scan_kernel(s_ref, a_ref, h_ref, *, batch, nc_x, nc_c):
    sw = s_ref.shape[1] // 4
    a = a_ref[...]
    afr, afi, abr, abi = (a[:, k * sw:(k + 1) * sw] for k in range(4))
    ctx0 = batch * nc_x

    def step(row_f, row_b, st):
        new = []
        for (row, ar, ai, off, (hr, hi)) in ((row_f, afr, afi, 0, st[0]), (row_b, abr, abi, 2 * sw, st[1])):
            h_ref[pl.ds(row, 1), off:off + sw] = hr
            h_ref[pl.ds(row, 1), off + sw:off + 2 * sw] = hi
            sr = s_ref[pl.ds(row, 1), off:off + sw]
            si = s_ref[pl.ds(row, 1), off + sw:off + 2 * sw]
            new.append((ar * hr - ai * hi + sr, ar * hi + ai * hr + si))
        return tuple(new)

    zero = jnp.zeros((1, sw), F32)
    for b in range(batch):
        st = ((zero, zero), (zero, zero))
        c_base = ctx0 + b * nc_c
        st = lax.fori_loop(0, nc_c, lambda t, s: step(c_base + t, c_base + nc_c - 1 - t, s), st)
        x_base = b * nc_x
        lax.fori_loop(0, nc_x, lambda t, s: step(x_base + t, x_base + nc_x - 1 - t, s), st)


def _ssm_out_kernel(x_ref, h_ref, kl_ref, fc_ref, d_ref, o_ref, w_scr, *, n_state):
    t = SSM_CHUNK
    tmc = h_ref.shape[0]
    sdim = h_ref.shape[1]
    qw = sdim // 4

    @pl.when(pl.program_id(1) == 0)
    def _():
        for j in range(t):
            for l in range(t):
                w_scr[j * LANES:(j + 1) * LANES, l * LANES:(l + 1) * LANES] = kl_ref[l - j + t - 1]
        lane_g = (lax.broadcasted_iota(jnp.int32, (n_state, t * LANES), 1) % LANES) // SSM_GROUP
        for q in range(4):
            v = fc_ref[q * n_state:(q + 1) * n_state, :]
            for g in range(qw // n_state):
                r0 = t * LANES + q * qw + g * n_state
                w_scr[r0:r0 + n_state, :] = jnp.where(lane_g == g, v, 0.0).astype(BF16)

    xs = _chunk_inputs(x_ref, tmc)
    lhs = jnp.concatenate([x.astype(BF16) for x in xs] + [h_ref[...].astype(BF16)], axis=1)
    y = jnp.dot(lhs, w_scr[...], preferred_element_type=F32)
    d = d_ref[...]
    for l in range(t):
        o_ref[pl.ds(l, tmc, stride=t), :] = jax.nn.gelu(y[:, l * LANES:(l + 1) * LANES] + d * xs[l])


def s5_mix(u, tables, d_skip, layer, *, n_state, batch, seq, n_ctx):
    k_lag, e_c, f_c, a_chunk = tables
    rows, ssm_width = u.shape
    t = SSM_CHUNK
    n_lb = ssm_width // LANES
    rc = rows // t
    sdim = 4 * (LANES // SSM_GROUP) * n_state
    tmc = max(m for m in range(SUBLANES, min(rc, 264) + 1, SUBLANES) if rc % m == 0)
    x_spec = pl.BlockSpec((tmc * t, LANES), lambda g, r: (r, g))
    x_bytes = _nbytes((tmc * t, LANES), F32)

    s_loc = pl.pallas_call(
        functools.partial(_ssm_local_kernel, n_state=n_state),
        grid=(n_lb, rc // tmc),
        in_specs=[x_spec, pl.BlockSpec((None, None, t, LANES, 4 * LANES), lambda g, r: (layer, g, 0, 0, 0))],
        out_specs=pl.BlockSpec((tmc, sdim), lambda g, r: (r, g)),
        out_shape=jax.ShapeDtypeStruct((rc, n_lb * sdim), F32),
        scratch_shapes=[pltpu.VMEM((t * LANES, sdim), BF16)],
        compiler_params=_params(("arbitrary", "arbitrary"),
                                2 * x_bytes + 2 * _nbytes((t, LANES, 4 * LANES), F32)
                                + _nbytes((t * LANES, sdim), BF16) + 4 * _nbytes((tmc, sdim), F32)),
        name="s5_local_states",
    )(u, e_c)

    h_in = pl.pallas_call(
        functools.partial(_ssm_scan_kernel, batch=batch, nc_x=seq // t, nc_c=n_ctx // t),
        grid=(n_lb,),
        in_specs=[pl.BlockSpec((rc, sdim), lambda g: (0, g)),
                  pl.BlockSpec((None, 1, sdim), lambda g: (layer, 0, g))],
        out_specs=pl.BlockSpec((rc, sdim), lambda g: (0, g)),
        out_shape=jax.ShapeDtypeStruct((rc, n_lb * sdim), F32),
        compiler_params=_params(("parallel",), 4 * _nbytes((rc, sdim), F32)),
        name="s5_chunk_scan",
    )(s_loc, a_chunk)

    return pl.pallas_call(
        functools.partial(_ssm_out_kernel, n_state=n_state),
        grid=(n_lb, rc // tmc),
        in_specs=[x_spec,
                  pl.BlockSpec((tmc, sdim), lambda g, r: (r, g)),
                  pl.BlockSpec((None, None, 2 * t - 1, LANES, LANES), lambda g, r: (layer, g, 0, 0, 0)),
                  pl.BlockSpec((None, None, 4 * n_state, t * LANES), lambda g, r: (layer, g, 0, 0)),
                  pl.BlockSpec((1, LANES), lambda g, r: (0, g))],
        out_specs=x_spec,
        out_shape=jax.ShapeDtypeStruct((rows, ssm_width), F32),
        scratch_shapes=[pltpu.VMEM((t * LANES + sdim, t * LANES), BF16)],
        compiler_params=_params(("arbitrary", "arbitrary"),
                                4 * x_bytes + 2 * _nbytes((tmc, sdim), F32)
                                + 2 * _nbytes((2 * t - 1, LANES, LANES), BF16)
                                + 2 * _nbytes((4 * n_state, t * LANES), F32)
                                + _nbytes((t * LANES + sdim, t * LANES), BF16) + 4 * _nbytes((tmc, sdim), F32)),
        name="s5_readout",
    )(u, h_in, k_lag, f_c, d_skip.reshape(1, ssm_width))


def _final_norm_kernel(x_ref, g_ref, o_ref):
    o_ref[...] = _rms(x_ref[...], g_ref[...])


def final_norm(x, g, *, rows):
    d = x.shape[1]
    tm = 256
    return pl.pallas_call(
        _final_norm_kernel,
        grid=(pl.cdiv(rows, tm),),
        in_specs=[pl.BlockSpec((tm, d), lambda i: (i, 0)), pl.BlockSpec((1, d), lambda i: (0, 0))],
        out_specs=pl.BlockSpec((tm, d), lambda i: (i, 0)),
        out_shape=jax.ShapeDtypeStruct((rows, d), F32),
        compiler_params=_params(("parallel",), 6 * _nbytes((tm, d), F32)),
        name="final_norm",
    )(x, g.reshape(1, d))


def _new_expert(te_ref, t):
    return jnp.logical_or(t == 0, te_ref[t] != te_ref[jnp.maximum(t - 1, 0)])


def _moe_up_kernel(te_ref, nu_ref, x_ref, wg_ref, wu_ref, o_ref, wg_scr, wu_scr):
    t = pl.program_id(1)

    @pl.when(_new_expert(te_ref, t))
    def _():
        wg_scr[...] = wg_ref[...].astype(BF16)
        wu_scr[...] = wu_ref[...].astype(BF16)

    @pl.when(t < nu_ref[0])
    def _():
        x = x_ref[...]
        a = jnp.dot(x, wg_scr[...], preferred_element_type=F32)
        b = jnp.dot(x, wu_scr[...], preferred_element_type=F32)
        o_ref[...] = (a * jax.nn.sigmoid(a) * b).astype(o_ref.dtype)

    @pl.when(t >= nu_ref[0])
    def _():
        o_ref[...] = jnp.zeros_like(o_ref)


def _moe_down_kernel(te_ref, nu_ref, h_ref, wd_ref, rw_ref, o_ref, wd_scr):
    t = pl.program_id(1)

    @pl.when(_new_expert(te_ref, t))
    def _():
        wd_scr[...] = wd_ref[...].astype(BF16)

    @pl.when(t < nu_ref[0])
    def _():
        acc = jnp.dot(h_ref[...], wd_scr[...], preferred_element_type=F32)
        o_ref[...] = acc * rw_ref[...]

    @pl.when(t >= nu_ref[0])
    def _():
        o_ref[...] = jnp.zeros_like(o_ref)


def moe_experts(xg, row_w, tile_expert, n_used, w_g, w_u, w_d, e_base, *, tmr, tn):
    r_rows, d = xg.shape
    d_e = w_g.shape[2]
    n_tiles = r_rows // tmr
    est = 2 * (_nbytes((tmr, d), BF16) + 2 * _nbytes((d, tn), F32) + _nbytes((tmr, tn), BF16)) \
        + 2 * _nbytes((d, tn), BF16) + 3 * _nbytes((tmr, tn), F32)
    hidden = pl.pallas_call(
        _moe_up_kernel,
        grid_spec=pltpu.PrefetchScalarGridSpec(
            num_scalar_prefetch=2, grid=(d_e // tn, n_tiles),
            in_specs=[pl.BlockSpec((tmr, d), lambda j, t, te, nu: (t, 0)),
                      pl.BlockSpec((None, d, tn), lambda j, t, te, nu: (e_base + te[t], 0, j)),
                      pl.BlockSpec((None, d, tn), lambda j, t, te, nu: (e_base + te[t], 0, j))],
            out_specs=pl.BlockSpec((tmr, tn), lambda j, t, te, nu: (t, j)),
            scratch_shapes=[pltpu.VMEM((d, tn), BF16), pltpu.VMEM((d, tn), BF16)]),
        out_shape=jax.ShapeDtypeStruct((r_rows, d_e), BF16),
        compiler_params=_params(("arbitrary", "arbitrary"), est),
        name="moe_gate_up",
    )(tile_expert, n_used, xg, w_g, w_u)
    est = 2 * (_nbytes((tmr, d_e), BF16) + _nbytes((d_e, tn), F32) + _nbytes((tmr, tn), F32)) \
        + _nbytes((d_e, tn), BF16) + 2 * _nbytes((tmr, tn), F32)
    return pl.pallas_call(
        _moe_down_kernel,
        grid_spec=pltpu.PrefetchScalarGridSpec(
            num_scalar_prefetch=2, grid=(d // tn, n_tiles),
            in_specs=[pl.BlockSpec((tmr, d_e), lambda j, t, te, nu: (t, 0)),
                      pl.BlockSpec((None, d_e, tn), lambda j, t, te, nu: (e_base + te[t], 0, j)),
                      pl.BlockSpec((tmr, 1), lambda j, t, te, nu: (t, 0))],
            out_specs=pl.BlockSpec((tmr, tn), lambda j, t, te, nu: (t, j)),
            scratch_shapes=[pltpu.VMEM((d_e, tn), BF16)]),
        out_shape=jax.ShapeDtypeStruct((r_rows, d), F32),
        compiler_params=_params(("arbitrary", "arbitrary"), est),
        name="moe_down",
    )(tile_expert, n_used, hidden, w_d, row_w)


def _moe_combine_kernel(x_ref, y1_ref, y2_ref, gate_ref, *rest, final):
    out = x_ref[...] + gate_ref[...] * (y1_ref[...] + y2_ref[...])
    if final:
        g_ref, o_ref = rest
        o_ref[...] = _rms(out, g_ref[...])
    else:
        (o_ref,) = rest
        o_ref[...] = out


def moe_route(logits, n_experts, tmr):
    n_tok = logits.shape[0]
    lg = logits[:, :n_experts]
    top_v, top_i = lax.top_k(lg, TOP_K)
    top_w = jax.nn.softmax(top_v, axis=-1)
    flat_e = top_i.reshape(-1)
    onehot = (flat_e[:, None] == jnp.arange(n_experts)[None, :]).astype(jnp.int32)
    rank = jnp.take_along_axis(jnp.cumsum(onehot, axis=0) - onehot, flat_e[:, None], axis=1)[:, 0]
    counts = jnp.sum(onehot, axis=0)
    tiles_per = (counts + tmr - 1) // tmr
    tile_start = jnp.cumsum(tiles_per) - tiles_per
    dest = tile_start[flat_e] * tmr + rank
    n_tiles = (n_tok * TOP_K) // tmr + n_experts
    r_rows = n_tiles * tmr
    src_tok = jnp.zeros((r_rows,), jnp.int32).at[dest].set(jnp.arange(n_tok * TOP_K, dtype=jnp.int32) // TOP_K)
    row_w = jnp.zeros((r_rows,), F32).at[dest].set(top_w.reshape(-1))
    tile_ids = jnp.arange(n_tiles, dtype=jnp.int32)
    tile_expert = jnp.sum((tile_ids[:, None] >= jnp.cumsum(tiles_per)[None, :]).astype(jnp.int32), axis=1)
    n_used = jnp.sum(tiles_per).astype(jnp.int32)
    tile_expert = jnp.where(tile_ids < n_used, tile_expert, tile_expert[jnp.maximum(n_used - 1, 0)])
    tile_expert = jnp.minimum(tile_expert, n_experts - 1).astype(jnp.int32)
    return src_tok, row_w.reshape(r_rows, 1), tile_expert, n_used.reshape(1), dest.reshape(n_tok, TOP_K)


def moe_ffn_residual(x, h, logits, mod, gate_idx, w_g, w_u, w_d, layer_j, *, rows, seq, batch, final_g=None):
    d = x.shape[1]
    n_moe, n_experts, _, d_e = w_g.shape
    tmr = 512 if rows * TOP_K >= 8 * 512 * n_experts else 128
    tn = 512 if d_e % 512 == 0 and d >= 4096 else 256
    src_tok, row_w, tile_expert, n_used, dest = moe_route(logits, n_experts, tmr)
    xg = jnp.take(h, src_tok, axis=0, mode="clip")
    y = moe_experts(xg, row_w, tile_expert, n_used,
                    w_g.reshape(n_moe * n_experts, d, d_e), w_u.reshape(n_moe * n_experts, d, d_e),
                    w_d.reshape(n_moe * n_experts, d_e, d), layer_j * n_experts, tmr=tmr, tn=tn)
    y1 = jnp.take(y, dest[:, 0], axis=0, mode="clip")
    y2 = jnp.take(y, dest[:, 1], axis=0, mode="clip")
    tm = 256
    final = final_g is not None
    in_specs = [pl.BlockSpec((tm, d), lambda i: (i, 0)),
                pl.BlockSpec((tm, d), lambda i: (i, 0)),
                pl.BlockSpec((tm, d), lambda i: (i, 0)),
                pl.BlockSpec((None, 1, d), lambda i: (_mod_row(i, tm, seq, batch), 0, gate_idx))]
    args = [x, y1, y2, mod]
    if final:
        in_specs.append(pl.BlockSpec((1, d), lambda i: (0, 0)))
        args.append(final_g.reshape(1, d))
    return pl.pallas_call(
        functools.partial(_moe_combine_kernel, final=final),
        grid=(rows // tm,),
        in_specs=in_specs,
        out_specs=pl.BlockSpec((tm, d), lambda i: (i, 0)),
        out_shape=jax.ShapeDtypeStruct((rows, d), F32),
        compiler_params=_params(("parallel",), 10 * _nbytes((tm, d), F32)),
        name="moe_combine",
    )(*args)


def _rope_tables(seq, pad_rows):
    rows = seq // GRID_W
    half = HEAD_DIM // 2
    row = jnp.broadcast_to(jnp.arange(rows)[:, None], (rows, GRID_W)).reshape(-1).astype(F32)
    col = jnp.broadcast_to(jnp.arange(GRID_W)[None, :], (rows, GRID_W)).reshape(-1).astype(F32)
    inv_freq = ROPE_BASE ** (-jnp.arange(0, half, 2, dtype=F32) / half)
    ang_r = row[:, None] * inv_freq
    ang_c = col[:, None] * inv_freq
    ang = jnp.concatenate([ang_r, ang_r, ang_c, ang_c], axis=-1)
    cos, sin = jnp.cos(ang), jnp.sin(ang)
    first = (jnp.arange(HEAD_DIM) % half) < half // 2
    ones = jnp.ones((pad_rows, HEAD_DIM), F32)
    zeros = jnp.zeros((pad_rows, HEAD_DIM), F32)
    return (jnp.concatenate([cos, ones]), jnp.concatenate([jnp.where(first, -sin, 0.0), zeros]),
            jnp.concatenate([jnp.where(first, 0.0, sin), zeros]))


def kernel(x, c, ctx, c_ctx, w_mod, b_mod, g_attn_norm, w_in, attn_sink, ssm_a_re, ssm_a_im, ssm_log_dt,
           ssm_b_re, ssm_b_im, ssm_c_re, ssm_c_im, ssm_d, w_glu, g_out_attn, g_out_ssm, w_out, g_ffn_norm,
           w_ff_gate, w_ff_up, w_ff_down, w_router, w_exp_gate, w_exp_up, w_exp_down, g_final):
    batch, seq, d = x.shape
    n_ctx = ctx.shape[1]
    depth = w_in.shape[0]
    n_q = attn_sink.shape[1]
    n_kv = n_q // Q_PER_KV
    attn_w = n_q * HEAD_DIM
    kv_w = n_kv * HEAD_DIM
    rows_x = batch * seq
    rows_all = rows_x + batch * n_ctx
    big = rows_x >= 8192
    tm_big = 1024 if big else 256
    tm_mid = 512 if big else 256
    tn = 512 if big else 256
    geo = dict(seq=seq, batch=batch)

    xs = jnp.concatenate([x.reshape(rows_x, d), ctx.reshape(batch * n_ctx, d)], axis=0)
    cond = jnp.concatenate([c, c_ctx[None, :]], axis=0)
    rope_tabs = _rope_tables(seq, tm_big)
    rope_geo = dict(rows=rows_all, rows_x=rows_x, seq=seq, tm=tm_big, tn=tn)
    ssm_tables = jax.vmap(_ssm_tables)(ssm_a_re, ssm_a_im, ssm_log_dt, ssm_b_re, ssm_b_im, ssm_c_re, ssm_c_im)
    normed = False

    for i in range(depth):
        last = i == depth - 1
        rows_out = rows_x if last else rows_all
        mod = ada_modulation(cond, w_mod, b_mod, i)
        h = norm_modulate(xs, g_attn_norm[i], mod, 0, 1, rows=rows_all, **geo)
        w_in_b = w_in[i].astype(BF16)
        q_rot, q_plain = matmul_rope(h, w_in_b[:, :attn_w], rope_tabs, n_rope_tiles=attn_w // tn,
                                     scale=HEAD_DIM ** -0.5, both=True, **rope_geo)
        kv = matmul_rope(h, w_in_b[:, attn_w:attn_w + 2 * kv_w], rope_tabs, n_rope_tiles=kv_w // tn,
                         scale=1.0, both=False, **rope_geo)
        u = matmul(h, w_in_b[:, attn_w + 2 * kv_w:], rows=rows_all, tm=tm_big, tn=tn, out_dtype=F32)
        o_attn = attention(q_rot, q_plain, kv, attn_sink[i], g_out_attn[i], batch=batch, seq=seq, n_ctx=n_ctx,
                           n_kv=n_kv, ctx_queries=not last)
        y = s5_mix(u, ssm_tables, ssm_d[i], i, n_state=ssm_a_re.shape[-1], batch=batch, seq=seq, n_ctx=n_ctx)
        o_ssm = matmul_glu_norm(y, w_glu[i].astype(BF16), g_out_ssm[i], rows=rows_out, tm=tm_big)
        w_out_b = w_out[i].astype(BF16)
        xs = matmul_gated_residual([(o_attn, w_out_b[:attn_w]), (o_ssm, w_out_b[attn_w:])], xs, mod, 2,
                                   rows=rows_out, tm=tm_big, tn=tn, weight_outer=False, **geo)
        j = i // 2
        if i % 2 == 0:
            h2 = norm_modulate(xs, g_ffn_norm[i], mod, 3, 4, rows=rows_out, **geo)
            hid = matmul_swiglu(h2, w_ff_gate[j].astype(BF16), w_ff_up[j].astype(BF16),
                                rows=rows_out, tm=tm_big, tn=tn)
            xs = matmul_gated_residual([(hid, w_ff_down[j].astype(BF16))], xs, mod, 5, rows=rows_out, tm=tm_mid,
                                       tn=tn, weight_outer=True, **geo)
        else:
            n_e = w_router.shape[2]
            wr = jnp.zeros((d, LANES), F32).at[:, :n_e].set(w_router[j])
            h2, logits = norm_modulate(xs, g_ffn_norm[i], mod, 3, 4, rows=rows_out, w_router=wr, **geo)
            xs = moe_ffn_residual(xs, h2, logits, mod, 5, w_exp_gate, w_exp_up, w_exp_down, j,
                                  rows=rows_out, seq=seq, batch=batch, final_g=g_final if last else None)
            normed = last
    if not normed:
        xs = final_norm(xs, g_final, rows=rows_x)
    return xs.reshape(batch, seq, d)
```

```python
import functools
import math

import jax
import jax.numpy as jnp
from jax import lax
from jax.experimental import pallas as pl
from jax.experimental.pallas import tpu as pltpu

F32 = jnp.float32
BF16 = jnp.bfloat16

HEAD_DIM = 128
BLOCK = 128
Q_PER_KV = 3
GRID_W = 64
ROPE_BASE = 10000.0
SSM_GROUP = 16
N_MOD = 6
TOP_K = 2
EPS = 1e-6
NEG_INF = -1e30
LOG2E = math.log2(math.e)

LANES = 128
SUBLANES = 8
VMEM_BYTES_V7X = 64 * 1024 * 1024
VMEM_CAP = VMEM_BYTES_V7X - 8 * 1024 * 1024

SSM_CHUNK = 16


def _params(sem, est_bytes):
    limit = int(min(VMEM_CAP, max(32 * 1024 * 1024, est_bytes * 5 // 4)))
    return pltpu.CompilerParams(dimension_semantics=sem, vmem_limit_bytes=limit)


def _nbytes(shape, dtype):
    return math.prod(shape) * jnp.dtype(dtype).itemsize


def _mod_row(i, tm, seq, batch):
    return jnp.minimum((i * tm) // seq, batch)


def _mod_kernel(cb_ref, w_ref, b_ref, o_ref, *, n_rows, tn):
    k_dim = w_ref.shape[0]
    n_chunks = tn // LANES

    def body(kb, accs):
        k0 = pl.multiple_of(kb * SUBLANES, SUBLANES)
        wv = w_ref[pl.ds(k0, SUBLANES), :]
        new = []
        for r in range(n_rows):
            cv = cb_ref[r, pl.ds(k0, SUBLANES), :]
            sv = cv * jax.nn.sigmoid(cv)
            for c in range(n_chunks):
                new.append(accs[r * n_chunks + c] + wv[:, c * LANES:(c + 1) * LANES] * sv)
        return tuple(new)

    init = tuple(jnp.zeros((SUBLANES, LANES), F32) for _ in range(n_rows * n_chunks))
    accs = lax.fori_loop(0, k_dim // SUBLANES, body, init, unroll=4)
    o_ref[...] = jnp.zeros_like(o_ref)
    for r in range(n_rows):
        row = jnp.concatenate(
            [jnp.sum(accs[r * n_chunks + c], axis=0, keepdims=True) for c in range(n_chunks)], axis=1)
        o_ref[r:r + 1, :] = row + b_ref[...]


def ada_modulation(cond_rows, w_mod, b_mod, layer):
    n_rows, d = cond_rows.shape
    depth, _, n = w_mod.shape
    tn = 1024
    cb =jnp.broadcast_to(cond_rows[:, :, None], (n_rows, d, LANES))
    est = 2 * (_nbytes((d, tn), F32) + _nbytes((n_rows, d, LANES), F32))
    out = pl.pallas_call(
        functools.partial(_mod_kernel, n_rows=n_rows, tn=tn),
        grid=(n // tn,),
        in_specs=[pl.BlockSpec((n_rows, d, LANES), lambda j: (0, 0, 0)),
                  pl.BlockSpec((None, d, tn), lambda j: (layer, 0, j)),
                  pl.BlockSpec((None, 1, tn), lambda j: (layer, 0, j))],
        out_specs=pl.BlockSpec((SUBLANES, tn), lambda j: (0, j)),
        out_shape=jax.ShapeDtypeStruct((SUBLANES, n), F32),
        compiler_params=_params(("parallel",), est),
        name="ada_modulation",
    )(cb, w_mod, b_mod.reshape(depth, 1, n))
    return out.reshape(SUBLANES, 1, n)


def _split_bf16(v):
    hi = v.astype(BF16)
    return hi, (v - hi.astype(F32)).astype(BF16)


def _normmod_kernel(x_ref, g_ref, shift_ref, scale_ref, *rest, with_router):
    xf = x_ref[...]
    y = xf * lax.rsqrt(jnp.mean(xf * xf, axis=-1, keepdims=True) + EPS)
    h = (y * g_ref[...]) * (1.0 + scale_ref[...]) + shift_ref[...]
    if with_router:
        wr_ref, o_ref, lg_ref = rest
        hs = _split_bf16(h)
        ws = _split_bf16(wr_ref[...])
        acc = None
        for a, b in ((0, 0), (0, 1), (1, 0)):
            t = jnp.dot(hs[a], ws[b], preferred_element_type=F32)
            acc = t if acc is None else acc + t
        lg_ref[...] = acc
    else:
        (o_ref,) = rest
    o_ref[...] = h.astype(BF16)


def norm_modulate(x, g, mod, shift_idx, scale_idx, *, rows, seq, batch, w_router=None):
    d = x.shape[1]
    tm = 256
    with_router = w_router is not None
    in_specs = [pl.BlockSpec((tm, d), lambda i: (i, 0)),
                pl.BlockSpec((1, d), lambda i: (0, 0)),
                pl.BlockSpec((None, 1, d), lambda i: (_mod_row(i, tm, seq, batch), 0, shift_idx)),
                pl.BlockSpec((None, 1, d), lambda i: (_mod_row(i, tm, seq, batch), 0, scale_idx))]
    args = [x, g.reshape(1, d), mod, mod]
    out_specs = [pl.BlockSpec((tm, d), lambda i: (i, 0))]
    out_shape = [jax.ShapeDtypeStruct((rows, d), BF16)]
    if with_router:
        in_specs.append(pl.BlockSpec((d, LANES), lambda i: (0, 0)))
        args.append(w_router)
        out_specs.append(pl.BlockSpec((tm, LANES), lambda i: (i, 0)))
        out_shape.append(jax.ShapeDtypeStruct((rows, LANES), F32))
    est = 2 * (_nbytes((tm, d), F32) + _nbytes((tm, d), BF16)) + 8 * _nbytes((tm, d), F32)
    outs = pl.pallas_call(
        functools.partial(_normmod_kernel, with_router=with_router),
        grid=(pl.cdiv(rows, tm),),
        in_specs=in_specs, out_specs=out_specs, out_shape=out_shape,
        compiler_params=_params(("parallel",), est),
        name="norm_modulate_router" if with_router else "norm_modulate",
    )(*args)
    return outs if with_router else outs[0]


def _mm_kernel(x_ref, w_ref, o_ref):
    o_ref[...] = jnp.dot(x_ref[...], w_ref[...], preferred_element_type=F32).astype(o_ref.dtype)


def _mm_swiglu_kernel(x_ref, wg_ref, wu_ref, o_ref):
    x = x_ref[...]
    a = jnp.dot(x, wg_ref[...], preferred_element_type=F32)
    b = jnp.dot(x, wu_ref[...], preferred_element_type=F32)
    o_ref[...] = (a * jax.nn.sigmoid(a) * b).astype(o_ref.dtype)


def _rms(x, g):
    return x * lax.rsqrt(jnp.mean(x * x, axis=-1, keepdims=True) + EPS) * g


def _mm_resid_kernel(*refs, n_parts):
    x_refs, w_refs = refs[:n_parts], refs[n_parts:2 * n_parts]
    r_ref, gate_ref, o_ref = refs[2 * n_parts:]
    acc = None
    for x_ref, w_ref in zip(x_refs, w_refs):
        t = jnp.dot(x_ref[...], w_ref[...], preferred_element_type=F32)
        acc = t if acc is None else acc + t
    o_ref[...] = r_ref[...] + gate_ref[...] * acc


def _mm_glu_norm_kernel(y_ref, w_ref, g_ref, o_ref):
    y = y_ref[...]
    z = jnp.dot(y.astype(BF16), w_ref[...], preferred_element_type=F32)
    o_ref[...] = _rms(y * jax.nn.sigmoid(z), g_ref[...]).astype(BF16)


def matmul(x, w, *, rows, tm, tn, out_dtype):
    k, n = w.shape
    est = 2 * (_nbytes((tm, k), x.dtype) + _nbytes((k, tn), w.dtype) + _nbytes((tm, tn), out_dtype))
    return pl.pallas_call(
        _mm_kernel,
        grid=(pl.cdiv(rows, tm), pl.cdiv(n, tn)),
        in_specs=[pl.BlockSpec((tm, k), lambda i, j: (i, 0)),
                  pl.BlockSpec((k, tn), lambda i, j: (0, j))],
        out_specs=pl.BlockSpec((tm, tn), lambda i, j: (i, j)),
        out_shape=jax.ShapeDtypeStruct((rows, n), out_dtype),
        compiler_params=_params(("parallel", "parallel"), est),
        name="matmul",
    )(x, w)


def _rope(x, cos, sin_lo, sin_hi):
    return (x * cos + pltpu.roll(x, HEAD_DIM - HEAD_DIM // 4, 1) * sin_lo
            + pltpu.roll(x, HEAD_DIM // 4, 1) * sin_hi)


def _mm_rope_kernel(x_ref, w_ref, cos_ref, slo_ref, shi_ref, *o_refs, n_rope_tiles, scale, both, sub):
    j = pl.program_id(1)
    tm, tn = o_refs[0].shape
    for r0 in range(0, tm, sub):
        acc = jnp.dot(x_ref[r0:r0 + sub, :], w_ref[...], preferred_element_type=F32)
        if scale != 1.0:
            acc = acc * scale
        cos, slo, shi = (r[r0:r0 + sub, :] for r in (cos_ref, slo_ref, shi_ref))
        rot = jnp.concatenate([_rope(acc[:, h * HEAD_DIM:(h + 1) * HEAD_DIM], cos, slo, shi)
                               for h in range(tn // HEAD_DIM)], axis=1)
        if both:
            o_refs[0][r0:r0 + sub, :] = rot.astype(BF16)
            o_refs[1][r0:r0 + sub, :] = acc.astype(BF16)
        else:
            o_refs[0][r0:r0 + sub, :] = jnp.where(j < n_rope_tiles, rot, acc).astype(BF16)


def matmul_rope(x, w, tabs, *, rows, rows_x, seq, tm, tn, n_rope_tiles, scale, both):
    k, n = w.shape
    per_seq = seq // tm

    def tab_map(i, j):
        return (jnp.where(i * tm < rows_x, i % per_seq, per_seq), 0)

    n_out = 2 if both else 1
    est = 2 * (_nbytes((tm, k), BF16) + _nbytes((k, tn), BF16) + n_out * _nbytes((tm, tn), BF16)
               + 3 * _nbytes((tm, HEAD_DIM), F32)) + 4 * _nbytes((tm, tn), F32)
    out = pl.pallas_call(
        functools.partial(_mm_rope_kernel, n_rope_tiles=n_rope_tiles, scale=scale, both=both, sub=min(tm, 256)),
        grid=(pl.cdiv(rows, tm), n // tn),
        in_specs=[pl.BlockSpec((tm, k), lambda i, j: (i, 0)),
                  pl.BlockSpec((k, tn), lambda i, j: (0, j))] + [pl.BlockSpec((tm, HEAD_DIM), tab_map)] * 3,
        out_specs=[pl.BlockSpec((tm, tn), lambda i, j: (i, j))] * n_out,
        out_shape=[jax.ShapeDtypeStruct((rows, n), BF16)] * n_out,
        compiler_params=_params(("parallel", "parallel"), est),
        name="matmul_rope",
    )(x, w, *tabs)
    return out if both else out[0]


def matmul_swiglu(x, wg, wu, *, rows, tm, tn):
    k, n = wg.shape
    est = 2 * (_nbytes((tm, k), BF16) + 2 * _nbytes((k, tn), BF16) + _nbytes((tm, tn), BF16)) \
        + 3 * _nbytes((tm, tn), F32)
    return pl.pallas_call(
        _mm_swiglu_kernel,
        grid=(pl.cdiv(rows, tm), pl.cdiv(n, tn)),
        in_specs=[pl.BlockSpec((tm, k), lambda i, j: (i, 0)),
                  pl.BlockSpec((k, tn), lambda i, j: (0, j)),
                  pl.BlockSpec((k, tn), lambda i, j: (0, j))],
        out_specs=pl.BlockSpec((tm, tn), lambda i, j: (i, j)),
        out_shape=jax.ShapeDtypeStruct((rows, n), BF16),
        compiler_params=_params(("parallel", "parallel"), est),
        name="matmul_swiglu",
    )(x, wg, wu)


def matmul_gated_residual(xw, resid, mod, gate_idx, *, rows, tm, tn, seq, batch, weight_outer):
    xs_, ws_ = [p[0] for p in xw], [p[1] for p in xw]
    n = ws_[0].shape[1]
    k = sum(w.shape[0] for w in ws_)
    nj = n // tn
    if weight_outer:
        grid = (nj, pl.cdiv(rows, tm))
        ij = lambda a, b: (b, a)
    else:
        grid = (pl.cdiv(rows, tm), nj)
        ij = lambda a, b: (a, b)

    def gate_map(a, b):
        i, j = ij(a, b)
        return (_mod_row(i, tm, seq, batch), 0, gate_idx * nj + j)

    est = 2 * (_nbytes((tm, k), BF16) + _nbytes((k, tn), BF16) + 2 * _nbytes((tm, tn), F32))
    return pl.pallas_call(
        functools.partial(_mm_resid_kernel, n_parts=len(xw)),
        grid=grid,
        in_specs=[pl.BlockSpec((tm, x.shape[1]), lambda a, b: (ij(a, b)[0], 0)) for x in xs_]
        + [pl.BlockSpec((w.shape[0], tn), lambda a, b: (0, ij(a, b)[1])) for w in ws_]
        + [pl.BlockSpec((tm, tn), lambda a, b: ij(a, b)),
           pl.BlockSpec((None, 1, tn), gate_map)],
        out_specs=pl.BlockSpec((tm, tn), lambda a, b: ij(a, b)),
        out_shape=jax.ShapeDtypeStruct((rows, n), F32),
        compiler_params=_params(("parallel", "parallel"), est),
        name="matmul_gated_residual",
    )(*xs_, *ws_, resid, mod)


def matmul_glu_norm(y, w, g, *, rows, tm):
    k, n = w.shape
    est = 2 * (_nbytes((tm, k), F32) + _nbytes((k, n), BF16) + _nbytes((tm, n), BF16)) + 4 * _nbytes((tm, n), F32)
    return pl.pallas_call(
        _mm_glu_norm_kernel,
        grid=(pl.cdiv(rows, tm),),
        in_specs=[pl.BlockSpec((tm, k), lambda i: (i, 0)),
                  pl.BlockSpec((k, n), lambda i: (0, 0)),
                  pl.BlockSpec((1, n), lambda i: (0, 0))],
        out_specs=pl.BlockSpec((tm, n), lambda i: (i, 0)),
        out_shape=jax.ShapeDtypeStruct((rows, n), BF16),
        compiler_params=_params(("parallel",), est),
        name="matmul_glu_norm",
    )(y, w, g.reshape(1, n))


def _nt_dot(a, b):
    return lax.dot_general(a, b, (((1,), (1,)), ((), ())), preferred_element_type=F32)


def _attn_kernel(sink_ref, qr_ref, qp_ref, kp_ref, k0_ref, kn_ref, vp_ref, v0_ref, vn_ref, kc_ref, vc_ref,
                 bias_ref, g_ref, o_ref, o_scr, *, n_blocks, ctx_queries, n_kv):
    n = pl.program_id(1)
    qw = Q_PER_KV * HEAD_DIM

    def stack_heads(ref, hh):
        return jnp.concatenate([ref[:, hh * qw + g * HEAD_DIM:hh * qw + (g + 1) * HEAD_DIM]
                                for g in range(Q_PER_KV)], axis=0)

    def head(ref, hh):
        return ref[:, hh * HEAD_DIM:(hh + 1) * HEAD_DIM]

    def run_head(window, hh):
        sink = jnp.concatenate(
            [jnp.full((BLOCK, 1), sink_ref[hh * Q_PER_KV + g], F32) for g in range(Q_PER_KV)], axis=0)
        s_c = _nt_dot(stack_heads(qp_ref, hh), head(kc_ref, hh))
        m = jnp.maximum(jnp.max(s_c, axis=1, keepdims=True), sink)
        if window:
            k_win = jnp.concatenate([head(kp_ref, hh), head(k0_ref, hh), head(kn_ref, hh)], axis=0)
            s_w = _nt_dot(stack_heads(qr_ref, hh), k_win) + bias_ref[...]
            m = jnp.maximum(m, jnp.max(s_w, axis=1, keepdims=True))
        p_c = jnp.exp2(s_c - m)
        den = jnp.sum(p_c, axis=1, keepdims=True) + jnp.exp2(sink - m)
        o = jnp.dot(p_c.astype(BF16), head(vc_ref, hh), preferred_element_type=F32)
        if window:
            p_w = jnp.exp2(s_w - m)
            den = den + jnp.sum(p_w, axis=1, keepdims=True)
            v_win = jnp.concatenate([head(vp_ref, hh), head(v0_ref, hh), head(vn_ref, hh)], axis=0)
            o = o + jnp.dot(p_w.astype(BF16), v_win, preferred_element_type=F32)
        o = o / den
        for g in range(Q_PER_KV):
            o_scr[:, hh * qw + g * HEAD_DIM:hh * qw + (g + 1) * HEAD_DIM] = o[g * BLOCK:(g + 1) * BLOCK]

    def run(window):
        for hh in range(n_kv):
            run_head(window, hh)
        o_ref[...] = _rms(o_scr[...], g_ref[...]).astype(BF16)

    if ctx_queries:
        pl.when(n < n_blocks)(lambda: run(True))
        pl.when(n >= n_blocks)(lambda: run(False))
    else:
        run(True)


def _window_bias(n_blocks):
    rows = Q_PER_KV * BLOCK
    row = (jnp.arange(rows) % BLOCK)[:, None]
    col = jnp.arange(3 * BLOCK)[None, :]
    band = (col >= row) & (col <= row + 2 * BLOCK)
    variants = []
    for code in range(4):
        ok = band
        if code & 1:
            ok = ok & (col >= BLOCK)
        if code & 2:
            ok = ok & (col < 2 * BLOCK)
        variants.append(jnp.where(ok, 0.0, NEG_INF).astype(F32))
    return jnp.stack(variants)


def attention(q_rot, q_plain, kv, sink, g_out, *, batch, seq, n_ctx, n_kv, ctx_queries):
    nb = seq // BLOCK
    qb = n_ctx // BLOCK if ctx_queries else 0
    qw = n_kv * Q_PER_KV * HEAD_DIM
    kw = n_kv * HEAD_DIM
    rows_out = batch * seq + (batch * n_ctx if ctx_queries else 0)
    ctx_blk0 = batch * seq // n_ctx
    q_blk0 = batch * seq // BLOCK

    def q_map(b, n, s):
        return (jnp.where(n < nb, b * nb + n, q_blk0 + b * qb + (n - nb)), 0)

    def kv_spec(col, shift):
        return pl.BlockSpec((BLOCK, kw), lambda b, n, s: (b * nb + jnp.clip(n + shift, 0, nb - 1), col))

    def bias_map(b, n, s):
        return ((n == 0).astype(jnp.int32) + 2 * (n == nb - 1).astype(jnp.int32), 0, 0)

    in_specs = [pl.BlockSpec((BLOCK, qw), q_map), pl.BlockSpec((BLOCK, qw), q_map),
                kv_spec(0, -1), kv_spec(0, 0), kv_spec(0, 1),
                kv_spec(1, -1), kv_spec(1, 0), kv_spec(1, 1),
                pl.BlockSpec((n_ctx, kw), lambda b, n, s: (ctx_blk0 + b, 0)),
                pl.BlockSpec((n_ctx, kw), lambda b, n, s: (ctx_blk0 + b, 1)),
                pl.BlockSpec((None, Q_PER_KV * BLOCK, 3 * BLOCK), bias_map),
                pl.BlockSpec((1, qw), lambda b, n, s: (0, 0))]
    grid_spec = pltpu.PrefetchScalarGridSpec(
        num_scalar_prefetch=1, grid=(batch, nb + qb), in_specs=in_specs,
        out_specs=pl.BlockSpec((BLOCK, qw), q_map),
        scratch_shapes=[pltpu.VMEM((BLOCK, qw), F32)])
    return pl.pallas_call(
        functools.partial(_attn_kernel, n_blocks=nb, ctx_queries=ctx_queries, n_kv=n_kv),
        grid_spec=grid_spec,
        out_shape=jax.ShapeDtypeStruct((rows_out, qw), BF16),
        compiler_params=_params(("parallel", "parallel"), 16 * 1024 * 1024),
        name="attention",
    )(sink, q_rot, q_plain, *([kv] * 8), _window_bias(nb), g_out.reshape(1, qw))


def _ssm_tables(a_re, a_im, log_dt, b_re, b_im, c_re, c_im):
    hp = lax.Precision.HIGHEST
    t_len = SSM_CHUNK
    g_per = LANES // SSM_GROUP
    n_groups, n_state = a_re.shape[1:]
    n_lb = n_groups // g_per
    dt = jnp.exp(log_dt.astype(F32))[..., None]
    lam_re, lam_im = a_re.astype(F32), a_im.astype(F32)
    mag = jnp.exp(lam_re * dt)
    abar = lax.complex(mag * jnp.cos(lam_im * dt), mag * jnp.sin(lam_im * dt))
    lam = lax.complex(lam_re, lam_im)
    bbar = ((abar - 1.0) / lam)[..., None] * lax.complex(b_re.astype(F32), b_im.astype(F32))
    cmat = lax.complex(c_re.astype(F32), c_im.astype(F32))

    def powers(d, exps):
        e = jnp.asarray(exps, F32)[:, None, None]
        m = jnp.exp(lam_re[d] * dt[d] * e)
        ph = lam_im[d] * dt[d] * e
        return lax.complex(m * jnp.cos(ph), m * jnp.sin(ph))

    ar = list(range(t_len + 1))
    pw_f, pw_b = powers(0, ar), powers(1, ar)

    kf = jnp.real(jnp.einsum('gcp,tgp,gpi->tgci', cmat[0], pw_f[:t_len], bbar[0], precision=hp))
    kb = jnp.real(jnp.einsum('gcp,tgp,gpi->tgci', cmat[1], pw_b[:t_len], bbar[1], precision=hp))
    lag = jnp.arange(2 * t_len - 1) - (t_len - 1)
    k_lag = jnp.where((lag >= 0)[:, None, None, None], kf[jnp.clip(lag, 0)], 0.0) \
        + jnp.where((lag <= 0)[:, None, None, None], kb[jnp.clip(-lag, 0)], 0.0)
    k_lag = jnp.swapaxes(k_lag, -1, -2).reshape(2 * t_len - 1, n_lb, g_per, SSM_GROUP, SSM_GROUP)
    eye = jnp.eye(g_per, dtype=F32)
    k_lag = jnp.einsum('mngab,gh->nmgahb', k_lag, eye).reshape(n_lb, 2 * t_len - 1, LANES, LANES)

    rep = LANES // n_state

    def e_coef(d, pw_sel):
        pw_rep = jnp.concatenate([pw_sel] * rep, axis=-1)
        bb_rep = jnp.concatenate([bbar[d]] * rep, axis=-2)
        return jnp.swapaxes(pw_rep[..., None] * bb_rep[None], -1, -2)

    ef, eb = e_coef(0, powers(0, [t_len - 1 - j for j in range(t_len)])), e_coef(1, pw_b[:t_len])
    quarters = jnp.stack([jnp.real(ef), jnp.imag(ef), jnp.real(eb), jnp.imag(eb)], axis=-2)
    e_c = quarters.reshape(t_len, n_lb, LANES, 4 * LANES).transpose(1, 0, 2, 3)

    def f_coef(d, pw_sel):
        return cmat[d][None] * pw_sel[:, :, None, :]

    ff, fb = f_coef(0, pw_f[1:]), f_coef(1, powers(1, [t_len - l for l in range(t_len)]))
    f_c = jnp.stack([jnp.real(ff), -jnp.imag(ff), jnp.real(fb), -jnp.imag(fb)])
    f_c = f_c.reshape(4, t_len, n_lb, g_per, SSM_GROUP, n_state).transpose(2, 0, 5, 1, 3, 4)
    f_c = f_c.reshape(n_lb, 4 * n_state, t_len * LANES)

    def lay(v):
        return v.reshape(n_lb, g_per * n_state)

    a_chunk = jnp.concatenate([lay(jnp.real(pw_f[t_len])), lay(jnp.imag(pw_f[t_len])),
                               lay(jnp.real(pw_b[t_len])), lay(jnp.imag(pw_b[t_len]))], axis=-1)
    return k_lag.astype(BF16), e_c, f_c, a_chunk.reshape(1, -1)


def _chunk_inputs(x_ref, tmc):
    return [x_ref[pl.ds(j, tmc, stride=SSM_CHUNK), :] for j in range(SSM_CHUNK)]


def _ssm_local_kernel(x_ref, ec_ref, s_ref, e_scr, *, n_state):
    tmc = s_ref.shape[0]
    qw = e_scr.shape[1] // 4

    @pl.when(pl.program_id(1) == 0)
    def _():
        row_g = lax.broadcasted_iota(jnp.int32, (LANES, qw), 0) // SSM_GROUP
        col_g = lax.broadcasted_iota(jnp.int32, (LANES, qw), 1) // n_state
        same = row_g == col_g
        for j in range(SSM_CHUNK):
            for q in range(4):
                v = ec_ref[j, :, q * LANES:(q + 1) * LANES]
                tiled = jnp.concatenate([v] * (qw // LANES), axis=1)
                e_scr[j * LANES:(j + 1) * LANES, q * qw:(q + 1) * qw] = jnp.where(same, tiled, 0.0).astype(BF16)

    xs = jnp.concatenate([x.astype(BF16) for x in _chunk_inputs(x_ref, tmc)], axis=1)
    s_ref[...] = jnp.dot(xs, e_scr[...], preferred_element_type=F32)


def _ssm_scan_kernel(s_ref, a_ref, h_ref, *, batch, nc_x, nc_c):
    sw = s_ref.shape[1] // 4
    a = a_ref[...]
    afr, afi, abr, abi = (a[:, k * sw:(k + 1) * sw] for k in range(4))
    ctx0 = batch * nc_x

    def step(row_f, row_b, st):
        new = []
        for (row, ar, ai, off, (hr, hi)) in ((row_f, afr, afi, 0, st[0]), (row_b, abr, abi, 2 * sw, st[1])):
            h_ref[pl.ds(row, 1), off:off + sw] = hr
            h_ref[pl.ds(row, 1), off + sw:off + 2 * sw] = hi
            sr = s_ref[pl.ds(row, 1), off:off + sw]
            si = s_ref[pl.ds(row, 1), off + sw:off + 2 * sw]
            new.append((ar * hr - ai * hi + sr, ar * hi + ai * hr + si))
        return tuple(new)

    zero = jnp.zeros((1, sw), F32)
    for b in range(batch):
        st = ((zero, zero), (zero, zero))
        c_base = ctx0 + b * nc_c
        st = lax.fori_loop(0, nc_c, lambda t, s: step(c_base + t, c_base + nc_c - 1 - t, s), st)
        x_base = b * nc_x
        lax.fori_loop(0, nc_x, lambda t, s: step(x_base + t, x_base + nc_x - 1 - t, s), st)


def _ssm_out_kernel(x_ref, h_ref, kl_ref, fc_ref, d_ref, o_ref, w_scr, *, n_state):
    t = SSM_CHUNK
    tmc = h_ref.shape[0]
    sdim = h_ref.shape[1]
    qw = sdim // 4

    @pl.when(pl.program_id(1) == 0)
    def _():
        for j in range(t):
            for l in range(t):
                w_scr[j * LANES:(j + 1) * LANES, l * LANES:(l + 1) * LANES] = kl_ref[l - j + t - 1]
        lane_g = (lax.broadcasted_iota(jnp.int32, (n_state, t * LANES), 1) % LANES) // SSM_GROUP
        for q in range(4):
            v = fc_ref[q * n_state:(q + 1) * n_state, :]
            for g in range(qw // n_state):
                r0 = t * LANES + q * qw + g * n_state
                w_scr[r0:r0 + n_state, :] = jnp.where(lane_g == g, v, 0.0).astype(BF16)

    xs = _chunk_inputs(x_ref, tmc)
    lhs = jnp.concatenate([x.astype(BF16) for x in xs] + [h_ref[...].astype(BF16)], axis=1)
    y = jnp.dot(lhs, w_scr[...], preferred_element_type=F32)
    d = d_ref[...]
    for l in range(t):
        o_ref[pl.ds(l, tmc, stride=t), :] = jax.nn.gelu(y[:, l * LANES:(l + 1) * LANES] + d * xs[l])


def s5_mix(u, tables, d_skip, layer, *, n_state, batch, seq, n_ctx):
    k_lag, e_c, f_c, a_chunk = tables
    rows, ssm_width = u.shape
    t = SSM_CHUNK
    n_lb = ssm_width // LANES
    rc = rows // t
    sdim = 4 * (LANES // SSM_GROUP) * n_state
    tmc = max(m for m in range(SUBLANES, min(rc, 264) + 1, SUBLANES) if rc % m == 0)
    x_spec = pl.BlockSpec((tmc * t, LANES), lambda g, r: (r, g))
    x_bytes = _nbytes((tmc * t, LANES), F32)

    s_loc = pl.pallas_call(
        functools.partial(_ssm_local_kernel, n_state=n_state),
        grid=(n_lb, rc // tmc),
        in_specs=[x_spec, pl.BlockSpec((None, None, t, LANES, 4 * LANES), lambda g, r: (layer, g, 0, 0, 0))],
        out_specs=pl.BlockSpec((tmc, sdim), lambda g, r: (r, g)),
        out_shape=jax.ShapeDtypeStruct((rc, n_lb * sdim), F32),
        scratch_shapes=[pltpu.VMEM((t * LANES, sdim), BF16)],
        compiler_params=_params(("arbitrary", "arbitrary"),
                                2 * x_bytes + 2 * _nbytes((t, LANES, 4 * LANES), F32)
                                + _nbytes((t * LANES, sdim), BF16) + 4 * _nbytes((tmc, sdim), F32)),
        name="s5_local_states",
    )(u, e_c)

    h_in = pl.pallas_call(
        functools.partial(_ssm_scan_kernel, batch=batch, nc_x=seq // t, nc_c=n_ctx // t),
        grid=(n_lb,),
        in_specs=[pl.BlockSpec((rc, sdim), lambda g: (0, g)),
                  pl.BlockSpec((None, 1, sdim), lambda g: (layer, 0, g))],
        out_specs=pl.BlockSpec((rc, sdim), lambda g: (0, g)),
        out_shape=jax.ShapeDtypeStruct((rc, n_lb * sdim), F32),
        compiler_params=_params(("parallel",), 4 * _nbytes((rc, sdim), F32)),
        name="s5_chunk_scan",
    )(s_loc, a_chunk)

    return pl.pallas_call(
        functools.partial(_ssm_out_kernel, n_state=n_state),
        grid=(n_lb, rc // tmc),
        in_specs=[x_spec,
                  pl.BlockSpec((tmc, sdim), lambda g, r: (r, g)),
                  pl.BlockSpec((None, None, 2 * t - 1, LANES, LANES), lambda g, r: (layer, g, 0, 0, 0)),
                  pl.BlockSpec((None, None, 4 * n_state, t * LANES), lambda g, r: (layer, g, 0, 0)),
                  pl.BlockSpec((1, LANES), lambda g, r: (0, g))],
        out_specs=x_spec,
        out_shape=jax.ShapeDtypeStruct((rows, ssm_width), F32),
        scratch_shapes=[pltpu.VMEM((t * LANES + sdim, t * LANES), BF16)],
        compiler_params=_params(("arbitrary", "arbitrary"),
                                4 * x_bytes + 2 * _nbytes((tmc, sdim), F32)
                                + 2 * _nbytes((2 * t - 1, LANES, LANES), BF16)
                                + 2 * _nbytes((4 * n_state, t * LANES), F32)
                                + _nbytes((t * LANES + sdim, t * LANES), BF16) + 4 * _nbytes((tmc, sdim), F32)),
        name="s5_readout",
    )(u, h_in, k_lag, f_c, d_skip.reshape(1, ssm_width))


def _final_norm_kernel(x_ref, g_ref, o_ref):
    o_ref[...] = _rms(x_ref[...], g_ref[...])


def final_norm(x, g, *, rows):
    d = x.shape[1]
    tm = 256
    return pl.pallas_call(
        _final_norm_kernel,
        grid=(pl.cdiv(rows, tm),),
        in_specs=[pl.BlockSpec((tm, d), lambda i: (i, 0)), pl.BlockSpec((1, d), lambda i: (0, 0))],
        out_specs=pl.BlockSpec((tm, d), lambda i: (i, 0)),
        out_shape=jax.ShapeDtypeStruct((rows, d), F32),
        compiler_params=_params(("parallel",), 6 * _nbytes((tm, d), F32)),
        name="final_norm",
    )(x, g.reshape(1, d))


def _new_expert(te_ref, t):
    return jnp.logical_or(t == 0, te_ref[t] != te_ref[jnp.maximum(t - 1, 0)])


def _moe_up_kernel(te_ref, nu_ref, x_ref, wg_ref, wu_ref, o_ref, wg_scr, wu_scr):
    t = pl.program_id(1)

    @pl.when(_new_expert(te_ref, t))
    def _():
        wg_scr[...] = wg_ref[...].astype(BF16)
        wu_scr[...] = wu_ref[...].astype(BF16)

    @pl.when(t < nu_ref[0])
    def _():
        x = x_ref[...]
        a = jnp.dot(x, wg_scr[...], preferred_element_type=F32)
        b = jnp.dot(x, wu_scr[...], preferred_element_type=F32)
        o_ref[...] = (a * jax.nn.sigmoid(a) * b).astype(o_ref.dtype)

    @pl.when(t >= nu_ref[0])
    def _():
        o_ref[...] = jnp.zeros_like(o_ref)


def _moe_down_kernel(te_ref, nu_ref, h_ref, wd_ref, rw_ref, o_ref, wd_scr):
    t = pl.program_id(1)

    @pl.when(_new_expert(te_ref, t))
    def _():
        wd_scr[...] = wd_ref[...].astype(BF16)

    @pl.when(t < nu_ref[0])
    def _():
        acc = jnp.dot(h_ref[...], wd_scr[...], preferred_element_type=F32)
        o_ref[...] = (acc * rw_ref[...]).astype(o_ref.dtype)

    @pl.when(t >= nu_ref[0])
    def _():
        o_ref[...] = jnp.zeros_like(o_ref)


def moe_experts(xg, row_w, tile_expert, n_used, w_g, w_u, w_d, e_base, *, tmr, tn):
    r_rows, d = xg.shape
    d_e = w_g.shape[2]
    n_tiles = r_rows // tmr
    est = 2 * (_nbytes((tmr, d), BF16) + 2 * _nbytes((d, tn), F32) + _nbytes((tmr, tn), BF16)) \
        + 2 * _nbytes((d, tn), BF16) + 3 * _nbytes((tmr, tn), F32)
    hidden = pl.pallas_call(
        _moe_up_kernel,
        grid_spec=pltpu.PrefetchScalarGridSpec(
            num_scalar_prefetch=2, grid=(d_e // tn, n_tiles),
            in_specs=[pl.BlockSpec((tmr, d), lambda j, t, te, nu: (t, 0)),
                      pl.BlockSpec((None, d, tn), lambda j, t, te, nu: (e_base + te[t], 0, j)),
                      pl.BlockSpec((None, d, tn), lambda j, t, te, nu: (e_base + te[t], 0, j))],
            out_specs=pl.BlockSpec((tmr, tn), lambda j, t, te, nu: (t, j)),
            scratch_shapes=[pltpu.VMEM((d, tn), BF16), pltpu.VMEM((d, tn), BF16)]),
        out_shape=jax.ShapeDtypeStruct((r_rows, d_e), BF16),
        compiler_params=_params(("arbitrary", "arbitrary"), est),
        name="moe_gate_up",
    )(tile_expert, n_used, xg, w_g, w_u)
    est = 2 * (_nbytes((tmr, d_e), BF16) + _nbytes((d_e, tn), F32) + _nbytes((tmr, tn), BF16)) \
        + _nbytes((d_e, tn), BF16) + 2 * _nbytes((tmr, tn), F32)
    return pl.pallas_call(
        _moe_down_kernel,
        grid_spec=pltpu.PrefetchScalarGridSpec(
            num_scalar_prefetch=2, grid=(d // tn, n_tiles),
            in_specs=[pl.BlockSpec((tmr, d_e), lambda j, t, te, nu: (t, 0)),
                      pl.BlockSpec((None, d_e, tn), lambda j, t, te, nu: (e_base + te[t], 0, j)),
                      pl.BlockSpec((tmr, 1), lambda j, t, te, nu: (t, 0))],
            out_specs=pl.BlockSpec((tmr, tn), lambda j, t, te, nu: (t, j)),
            scratch_shapes=[pltpu.VMEM((d_e, tn), BF16)]),
        out_shape=jax.ShapeDtypeStruct((r_rows, d), BF16),
        compiler_params=_params(("arbitrary", "arbitrary"), est),
        name="moe_down",
    )(tile_expert, n_used, hidden, w_d, row_w)


def _moe_combine_kernel(x_ref, y1_ref, y2_ref, gate_ref, *rest, final):
    out = x_ref[...] + gate_ref[...] * (y1_ref[...].astype(F32) + y2_ref[...].astype(F32))
    if final:
        g_ref, o_ref = rest
        o_ref[...] = _rms(out, g_ref[...])
    else:
        (o_ref,) = rest
        o_ref[...] = out


def moe_route(logits, n_experts, tmr):
    n_tok = logits.shape[0]
    lg = logits[:, :n_experts]
    top_v, top_i = lax.top_k(lg, TOP_K)
    top_w = jax.nn.softmax(top_v, axis=-1)
    flat_e = top_i.reshape(-1)
    onehot = (flat_e[:, None] == jnp.arange(n_experts)[None, :]).astype(jnp.int32)
    rank = jnp.take_along_axis(jnp.cumsum(onehot, axis=0) - onehot, flat_e[:, None], axis=1)[:, 0]
    counts = jnp.sum(onehot, axis=0)
    tiles_per = (counts + tmr - 1) // tmr
    tile_start = jnp.cumsum(tiles_per) - tiles_per
    dest = tile_start[flat_e] * tmr + rank
    n_tiles = (n_tok * TOP_K) // tmr + n_experts
    r_rows = n_tiles * tmr
    slot_tok = (jnp.arange(n_tok * TOP_K, dtype=jnp.int32) // TOP_K).astype(F32)
    pad_tok = (jnp.arange(r_rows, dtype=jnp.int32) % n_tok).astype(F32)
    placed = jnp.stack([pad_tok, jnp.zeros((r_rows,), F32)], axis=1).at[dest].set(
        jnp.stack([slot_tok, top_w.reshape(-1)], axis=1))
    src_tok = placed[:, 0].astype(jnp.int32)
    row_w = placed[:, 1]
    tile_ids = jnp.arange(n_tiles, dtype=jnp.int32)
    tile_expert = jnp.sum((tile_ids[:, None] >= jnp.cumsum(tiles_per)[None, :]).astype(jnp.int32), axis=1)
    n_used = jnp.sum(tiles_per).astype(jnp.int32)
    tile_expert = jnp.where(tile_ids < n_used, tile_expert, tile_expert[jnp.maximum(n_used - 1, 0)])
    tile_expert = jnp.minimum(tile_expert, n_experts - 1).astype(jnp.int32)
    return src_tok, row_w.reshape(r_rows, 1), tile_expert, n_used.reshape(1), dest.reshape(n_tok, TOP_K)


def moe_ffn_residual(x, h, logits, mod, gate_idx, w_g, w_u, w_d, layer_j, *, rows, seq, batch, final_g=None):
    d = x.shape[1]
    n_moe, n_experts, _, d_e = w_g.shape
    tmr = 512 if rows * TOP_K >= 8 * 512 * n_experts else 128
    tn = 512 if d_e % 512 == 0 and d >= 4096 else 256
    src_tok, row_w, tile_expert, n_used, dest = moe_route(logits, n_experts, tmr)
    xg = jnp.take(h, src_tok, axis=0, mode="clip")
    y = moe_experts(xg, row_w, tile_expert, n_used,
                    w_g.reshape(n_moe * n_experts, d, d_e), w_u.reshape(n_moe * n_experts, d, d_e),
                    w_d.reshape(n_moe * n_experts, d_e, d), layer_j * n_experts, tmr=tmr, tn=tn)
    y1 = jnp.take(y, dest[:, 0], axis=0, mode="clip")
    y2 = jnp.take(y, dest[:, 1], axis=0, mode="clip")
    tm = 256
    final = final_g is not None
    in_specs = [pl.BlockSpec((tm, d), lambda i: (i, 0)),
                pl.BlockSpec((tm, d), lambda i: (i, 0)),
                pl.BlockSpec((tm, d), lambda i: (i, 0)),
                pl.BlockSpec((None, 1, d), lambda i: (_mod_row(i, tm, seq, batch), 0, gate_idx))]
    args = [x, y1, y2, mod]
    if final:
        in_specs.append(pl.BlockSpec((1, d), lambda i: (0, 0)))
        args.append(final_g.reshape(1, d))
    return pl.pallas_call(
        functools.partial(_moe_combine_kernel, final=final),
        grid=(rows // tm,),
        in_specs=in_specs,
        out_specs=pl.BlockSpec((tm, d), lambda i: (i, 0)),
        out_shape=jax.ShapeDtypeStruct((rows, d), F32),
        compiler_params=_params(("parallel",), 10 * _nbytes((tm, d), F32)),
        name="moe_combine",
    )(*args)


def _rope_tables(seq, pad_rows):
    rows = seq // GRID_W
    half = HEAD_DIM // 2
    row = jnp.broadcast_to(jnp.arange(rows)[:, None], (rows, GRID_W)).reshape(-1).astype(F32)
    col = jnp.broadcast_to(jnp.arange(GRID_W)[None, :], (rows, GRID_W)).reshape(-1).astype(F32)
    inv_freq = ROPE_BASE ** (-jnp.arange(0, half, 2, dtype=F32) / half)
    ang_r = row[:, None] * inv_freq
    ang_c = col[:, None] * inv_freq
    ang = jnp.concatenate([ang_r, ang_r, ang_c, ang_c], axis=-1)
    cos, sin = jnp.cos(ang), jnp.sin(ang)
    first = (jnp.arange(HEAD_DIM) % half) < half // 2
    ones = jnp.ones((pad_rows, HEAD_DIM), F32)
    zeros = jnp.zeros((pad_rows, HEAD_DIM), F32)
    return (jnp.concatenate([cos, ones]), jnp.concatenate([jnp.where(first, -sin, 0.0), zeros]),
            jnp.concatenate([jnp.where(first, 0.0, sin), zeros]))


def kernel(x, c, ctx, c_ctx, w_mod, b_mod, g_attn_norm, w_in, attn_sink, ssm_a_re, ssm_a_im, ssm_log_dt,
           ssm_b_re, ssm_b_im, ssm_c_re, ssm_c_im, ssm_d, w_glu, g_out_attn, g_out_ssm, w_out, g_ffn_norm,
           w_ff_gate, w_ff_up, w_ff_down, w_router, w_exp_gate, w_exp_up, w_exp_down, g_final):
    batch, seq, d = x.shape
    n_ctx = ctx.shape[1]
    depth = w_in.shape[0]
    n_q = attn_sink.shape[1]
    n_kv = n_q // Q_PER_KV
    attn_w = n_q * HEAD_DIM
    kv_w = n_kv * HEAD_DIM
    rows_x = batch * seq
    rows_all = rows_x + batch * n_ctx
    big = rows_x >= 8192
    tm_big = 1024 if big else 256
    tm_mid = 512 if big else 256
    tn = 512 if big else 256
    geo = dict(seq=seq, batch=batch)

    xs = jnp.concatenate([x.reshape(rows_x, d), ctx.reshape(batch * n_ctx, d)], axis=0)
    cond = jnp.concatenate([c, c_ctx[None, :]], axis=0)
    rope_tabs = _rope_tables(seq, tm_big)
    rope_geo = dict(rows=rows_all, rows_x=rows_x, seq=seq, tm=tm_big, tn=tn)
    ssm_tables = jax.vmap(_ssm_tables)(ssm_a_re, ssm_a_im, ssm_log_dt, ssm_b_re, ssm_b_im, ssm_c_re, ssm_c_im)
    normed = False

    for i in range(depth):
        last = i == depth - 1
        rows_out = rows_x if last else rows_all
        mod = ada_modulation(cond, w_mod, b_mod, i)
        h = norm_modulate(xs, g_attn_norm[i], mod, 0, 1, rows=rows_all, **geo)
        w_in_b = w_in[i].astype(BF16)
        q_rot, q_plain = matmul_rope(h, w_in_b[:, :attn_w], rope_tabs, n_rope_tiles=attn_w // tn,
                                     scale=HEAD_DIM ** -0.5 * LOG2E, both=True, **rope_geo)
        kv = matmul_rope(h, w_in_b[:, attn_w:attn_w + 2 * kv_w], rope_tabs, n_rope_tiles=kv_w // tn,
                         scale=1.0, both=False, **rope_geo)
        u = matmul(h, w_in_b[:, attn_w + 2 * kv_w:], rows=rows_all, tm=tm_big, tn=tn, out_dtype=F32)
        o_attn = attention(q_rot, q_plain, kv, attn_sink[i] * LOG2E, g_out_attn[i], batch=batch, seq=seq, n_ctx=n_ctx,
                           n_kv=n_kv, ctx_queries=not last)
        y = s5_mix(u, ssm_tables, ssm_d[i], i, n_state=ssm_a_re.shape[-1], batch=batch, seq=seq, n_ctx=n_ctx)
        o_ssm = matmul_glu_norm(y, w_glu[i].astype(BF16), g_out_ssm[i], rows=rows_out, tm=tm_big)
        w_out_b = w_out[i].astype(BF16)
        xs = matmul_gated_residual([(o_attn, w_out_b[:attn_w]), (o_ssm, w_out_b[attn_w:])], xs, mod, 2,
                                   rows=rows_out, tm=tm_big, tn=tn, weight_outer=False, **geo)
        j = i // 2
        if i % 2 == 0:
            h2 = norm_modulate(xs, g_ffn_norm[i], mod, 3, 4, rows=rows_out, **geo)
            hid = matmul_swiglu(h2, w_ff_gate[j].astype(BF16), w_ff_up[j].astype(BF16),
                                rows=rows_out, tm=tm_big, tn=tn)
            xs = matmul_gated_residual([(hid, w_ff_down[j].astype(BF16))], xs, mod, 5, rows=rows_out, tm=tm_mid,
                                       tn=tn, weight_outer=True, **geo)
        else:
            n_e = w_router.shape[2]
            wr = jnp.zeros((d, LANES), F32).at[:, :n_e].set(w_router[j])
            h2, logits = norm_modulate(xs, g_ffn_norm[i], mod, 3, 4, rows=rows_out, w_router=wr, **geo)
            xs = moe_ffn_residual(xs, h2, logits, mod, 5, w_exp_gate, w_exp_up, w_exp_down, j,
                                  rows=rows_out, seq=seq, batch=batch, final_g=g_final if last else None)
            normed = last
    if not normed:
        xs = final_norm(xs, g_final, rows=rows_x)
    return xs.reshape(batch, seq, d)
```

```python
import functools
import math

import jax
import jax.numpy as jnp
from jax import lax
from jax.experimental import pallas as pl
from jax.experimental.pallas import tpu as pltpu

F32 = jnp.float32
BF16 = jnp.bfloat16

HEAD_DIM = 128
BLOCK = 128
Q_PER_KV = 3
GRID_W = 64
ROPE_BASE = 10000.0
SSM_GROUP = 16
N_MOD = 6
TOP_K = 2
EPS = 1e-6
NEG_INF = -1e30
LOG2E = math.log2(math.e)

LANES = 128
SUBLANES = 8
VMEM_BYTES_V7X = 64 * 1024 * 1024
VMEM_CAP = VMEM_BYTES_V7X - 8 * 1024 * 1024

SSM_CHUNK = 16


def _params(sem, est_bytes):
    limit = int(min(VMEM_CAP, max(32 * 1024 * 1024, est_bytes * 5 // 4)))
    return pltpu.CompilerParams(dimension_semantics=sem, vmem_limit_bytes=limit)


def _nbytes(shape, dtype):
    return math.prod(shape) * jnp.dtype(dtype).itemsize


def _mod_row(i, tm, seq, batch):
    return jnp.minimum((i * tm) // seq, batch)


def _mod_kernel(cb_ref, w_ref, b_ref, o_ref, *, n_rows, tn):
    k_dim = w_ref.shape[0]
    n_chunks = tn // LANES

    def body(kb, accs):
        k0 = pl.multiple_of(kb * SUBLANES, SUBLANES)
        wv = w_ref[pl.ds(k0, SUBLANES), :]
        new = []
        for r in range(n_rows):
            cv = cb_ref[r, pl.ds(k0, SUBLANES), :]
            sv = cv * jax.nn.sigmoid(cv)
            for c in range(n_chunks):
                new.append(accs[r * n_chunks + c] + wv[:, c * LANES:(c + 1) * LANES] * sv)
        return tuple(new)

    init = tuple(jnp.zeros((SUBLANES, LANES), F32) for _ in range(n_rows * n_chunks))
    accs = lax.fori_loop(0, k_dim // SUBLANES, body, init, unroll=4)
    o_ref[...] = jnp.zeros_like(o_ref)
    for r in range(n_rows):
        row = jnp.concatenate(
            [jnp.sum(accs[r * n_chunks + c], axis=0, keepdims=True) for c in range(n_chunks)], axis=1)
        o_ref[r:r + 1, :] = row + b_ref[...]


def ada_modulation(cond_rows, w_mod, b_mod, layer):
    n_rows, d = cond_rows.shape
    depth, _, n = w_mod.shape
    tn = 1024
    cb =jnp.broadcast_to(cond_rows[:, :, None], (n_rows, d, LANES))
    est = 2 * (_nbytes((d, tn), F32) + _nbytes((n_rows, d, LANES), F32))
    out = pl.pallas_call(
        functools.partial(_mod_kernel, n_rows=n_rows, tn=tn),
        grid=(n // tn,),
        in_specs=[pl.BlockSpec((n_rows, d, LANES), lambda j: (0, 0, 0)),
                  pl.BlockSpec((None, d, tn), lambda j: (layer, 0, j)),
                  pl.BlockSpec((None, 1, tn), lambda j: (layer, 0, j))],
        out_specs=pl.BlockSpec((SUBLANES, tn), lambda j: (0, j)),
        out_shape=jax.ShapeDtypeStruct((SUBLANES, n), F32),
        compiler_params=_params(("parallel",), est),
        name="ada_modulation",
    )(cb, w_mod, b_mod.reshape(depth, 1, n))
    return out.reshape(SUBLANES, 1, n)


def _split_bf16(v):
    hi = v.astype(BF16)
    return hi, (v - hi.astype(F32)).astype(BF16)


def _normmod_kernel(x_ref, g_ref, shift_ref, scale_ref, *rest, with_router):
    xf = x_ref[...]
    y = xf * lax.rsqrt(jnp.mean(xf * xf, axis=-1, keepdims=True) + EPS)
    h = (y * g_ref[...]) * (1.0 + scale_ref[...]) + shift_ref[...]
    if with_router:
        wr_ref, o_ref, lg_ref = rest
        hs = _split_bf16(h)
        ws = _split_bf16(wr_ref[...])
        acc = None
        for a, b in ((0, 0), (0, 1), (1, 0)):
            t = jnp.dot(hs[a], ws[b], preferred_element_type=F32)
            acc = t if acc is None else acc + t
        lg_ref[...] = acc
    else:
        (o_ref,) = rest
    o_ref[...] = h.astype(BF16)


def norm_modulate(x, g, mod, shift_idx, scale_idx, *, rows, seq, batch, w_router=None):
    d = x.shape[1]
    tm = 256
    with_router = w_router is not None
    in_specs = [pl.BlockSpec((tm, d), lambda i: (i, 0)),
                pl.BlockSpec((1, d), lambda i: (0, 0)),
                pl.BlockSpec((None, 1, d), lambda i: (_mod_row(i, tm, seq, batch), 0, shift_idx)),
                pl.BlockSpec((None, 1, d), lambda i: (_mod_row(i, tm, seq, batch), 0, scale_idx))]
    args = [x, g.reshape(1, d), mod, mod]
    out_specs = [pl.BlockSpec((tm, d), lambda i: (i, 0))]
    out_shape = [jax.ShapeDtypeStruct((rows, d), BF16)]
    if with_router:
        in_specs.append(pl.BlockSpec((d, LANES), lambda i: (0, 0)))
        args.append(w_router)
        out_specs.append(pl.BlockSpec((tm, LANES), lambda i: (i, 0)))
        out_shape.append(jax.ShapeDtypeStruct((rows, LANES), F32))
    est = 2 * (_nbytes((tm, d), F32) + _nbytes((tm, d), BF16)) + 8 * _nbytes((tm, d), F32)
    outs = pl.pallas_call(
        functools.partial(_normmod_kernel, with_router=with_router),
        grid=(pl.cdiv(rows, tm),),
        in_specs=in_specs, out_specs=out_specs, out_shape=out_shape,
        compiler_params=_params(("parallel",), est),
        name="norm_modulate_router" if with_router else "norm_modulate",
    )(*args)
    return outs if with_router else outs[0]


def _mm_kernel(x_ref, w_ref, o_ref):
    o_ref[...] = jnp.dot(x_ref[...], w_ref[...], preferred_element_type=F32).astype(o_ref.dtype)


def _mm_swiglu_kernel(x_ref, wg_ref, wu_ref, o_ref):
    x = x_ref[...]
    a = jnp.dot(x, wg_ref[...], preferred_element_type=F32)
    b = jnp.dot(x, wu_ref[...], preferred_element_type=F32)
    o_ref[...] = (a * jax.nn.sigmoid(a) * b).astype(o_ref.dtype)


def _rms(x, g):
    return x * lax.rsqrt(jnp.mean(x * x, axis=-1, keepdims=True) + EPS) * g


def _mm_resid_kernel(*refs, n_parts):
    x_refs, w_refs = refs[:n_parts], refs[n_parts:2 * n_parts]
    r_ref, gate_ref, o_ref = refs[2 * n_parts:]
    acc = None
    for x_ref, w_ref in zip(x_refs, w_refs):
        t = jnp.dot(x_ref[...], w_ref[...], preferred_element_type=F32)
        acc = t if acc is None else acc + t
    o_ref[...] = r_ref[...] + gate_ref[...] * acc


def _mm_glu_norm_kernel(y_ref, w_ref, g_ref, o_ref):
    y = y_ref[...]
    z = jnp.dot(y.astype(BF16), w_ref[...], preferred_element_type=F32)
    o_ref[...] = _rms(y * jax.nn.sigmoid(z), g_ref[...]).astype(BF16)


def matmul(x, w, *, rows, tm, tn, out_dtype):
    k, n = w.shape
    est = 2 * (_nbytes((tm, k), x.dtype) + _nbytes((k, tn), w.dtype) + _nbytes((tm, tn), out_dtype))
    return pl.pallas_call(
        _mm_kernel,
        grid=(pl.cdiv(rows, tm), pl.cdiv(n, tn)),
        in_specs=[pl.BlockSpec((tm, k), lambda i, j: (i, 0)),
                  pl.BlockSpec((k, tn), lambda i, j: (0, j))],
        out_specs=pl.BlockSpec((tm, tn), lambda i, j: (i, j)),
        out_shape=jax.ShapeDtypeStruct((rows, n), out_dtype),
        compiler_params=_params(("parallel", "parallel"), est),
        name="matmul",
    )(x, w)


def _rope(x, cos, sin_lo, sin_hi):
    return (x * cos + pltpu.roll(x, HEAD_DIM - HEAD_DIM // 4, 1) * sin_lo
            + pltpu.roll(x, HEAD_DIM // 4, 1) * sin_hi)


def _mm_rope_kernel(x_ref, w_ref, cos_ref, slo_ref, shi_ref, *o_refs, n_rope_tiles, scale, both, sub):
    j = pl.program_id(1)
    tm, tn = o_refs[0].shape
    for r0 in range(0, tm, sub):
        acc = jnp.dot(x_ref[r0:r0 + sub, :], w_ref[...], preferred_element_type=F32)
        if scale != 1.0:
            acc = acc * scale
        cos, slo, shi = (r[r0:r0 + sub, :] for r in (cos_ref, slo_ref, shi_ref))
        rot = jnp.concatenate([_rope(acc[:, h * HEAD_DIM:(h + 1) * HEAD_DIM], cos, slo, shi)
                               for h in range(tn // HEAD_DIM)], axis=1)
        if both:
            o_refs[0][r0:r0 + sub, :] = rot.astype(BF16)
            o_refs[1][r0:r0 + sub, :] = acc.astype(BF16)
        else:
            o_refs[0][r0:r0 + sub, :] = jnp.where(j < n_rope_tiles, rot, acc).astype(BF16)


def matmul_rope(x, w, tabs, *, rows, rows_x, seq, tm, tn, n_rope_tiles, scale, both):
    k, n = w.shape
    per_seq = seq // tm

    def tab_map(i, j):
        return (jnp.where(i * tm < rows_x, i % per_seq, per_seq), 0)

    n_out = 2 if both else 1
    est = 2 * (_nbytes((tm, k), BF16) + _nbytes((k, tn), BF16) + n_out * _nbytes((tm, tn), BF16)
               + 3 * _nbytes((tm, HEAD_DIM), F32)) + 4 * _nbytes((tm, tn), F32)
    out = pl.pallas_call(
        functools.partial(_mm_rope_kernel, n_rope_tiles=n_rope_tiles, scale=scale, both=both, sub=min(tm, 256)),
        grid=(pl.cdiv(rows, tm), n // tn),
        in_specs=[pl.BlockSpec((tm, k), lambda i, j: (i, 0)),
                  pl.BlockSpec((k, tn), lambda i, j: (0, j))] + [pl.BlockSpec((tm, HEAD_DIM), tab_map)] * 3,
        out_specs=[pl.BlockSpec((tm, tn), lambda i, j: (i, j))] * n_out,
        out_shape=[jax.ShapeDtypeStruct((rows, n), BF16)] * n_out,
        compiler_params=_params(("parallel", "parallel"), est),
        name="matmul_rope",
    )(x, w, *tabs)
    return out if both else out[0]


def matmul_swiglu(x, wg, wu, *, rows, tm, tn):
    k, n = wg.shape
    est = 2 * (_nbytes((tm, k), BF16) + 2 * _nbytes((k, tn), BF16) + _nbytes((tm, tn), BF16)) \
        + 3 * _nbytes((tm, tn), F32)
    return pl.pallas_call(
        _mm_swiglu_kernel,
        grid=(pl.cdiv(rows, tm), pl.cdiv(n, tn)),
        in_specs=[pl.BlockSpec((tm, k), lambda i, j: (i, 0)),
                  pl.BlockSpec((k, tn), lambda i, j: (0, j)),
                  pl.BlockSpec((k, tn), lambda i, j: (0, j))],
        out_specs=pl.BlockSpec((tm, tn), lambda i, j: (i, j)),
        out_shape=jax.ShapeDtypeStruct((rows, n), BF16),
        compiler_params=_params(("parallel", "parallel"), est),
        name="matmul_swiglu",
    )(x, wg, wu)


def matmul_gated_residual(xw, resid, mod, gate_idx, *, rows, tm, tn, seq, batch, weight_outer):
    xs_, ws_ = [p[0] for p in xw], [p[1] for p in xw]
    n = ws_[0].shape[1]
    k = sum(w.shape[0] for w in ws_)
    nj = n // tn
    if weight_outer:
        grid = (nj, pl.cdiv(rows, tm))
        ij = lambda a, b: (b, a)
    else:
        grid = (pl.cdiv(rows, tm), nj)
        ij = lambda a, b: (a, b)

    def gate_map(a, b):
        i, j = ij(a, b)
        return (_mod_row(i, tm, seq, batch), 0, gate_idx * nj + j)

    est = 2 * (_nbytes((tm, k), BF16) + _nbytes((k, tn), BF16) + 2 * _nbytes((tm, tn), F32))
    return pl.pallas_call(
        functools.partial(_mm_resid_kernel, n_parts=len(xw)),
        grid=grid,
        in_specs=[pl.BlockSpec((tm, x.shape[1]), lambda a, b: (ij(a, b)[0], 0)) for x in xs_]
        + [pl.BlockSpec((w.shape[0], tn), lambda a, b: (0, ij(a, b)[1])) for w in ws_]
        + [pl.BlockSpec((tm, tn), lambda a, b: ij(a, b)),
           pl.BlockSpec((None, 1, tn), gate_map)],
        out_specs=pl.BlockSpec((tm, tn), lambda a, b: ij(a, b)),
        out_shape=jax.ShapeDtypeStruct((rows, n), F32),
        compiler_params=_params(("parallel", "parallel"), est),
        name="matmul_gated_residual",
    )(*xs_, *ws_, resid, mod)


def matmul_glu_norm(y, w, g, *, rows, tm):
    k, n = w.shape
    est = 2 * (_nbytes((tm, k), F32) + _nbytes((k, n), BF16) + _nbytes((tm, n), BF16)) + 4 * _nbytes((tm, n), F32)
    return pl.pallas_call(
        _mm_glu_norm_kernel,
        grid=(pl.cdiv(rows, tm),),
        in_specs=[pl.BlockSpec((tm, k), lambda i: (i, 0)),
                  pl.BlockSpec((k, n), lambda i: (0, 0)),
                  pl.BlockSpec((1, n), lambda i: (0, 0))],
        out_specs=pl.BlockSpec((tm, n), lambda i: (i, 0)),
        out_shape=jax.ShapeDtypeStruct((rows, n), BF16),
        compiler_params=_params(("parallel",), est),
        name="matmul_glu_norm",
    )(y, w, g.reshape(1, n))


def _nt_dot(a, b):
    return lax.dot_general(a, b, (((1,), (1,)), ((), ())), preferred_element_type=F32)


def _attn_kernel(sink_ref, qr_ref, qp_ref, kp_ref, k0_ref, kn_ref, vp_ref, v0_ref, vn_ref, kc_ref, vc_ref,
                 bias_ref, g_ref, o_ref, o_scr, *, n_blocks, ctx_queries, n_kv):
    n = pl.program_id(1)
    qw = Q_PER_KV * HEAD_DIM

    def stack_heads(ref, hh):
        return jnp.concatenate([ref[:, hh * qw + g * HEAD_DIM:hh * qw + (g + 1) * HEAD_DIM]
                                for g in range(Q_PER_KV)], axis=0)

    def head(ref, hh):
        return ref[:, hh * HEAD_DIM:(hh + 1) * HEAD_DIM]

    def run_head(window, hh):
        sink = jnp.concatenate(
            [jnp.full((BLOCK, 1), sink_ref[hh * Q_PER_KV + g], F32) for g in range(Q_PER_KV)], axis=0)
        s = _nt_dot(stack_heads(qp_ref, hh), head(kc_ref, hh))
        v = head(vc_ref, hh)
        if window:
            k_win = jnp.concatenate([head(kp_ref, hh), head(k0_ref, hh), head(kn_ref, hh)], axis=0)
            s_w = _nt_dot(stack_heads(qr_ref, hh), k_win) + bias_ref[...]
            s = jnp.concatenate([s_w, s], axis=1)
            v = jnp.concatenate([head(vp_ref, hh), head(v0_ref, hh), head(vn_ref, hh), v], axis=0)
        n_keys = s.shape[1]
        m = jnp.maximum(jnp.max(s, axis=1, keepdims=True), sink)
        m_b = jnp.broadcast_to(m, (Q_PER_KV * BLOCK, LANES))
        sink_b = jnp.broadcast_to(sink, (Q_PER_KV * BLOCK, LANES))
        p = jnp.exp2(s - jnp.concatenate([m_b] * (n_keys // LANES), axis=1)).astype(BF16)
        v_ext = jnp.concatenate([v, jnp.ones((n_keys, HEAD_DIM), BF16)], axis=1)
        o_ext = jnp.dot(p, v_ext, preferred_element_type=F32)
        o = o_ext[:, :HEAD_DIM] / (o_ext[:, HEAD_DIM:] + jnp.exp2(sink_b - m_b))
        for g in range(Q_PER_KV):
            o_scr[:, hh * qw + g * HEAD_DIM:hh * qw + (g + 1) * HEAD_DIM] = o[g * BLOCK:(g + 1) * BLOCK]

    def run(window):
        for hh in range(n_kv):
            run_head(window, hh)
        o_ref[...] = _rms(o_scr[...], g_ref[...]).astype(BF16)

    if ctx_queries:
        pl.when(n < n_blocks)(lambda: run(True))
        pl.when(n >= n_blocks)(lambda: run(False))
    else:
        run(True)


def _window_bias(n_blocks):
    rows = Q_PER_KV * BLOCK
    row = (jnp.arange(rows) % BLOCK)[:, None]
    col = jnp.arange(3 * BLOCK)[None, :]
    band = (col >= row) & (col <= row + 2 * BLOCK)
    variants = []
    for code in range(4):
        ok = band
        if code & 1:
            ok = ok & (col >= BLOCK)
        if code & 2:
            ok = ok & (col < 2 * BLOCK)
        variants.append(jnp.where(ok, 0.0, NEG_INF).astype(F32))
    return jnp.stack(variants)


def attention(q_rot, q_plain, kv, sink, g_out, *, batch, seq, n_ctx, n_kv, ctx_queries):
    nb = seq // BLOCK
    qb = n_ctx // BLOCK if ctx_queries else 0
    qw = n_kv * Q_PER_KV * HEAD_DIM
    kw = n_kv * HEAD_DIM
    rows_out = batch * seq + (batch * n_ctx if ctx_queries else 0)
    ctx_blk0 = batch * seq // n_ctx
    q_blk0 = batch * seq // BLOCK

    def q_map(b, n, s):
        return (jnp.where(n < nb, b * nb + n, q_blk0 + b * qb + (n - nb)), 0)

    def kv_spec(col, shift):
        return pl.BlockSpec((BLOCK, kw), lambda b, n, s: (b * nb + jnp.clip(n + shift, 0, nb - 1), col))

    def bias_map(b, n, s):
        return ((n == 0).astype(jnp.int32) + 2 * (n == nb - 1).astype(jnp.int32), 0, 0)

    in_specs = [pl.BlockSpec((BLOCK, qw), q_map), pl.BlockSpec((BLOCK, qw), q_map),
                kv_spec(0, -1), kv_spec(0, 0), kv_spec(0, 1),
                kv_spec(1, -1), kv_spec(1, 0), kv_spec(1, 1),
                pl.BlockSpec((n_ctx, kw), lambda b, n, s: (ctx_blk0 + b, 0)),
                pl.BlockSpec((n_ctx, kw), lambda b, n, s: (ctx_blk0 + b, 1)),
                pl.BlockSpec((None, Q_PER_KV * BLOCK, 3 * BLOCK), bias_map),
                pl.BlockSpec((1, qw), lambda b, n, s: (0, 0))]
    grid_spec = pltpu.PrefetchScalarGridSpec(
        num_scalar_prefetch=1, grid=(batch, nb + qb), in_specs=in_specs,
        out_specs=pl.BlockSpec((BLOCK, qw), q_map),
        scratch_shapes=[pltpu.VMEM((BLOCK, qw), F32)])
    return pl.pallas_call(
        functools.partial(_attn_kernel, n_blocks=nb, ctx_queries=ctx_queries, n_kv=n_kv),
        grid_spec=grid_spec,
        out_shape=jax.ShapeDtypeStruct((rows_out, qw), BF16),
        compiler_params=_params(("parallel", "parallel"), 16 * 1024 * 1024),
        name="attention",
    )(sink, q_rot, q_plain, *([kv] * 8), _window_bias(nb), g_out.reshape(1, qw))


def _ssm_tables(a_re, a_im, log_dt, b_re, b_im, c_re, c_im):
    hp = lax.Precision.HIGHEST
    t_len = SSM_CHUNK
    g_per = LANES // SSM_GROUP
    n_groups, n_state = a_re.shape[1:]
    n_lb = n_groups // g_per
    dt = jnp.exp(log_dt.astype(F32))[..., None]
    lam_re, lam_im = a_re.astype(F32), a_im.astype(F32)
    mag = jnp.exp(lam_re * dt)
    abar = lax.complex(mag * jnp.cos(lam_im * dt), mag * jnp.sin(lam_im * dt))
    lam = lax.complex(lam_re, lam_im)
    bbar = ((abar - 1.0) / lam)[..., None] * lax.complex(b_re.astype(F32), b_im.astype(F32))
    cmat = lax.complex(c_re.astype(F32), c_im.astype(F32))

    def powers(d, exps):
        e = jnp.asarray(exps, F32)[:, None, None]
        m = jnp.exp(lam_re[d] * dt[d] * e)
        ph = lam_im[d] * dt[d] * e
        return lax.complex(m * jnp.cos(ph), m * jnp.sin(ph))

    ar = list(range(t_len + 1))
    pw_f, pw_b = powers(0, ar), powers(1, ar)

    kf = jnp.real(jnp.einsum('gcp,tgp,gpi->tgci', cmat[0], pw_f[:t_len], bbar[0], precision=hp))
    kb = jnp.real(jnp.einsum('gcp,tgp,gpi->tgci', cmat[1], pw_b[:t_len], bbar[1], precision=hp))
    lag = jnp.arange(2 * t_len - 1) - (t_len - 1)
    k_lag = jnp.where((lag >= 0)[:, None, None, None], kf[jnp.clip(lag, 0)], 0.0) \
        + jnp.where((lag <= 0)[:, None, None, None], kb[jnp.clip(-lag, 0)], 0.0)
    k_lag = jnp.swapaxes(k_lag, -1, -2).reshape(2 * t_len - 1, n_lb, g_per, SSM_GROUP, SSM_GROUP)
    eye = jnp.eye(g_per, dtype=F32)
    k_lag = jnp.einsum('mngab,gh->nmgahb', k_lag, eye).reshape(n_lb, 2 * t_len - 1, LANES, LANES)

    rep = LANES // n_state

    def e_coef(d, pw_sel):
        pw_rep = jnp.concatenate([pw_sel] * rep, axis=-1)
        bb_rep = jnp.concatenate([bbar[d]] * rep, axis=-2)
        return jnp.swapaxes(pw_rep[..., None] * bb_rep[None], -1, -2)

    ef, eb = e_coef(0, powers(0, [t_len - 1 - j for j in range(t_len)])), e_coef(1, pw_b[:t_len])
    quarters = jnp.stack([jnp.real(ef), jnp.imag(ef), jnp.real(eb), jnp.imag(eb)], axis=-2)
    e_c = quarters.reshape(t_len, n_lb, LANES, 4 * LANES).transpose(1, 0, 2, 3)

    def f_coef(d, pw_sel):
        return cmat[d][None] * pw_sel[:, :, None, :]

    ff, fb = f_coef(0, pw_f[1:]), f_coef(1, powers(1, [t_len - l for l in range(t_len)]))
    f_c = jnp.stack([jnp.real(ff), -jnp.imag(ff), jnp.real(fb), -jnp.imag(fb)])
    f_c = f_c.reshape(4, t_len, n_lb, g_per, SSM_GROUP, n_state).transpose(2, 0, 5, 1, 3, 4)
    f_c = f_c.reshape(n_lb, 4 * n_state, t_len * LANES)

    def lay(v):
        return v.reshape(n_lb, g_per * n_state)

    a_chunk = jnp.concatenate([lay(jnp.real(pw_f[t_len])), lay(jnp.imag(pw_f[t_len])),
                               lay(jnp.real(pw_b[t_len])), lay(jnp.imag(pw_b[t_len]))], axis=-1)
    return k_lag.astype(BF16), e_c, f_c, a_chunk.reshape(1, -1)


def _chunk_inputs(x_ref, tmc):
    return [x_ref[pl.ds(j, tmc, stride=SSM_CHUNK), :] for j in range(SSM_CHUNK)]


def _ssm_local_kernel(x_ref, ec_ref, s_ref, e_scr, *, n_state):
    tmc = s_ref.shape[0]
    qw = e_scr.shape[1] // 4

    @pl.when(pl.program_id(1) == 0)
    def _():
        row_g = lax.broadcasted_iota(jnp.int32, (LANES, qw), 0) // SSM_GROUP
        col_g = lax.broadcasted_iota(jnp.int32, (LANES, qw), 1) // n_state
        same = row_g == col_g
        for j in range(SSM_CHUNK):
            for q in range(4):
                v = ec_ref[j, :, q * LANES:(q + 1) * LANES]
                tiled = jnp.concatenate([v] * (qw // LANES), axis=1)
                e_scr[j * LANES:(j + 1) * LANES, q * qw:(q + 1) * qw] = jnp.where(same, tiled, 0.0).astype(BF16)

    xs = jnp.concatenate([x.astype(BF16) for x in _chunk_inputs(x_ref, tmc)], axis=1)
    s_ref[...] = jnp.dot(xs, e_scr[...], preferred_element_type=F32)


def _ssm_scan_kernel(s_ref, a_ref, h_ref, *, batch, nc_x, nc_c):
    sw = s_ref.shape[1] // 4
    a = a_ref[...]
    afr, afi, abr, abi = (a[:, k * sw:(k + 1) * sw] for k in range(4))
    ctx0 = batch * nc_x

    def step(row_f, row_b, st):
        new = []
        for (row, ar, ai, off, (hr, hi)) in ((row_f, afr, afi, 0, st[0]), (row_b, abr, abi, 2 * sw, st[1])):
            h_ref[pl.ds(row, 1), off:off + sw] = hr
            h_ref[pl.ds(row, 1), off + sw:off + 2 * sw] = hi
            sr = s_ref[pl.ds(row, 1), off:off + sw]
            si = s_ref[pl.ds(row, 1), off + sw:off + 2 * sw]
            new.append((ar * hr - ai * hi + sr, ar * hi + ai * hr + si))
        return tuple(new)

    zero = jnp.zeros((1, sw), F32)
    for b in range(batch):
        st = ((zero, zero), (zero, zero))
        c_base = ctx0 + b * nc_c
        st = lax.fori_loop(0, nc_c, lambda t, s: step(c_base + t, c_base + nc_c - 1 - t, s), st)
        x_base = b * nc_x
        lax.fori_loop(0, nc_x, lambda t, s: step(x_base + t, x_base + nc_x - 1 - t, s), st)


def _ssm_out_kernel(x_ref, h_ref, kl_ref, fc_ref, d_ref, o_ref, w_scr, *, n_state):
    t = SSM_CHUNK
    tmc = h_ref.shape[0]
    sdim = h_ref.shape[1]
    qw = sdim // 4

    @pl.when(pl.program_id(1) == 0)
    def _():
        for j in range(t):
            for l in range(t):
                w_scr[j * LANES:(j + 1) * LANES, l * LANES:(l + 1) * LANES] = kl_ref[l - j + t - 1]
        lane_g = (lax.broadcasted_iota(jnp.int32, (n_state, t * LANES), 1) % LANES) // SSM_GROUP
        for q in range(4):
            v = fc_ref[q * n_state:(q + 1) * n_state, :]
            for g in range(qw // n_state):
                r0 = t * LANES + q * qw + g * n_state
                w_scr[r0:r0 + n_state, :] = jnp.where(lane_g == g, v, 0.0).astype(BF16)

    xs = _chunk_inputs(x_ref, tmc)
    lhs = jnp.concatenate([x.astype(BF16) for x in xs] + [h_ref[...].astype(BF16)], axis=1)
    y = jnp.dot(lhs, w_scr[...], preferred_element_type=F32)
    d = d_ref[...]
    for l in range(t):
        o_ref[pl.ds(l, tmc, stride=t), :] = jax.nn.gelu(y[:, l * LANES:(l + 1) * LANES] + d * xs[l])


def s5_mix(u, tables, d_skip, layer, *, n_state, batch, seq, n_ctx):
    k_lag, e_c, f_c, a_chunk = tables
    rows, ssm_width = u.shape
    t = SSM_CHUNK
    n_lb = ssm_width // LANES
    rc = rows // t
    sdim = 4 * (LANES // SSM_GROUP) * n_state
    tmc = max(m for m in range(SUBLANES, min(rc, 264) + 1, SUBLANES) if rc % m == 0)
    x_spec = pl.BlockSpec((tmc * t, LANES), lambda g, r: (r, g))
    x_bytes = _nbytes((tmc * t, LANES), F32)

    s_loc = pl.pallas_call(
        functools.partial(_ssm_local_kernel, n_state=n_state),
        grid=(n_lb, rc // tmc),
        in_specs=[x_spec, pl.BlockSpec((None, None, t, LANES, 4 * LANES), lambda g, r: (layer, g, 0, 0, 0))],
        out_specs=pl.BlockSpec((tmc, sdim), lambda g, r: (r, g)),
        out_shape=jax.ShapeDtypeStruct((rc, n_lb * sdim), F32),
        scratch_shapes=[pltpu.VMEM((t * LANES, sdim), BF16)],
        compiler_params=_params(("arbitrary", "arbitrary"),
                                2 * x_bytes + 2 * _nbytes((t, LANES, 4 * LANES), F32)
                                + _nbytes((t * LANES, sdim), BF16) + 4 * _nbytes((tmc, sdim), F32)),
        name="s5_local_states",
    )(u, e_c)

    h_in = pl.pallas_call(
        functools.partial(_ssm_scan_kernel, batch=batch, nc_x=seq // t, nc_c=n_ctx // t),
        grid=(n_lb,),
        in_specs=[pl.BlockSpec((rc, sdim), lambda g: (0, g)),
                  pl.BlockSpec((None, 1, sdim), lambda g: (layer, 0, g))],
        out_specs=pl.BlockSpec((rc, sdim), lambda g: (0, g)),
        out_shape=jax.ShapeDtypeStruct((rc, n_lb * sdim), F32),
        compiler_params=_params(("parallel",), 4 * _nbytes((rc, sdim), F32)),
        name="s5_chunk_scan",
    )(s_loc, a_chunk)

    return pl.pallas_call(
        functools.partial(_ssm_out_kernel, n_state=n_state),
        grid=(n_lb, rc // tmc),
        in_specs=[x_spec,
                  pl.BlockSpec((tmc, sdim), lambda g, r: (r, g)),
                  pl.BlockSpec((None, None, 2 * t - 1, LANES, LANES), lambda g, r: (layer, g, 0, 0, 0)),
                  pl.BlockSpec((None, None, 4 * n_state, t * LANES), lambda g, r: (layer, g, 0, 0)),
                  pl.BlockSpec((1, LANES), lambda g, r: (0, g))],
        out_specs=x_spec,
        out_shape=jax.ShapeDtypeStruct((rows, ssm_width), F32),
        scratch_shapes=[pltpu.VMEM((t * LANES + sdim, t * LANES), BF16)],
        compiler_params=_params(("arbitrary", "arbitrary"),
                                4 * x_bytes + 2 * _nbytes((tmc, sdim), F32)
                                + 2 * _nbytes((2 * t - 1, LANES, LANES), BF16)
                                + 2 * _nbytes((4 * n_state, t * LANES), F32)
                                + _nbytes((t * LANES + sdim, t * LANES), BF16) + 4 * _nbytes((tmc, sdim), F32)),
        name="s5_readout",
    )(u, h_in, k_lag, f_c, d_skip.reshape(1, ssm_width))


def _final_norm_kernel(x_ref, g_ref, o_ref):
    o_ref[...] = _rms(x_ref[...], g_ref[...])


def final_norm(x, g, *, rows):
    d = x.shape[1]
    tm = 256
    return pl.pallas_call(
        _final_norm_kernel,
        grid=(pl.cdiv(rows, tm),),
        in_specs=[pl.BlockSpec((tm, d), lambda i: (i, 0)), pl.BlockSpec((1, d), lambda i: (0, 0))],
        out_specs=pl.BlockSpec((tm, d), lambda i: (i, 0)),
        out_shape=jax.ShapeDtypeStruct((rows, d), F32),
        compiler_params=_params(("parallel",), 6 * _nbytes((tm, d), F32)),
        name="final_norm",
    )(x, g.reshape(1, d))


def _new_expert(te_ref, t):
    return jnp.logical_or(t == 0, te_ref[t] != te_ref[jnp.maximum(t - 1, 0)])


def _moe_up_kernel(te_ref, nu_ref, x_ref, wg_ref, wu_ref, o_ref, wg_scr, wu_scr):
    t = pl.program_id(1)

    @pl.when(_new_expert(te_ref, t))
    def _():
        wg_scr[...] = wg_ref[...].astype(BF16)
        wu_scr[...] = wu_ref[...].astype(BF16)

    @pl.when(t < nu_ref[0])
    def _():
        x = x_ref[...]
        a = jnp.dot(x, wg_scr[...], preferred_element_type=F32)
        b = jnp.dot(x, wu_scr[...], preferred_element_type=F32)
        o_ref[...] = (a * jax.nn.sigmoid(a) * b).astype(o_ref.dtype)

    @pl.when(t >= nu_ref[0])
    def _():
        o_ref[...] = jnp.zeros_like(o_ref)


def _moe_down_kernel(te_ref, nu_ref, h_ref, wd_ref, rw_ref, o_ref, wd_scr):
    t = pl.program_id(1)

    @pl.when(_new_expert(te_ref, t))
    def _():
        wd_scr[...] = wd_ref[...].astype(BF16)

    @pl.when(t < nu_ref[0])
    def _():
        acc = jnp.dot(h_ref[...], wd_scr[...], preferred_element_type=F32)
        o_ref[...] = (acc * rw_ref[...]).astype(o_ref.dtype)

    @pl.when(t >= nu_ref[0])
    def _():
        o_ref[...] = jnp.zeros_like(o_ref)


def moe_experts(xg, row_w, tile_expert, n_used, w_g, w_u, w_d, e_base, *, tmr, tn, tn_down):
    r_rows, d = xg.shape
    d_e = w_g.shape[2]
    n_tiles = r_rows // tmr
    est = 2 * (_nbytes((tmr, d), BF16) + 2 * _nbytes((d, tn), F32) + _nbytes((tmr, tn), BF16)) \
        + 2 * _nbytes((d, tn), BF16) + 3 * _nbytes((tmr, tn), F32)
    hidden = pl.pallas_call(
        _moe_up_kernel,
        grid_spec=pltpu.PrefetchScalarGridSpec(
            num_scalar_prefetch=2, grid=(d_e // tn, n_tiles),
            in_specs=[pl.BlockSpec((tmr, d), lambda j, t, te, nu: (t, 0)),
                      pl.BlockSpec((None, d, tn), lambda j, t, te, nu: (e_base + te[t], 0, j)),
                      pl.BlockSpec((None, d, tn), lambda j, t, te, nu: (e_base + te[t], 0, j))],
            out_specs=pl.BlockSpec((tmr, tn), lambda j, t, te, nu: (t, j)),
            scratch_shapes=[pltpu.VMEM((d, tn), BF16), pltpu.VMEM((d, tn), BF16)]),
        out_shape=jax.ShapeDtypeStruct((r_rows, d_e), BF16),
        compiler_params=_params(("arbitrary", "arbitrary"), est),
        name="moe_gate_up",
    )(tile_expert, n_used, xg, w_g, w_u)
    tn = tn_down
    est = 2 * (_nbytes((tmr, d_e), BF16) + _nbytes((d_e, tn), F32) + _nbytes((tmr, tn), BF16)) \
        + _nbytes((d_e, tn), BF16) + 2 * _nbytes((tmr, tn), F32)
    return pl.pallas_call(
        _moe_down_kernel,
        grid_spec=pltpu.PrefetchScalarGridSpec(
            num_scalar_prefetch=2, grid=(d // tn, n_tiles),
            in_specs=[pl.BlockSpec((tmr, d_e), lambda j, t, te, nu: (t, 0)),
                      pl.BlockSpec((None, d_e, tn), lambda j, t, te, nu: (e_base + te[t], 0, j)),
                      pl.BlockSpec((tmr, 1), lambda j, t, te, nu: (t, 0))],
            out_specs=pl.BlockSpec((tmr, tn), lambda j, t, te, nu: (t, j)),
            scratch_shapes=[pltpu.VMEM((d_e, tn), BF16)]),
        out_shape=jax.ShapeDtypeStruct((r_rows, d), BF16),
        compiler_params=_params(("arbitrary", "arbitrary"), est),
        name="moe_down",
    )(tile_expert, n_used, hidden, w_d, row_w)


def _moe_combine_kernel(x_ref, y1_ref, y2_ref, gate_ref, *rest, final):
    out = x_ref[...] + gate_ref[...] * (y1_ref[...].astype(F32) + y2_ref[...].astype(F32))
    if final:
        g_ref, o_ref = rest
        o_ref[...] = _rms(out, g_ref[...])
    else:
        (o_ref,) = rest
        o_ref[...] = out


def moe_route(logits, n_experts, tmr):
    n_tok = logits.shape[0]
    lg = logits[:, :n_experts]
    top_v, top_i = lax.top_k(lg, TOP_K)
    top_w = jax.nn.softmax(top_v, axis=-1)
    flat_e = top_i.reshape(-1)
    onehot = (flat_e[:, None] == jnp.arange(n_experts)[None, :]).astype(jnp.int32)
    rank = jnp.take_along_axis(jnp.cumsum(onehot, axis=0) - onehot, flat_e[:, None], axis=1)[:, 0]
    counts = jnp.sum(onehot, axis=0)
    tiles_per = (counts + tmr - 1) // tmr
    tile_start = jnp.cumsum(tiles_per) - tiles_per
    dest = tile_start[flat_e] * tmr + rank
    n_tiles = (n_tok * TOP_K) // tmr + n_experts
    r_rows = n_tiles * tmr
    slot_tok = (jnp.arange(n_tok * TOP_K, dtype=jnp.int32) // TOP_K).astype(F32)
    pad_tok = (jnp.arange(r_rows, dtype=jnp.int32) % n_tok).astype(F32)
    placed = jnp.stack([pad_tok, jnp.zeros((r_rows,), F32)], axis=1).at[dest].set(
        jnp.stack([slot_tok, top_w.reshape(-1)], axis=1))
    src_tok = placed[:, 0].astype(jnp.int32)
    row_w = placed[:, 1]
    tile_ids = jnp.arange(n_tiles, dtype=jnp.int32)
    tile_expert = jnp.sum((tile_ids[:, None] >= jnp.cumsum(tiles_per)[None, :]).astype(jnp.int32), axis=1)
    n_used = jnp.sum(tiles_per).astype(jnp.int32)
    tile_expert = jnp.where(tile_ids < n_used, tile_expert, tile_expert[jnp.maximum(n_used - 1, 0)])
    tile_expert = jnp.minimum(tile_expert, n_experts - 1).astype(jnp.int32)
    return src_tok, row_w.reshape(r_rows, 1), tile_expert, n_used.reshape(1), dest.reshape(n_tok, TOP_K)


def moe_ffn_residual(x, h, logits, mod, gate_idx, w_g, w_u, w_d, layer_j, *, rows, seq, batch, final_g=None):
    d = x.shape[1]
    n_moe, n_experts, _, d_e = w_g.shape
    tmr = 512 if rows * TOP_K >= 8 * 512 * n_experts else 128
    tn = 512 if d_e % 512 == 0 and d >= 4096 else 256
    src_tok, row_w, tile_expert, n_used, dest = moe_route(logits, n_experts, tmr)
    xg = jnp.take(h, src_tok, axis=0, mode="clip")
    y = moe_experts(xg, row_w, tile_expert, n_used,
                    w_g.reshape(n_moe * n_experts, d, d_e), w_u.reshape(n_moe * n_experts, d, d_e),
                    w_d.reshape(n_moe * n_experts, d_e, d), layer_j * n_experts, tmr=tmr, tn=tn,
                    tn_down=2 * tn)
    y1 = jnp.take(y, dest[:, 0], axis=0, mode="clip")
    y2 = jnp.take(y, dest[:, 1], axis=0, mode="clip")
    tm = 256
    final = final_g is not None
    in_specs = [pl.BlockSpec((tm, d), lambda i: (i, 0)),
                pl.BlockSpec((tm, d), lambda i: (i, 0)),
                pl.BlockSpec((tm, d), lambda i: (i, 0)),
                pl.BlockSpec((None, 1, d), lambda i: (_mod_row(i, tm, seq, batch), 0, gate_idx))]
    args = [x, y1, y2, mod]
    if final:
        in_specs.append(pl.BlockSpec((1, d), lambda i: (0, 0)))
        args.append(final_g.reshape(1, d))
    return pl.pallas_call(
        functools.partial(_moe_combine_kernel, final=final),
        grid=(rows // tm,),
        in_specs=in_specs,
        out_specs=pl.BlockSpec((tm, d), lambda i: (i, 0)),
        out_shape=jax.ShapeDtypeStruct((rows, d), F32),
        compiler_params=_params(("parallel",), 10 * _nbytes((tm, d), F32)),
        name="moe_combine",
    )(*args)


def _rope_tables(seq, pad_rows):
    rows = seq // GRID_W
    half = HEAD_DIM // 2
    row = jnp.broadcast_to(jnp.arange(rows)[:, None], (rows, GRID_W)).reshape(-1).astype(F32)
    col = jnp.broadcast_to(jnp.arange(GRID_W)[None, :], (rows, GRID_W)).reshape(-1).astype(F32)
    inv_freq = ROPE_BASE ** (-jnp.arange(0, half, 2, dtype=F32) / half)
    ang_r = row[:, None] * inv_freq
    ang_c = col[:, None] * inv_freq
    ang = jnp.concatenate([ang_r, ang_r, ang_c, ang_c], axis=-1)
    cos, sin = jnp.cos(ang), jnp.sin(ang)
    first = (jnp.arange(HEAD_DIM) % half) < half // 2
    ones = jnp.ones((pad_rows, HEAD_DIM), F32)
    zeros = jnp.zeros((pad_rows, HEAD_DIM), F32)
    return (jnp.concatenate([cos, ones]), jnp.concatenate([jnp.where(first, -sin, 0.0), zeros]),
            jnp.concatenate([jnp.where(first, 0.0, sin), zeros]))


def kernel(x, c, ctx, c_ctx, w_mod, b_mod, g_attn_norm, w_in, attn_sink, ssm_a_re, ssm_a_im, ssm_log_dt,
           ssm_b_re, ssm_b_im, ssm_c_re, ssm_c_im, ssm_d, w_glu, g_out_attn, g_out_ssm, w_out, g_ffn_norm,
           w_ff_gate, w_ff_up, w_ff_down, w_router, w_exp_gate, w_exp_up, w_exp_down, g_final):
    batch, seq, d = x.shape
    n_ctx = ctx.shape[1]
    depth = w_in.shape[0]
    n_q = attn_sink.shape[1]
    n_kv = n_q // Q_PER_KV
    attn_w = n_q * HEAD_DIM
    kv_w = n_kv * HEAD_DIM
    rows_x = batch * seq
    rows_all = rows_x + batch * n_ctx
    big = rows_x >= 8192
    tm_big = 1024 if big else 256
    tm_mid = 512 if big else 256
    tn = 512 if big else 256
    geo = dict(seq=seq, batch=batch)

    xs = jnp.concatenate([x.reshape(rows_x, d), ctx.reshape(batch * n_ctx, d)], axis=0)
    cond = jnp.concatenate([c, c_ctx[None, :]], axis=0)
    rope_tabs = _rope_tables(seq, tm_big)
    rope_geo = dict(rows=rows_all, rows_x=rows_x, seq=seq, tm=tm_big, tn=tn)
    ssm_tables = jax.vmap(_ssm_tables)(ssm_a_re, ssm_a_im, ssm_log_dt, ssm_b_re, ssm_b_im, ssm_c_re, ssm_c_im)
    normed = False

    for i in range(depth):
        last = i == depth - 1
        rows_out = rows_x if last else rows_all
        mod = ada_modulation(cond, w_mod, b_mod, i)
        h = norm_modulate(xs, g_attn_norm[i], mod, 0, 1, rows=rows_all, **geo)
        w_in_b = w_in[i].astype(BF16)
        q_rot, q_plain = matmul_rope(h, w_in_b[:, :attn_w], rope_tabs, n_rope_tiles=attn_w // tn,
                                     scale=HEAD_DIM ** -0.5 * LOG2E, both=True, **rope_geo)
        kv = matmul_rope(h, w_in_b[:, attn_w:attn_w + 2 * kv_w], rope_tabs, n_rope_tiles=kv_w // tn,
                         scale=1.0, both=False, **rope_geo)
        u = matmul(h, w_in_b[:, attn_w + 2 * kv_w:], rows=rows_all, tm=tm_big, tn=tn, out_dtype=F32)
        o_attn = attention(q_rot, q_plain, kv, attn_sink[i] * LOG2E, g_out_attn[i], batch=batch, seq=seq, n_ctx=n_ctx,
                           n_kv=n_kv, ctx_queries=not last)
        y = s5_mix(u, ssm_tables, ssm_d[i], i, n_state=ssm_a_re.shape[-1], batch=batch, seq=seq, n_ctx=n_ctx)
        o_ssm = matmul_glu_norm(y, w_glu[i].astype(BF16), g_out_ssm[i], rows=rows_out, tm=tm_big)
        w_out_b = w_out[i].astype(BF16)
        xs = matmul_gated_residual([(o_attn, w_out_b[:attn_w]), (o_ssm, w_out_b[attn_w:])], xs, mod, 2,
                                   rows=rows_out, tm=tm_big, tn=tn, weight_outer=False, **geo)
        j = i // 2
        if i % 2 == 0:
            h2 = norm_modulate(xs, g_ffn_norm[i], mod, 3, 4, rows=rows_out, **geo)
            hid = matmul_swiglu(h2, w_ff_gate[j].astype(BF16), w_ff_up[j].astype(BF16),
                                rows=rows_out, tm=tm_big, tn=tn)
            xs = matmul_gated_residual([(hid, w_ff_down[j].astype(BF16))], xs, mod, 5, rows=rows_out, tm=tm_mid,
                                       tn=tn, weight_outer=True, **geo)
        else:
            n_e = w_router.shape[2]
            wr = jnp.zeros((d, LANES), F32).at[:, :n_e].set(w_router[j])
            h2, logits = norm_modulate(xs, g_ffn_norm[i], mod, 3, 4, rows=rows_out, w_router=wr, **geo)
            xs = moe_ffn_residual(xs, h2, logits, mod, 5, w_exp_gate, w_exp_up, w_exp_down, j,
                                  rows=rows_out, seq=seq, batch=batch, final_g=g_final if last else None)
            normed = last
    if not normed:
        xs = final_norm(xs, g_final, rows=rows_x)
    return xs.reshape(batch, seq, d)
```

```python
import functools
import math

import jax
import jax.numpy as jnp
from jax import lax
from jax.experimental import pallas as pl
from jax.experimental.pallas import tpu as pltpu

F32 = jnp.float32
BF16 = jnp.bfloat16

HEAD_DIM = 128
BLOCK = 128
Q_PER_KV = 3
GRID_W = 64
ROPE_BASE = 10000.0
SSM_GROUP = 16
N_MOD = 6
TOP_K = 2
EPS = 1e-6
NEG_INF = -1e30
LOG2E = math.log2(math.e)

LANES = 128
SUBLANES = 8
VMEM_BYTES_V7X = 64 * 1024 * 1024
VMEM_CAP = VMEM_BYTES_V7X - 8 * 1024 * 1024

SSM_CHUNK = 16


def _params(sem, est_bytes):
    limit = int(min(VMEM_CAP, max(32 * 1024 * 1024, est_bytes * 5 // 4)))
    return pltpu.CompilerParams(dimension_semantics=sem, vmem_limit_bytes=limit)


def _nbytes(shape, dtype):
    return math.prod(shape) * jnp.dtype(dtype).itemsize


def _mod_row(i, tm, seq, batch):
    return jnp.minimum((i * tm) // seq, batch)


def _mod_kernel(cb_ref, w_ref, b_ref, o_ref, *, n_rows, tn):
    k_dim = w_ref.shape[0]
    n_chunks = tn // LANES

    def body(kb, accs):
        k0 = pl.multiple_of(kb * SUBLANES, SUBLANES)
        wv = w_ref[pl.ds(k0, SUBLANES), :]
        new = []
        for r in range(n_rows):
            cv = cb_ref[r, pl.ds(k0, SUBLANES), :]
            sv = cv * jax.nn.sigmoid(cv)
            for c in range(n_chunks):
                new.append(accs[r * n_chunks + c] + wv[:, c * LANES:(c + 1) * LANES] * sv)
        return tuple(new)

    init = tuple(jnp.zeros((SUBLANES, LANES), F32) for _ in range(n_rows * n_chunks))
    accs = lax.fori_loop(0, k_dim // SUBLANES, body, init, unroll=4)
    o_ref[...] = jnp.zeros_like(o_ref)
    for r in range(n_rows):
        row = jnp.concatenate(
            [jnp.sum(accs[r * n_chunks + c], axis=0, keepdims=True) for c in range(n_chunks)], axis=1)
        o_ref[r:r + 1, :] = row + b_ref[...]


def ada_modulation(cond_rows, w_mod, b_mod, layer):
    n_rows, d = cond_rows.shape
    depth, _, n = w_mod.shape
    tn = 1024
    cb =jnp.broadcast_to(cond_rows[:, :, None], (n_rows, d, LANES))
    est = 2 * (_nbytes((d, tn), F32) + _nbytes((n_rows, d, LANES), F32))
    out = pl.pallas_call(
        functools.partial(_mod_kernel, n_rows=n_rows, tn=tn),
        grid=(n // tn,),
        in_specs=[pl.BlockSpec((n_rows, d, LANES), lambda j: (0, 0, 0)),
                  pl.BlockSpec((None, d, tn), lambda j: (layer, 0, j)),
                  pl.BlockSpec((None, 1, tn), lambda j: (layer, 0, j))],
        out_specs=pl.BlockSpec((SUBLANES, tn), lambda j: (0, j)),
        out_shape=jax.ShapeDtypeStruct((SUBLANES, n), F32),
        compiler_params=_params(("parallel",), est),
        name="ada_modulation",
    )(cb, w_mod, b_mod.reshape(depth, 1, n))
    return out.reshape(SUBLANES, 1, n)


def _split_bf16(v):
    hi = v.astype(BF16)
    return hi, (v - hi.astype(F32)).astype(BF16)


def _normmod_kernel(x_ref, g_ref, shift_ref, scale_ref, *rest, with_router):
    xf = x_ref[...]
    y = xf * lax.rsqrt(jnp.mean(xf * xf, axis=-1, keepdims=True) + EPS)
    h = (y * g_ref[...]) * (1.0 + scale_ref[...]) + shift_ref[...]
    if with_router:
        wr_ref, o_ref, lg_ref = rest
        hs = _split_bf16(h)
        ws = _split_bf16(wr_ref[...])
        acc = None
        for a, b in ((0, 0), (0, 1), (1, 0)):
            t = jnp.dot(hs[a], ws[b], preferred_element_type=F32)
            acc = t if acc is None else acc + t
        lg_ref[...] = acc
    else:
        (o_ref,) = rest
    o_ref[...] = h.astype(BF16)


def norm_modulate(x, g, mod, shift_idx, scale_idx, *, rows, seq, batch, w_router=None):
    d = x.shape[1]
    tm = 256
    with_router = w_router is not None
    in_specs = [pl.BlockSpec((tm, d), lambda i: (i, 0)),
                pl.BlockSpec((1, d), lambda i: (0, 0)),
                pl.BlockSpec((None, 1, d), lambda i: (_mod_row(i, tm, seq, batch), 0, shift_idx)),
                pl.BlockSpec((None, 1, d), lambda i: (_mod_row(i, tm, seq, batch), 0, scale_idx))]
    args = [x, g.reshape(1, d), mod, mod]
    out_specs = [pl.BlockSpec((tm, d), lambda i: (i, 0))]
    out_shape = [jax.ShapeDtypeStruct((rows, d), BF16)]
    if with_router:
        in_specs.append(pl.BlockSpec((d, LANES), lambda i: (0, 0)))
        args.append(w_router)
        out_specs.append(pl.BlockSpec((tm, LANES), lambda i: (i, 0)))
        out_shape.append(jax.ShapeDtypeStruct((rows, LANES), F32))
    est = 2 * (_nbytes((tm, d), F32) + _nbytes((tm, d), BF16)) + 8 * _nbytes((tm, d), F32)
    outs = pl.pallas_call(
        functools.partial(_normmod_kernel, with_router=with_router),
        grid=(pl.cdiv(rows, tm),),
        in_specs=in_specs, out_specs=out_specs, out_shape=out_shape,
        compiler_params=_params(("parallel",), est),
        name="norm_modulate_router" if with_router else "norm_modulate",
    )(*args)
    return outs if with_router else outs[0]


def _mm_kernel(x_ref, w_ref, o_ref):
    o_ref[...] = jnp.dot(x_ref[...], w_ref[...], preferred_element_type=F32).astype(o_ref.dtype)


def _mm_swiglu_kernel(x_ref, wg_ref, wu_ref, o_ref):
    x = x_ref[...]
    a = jnp.dot(x, wg_ref[...], preferred_element_type=F32)
    b = jnp.dot(x, wu_ref[...], preferred_element_type=F32)
    o_ref[...] = (a * jax.nn.sigmoid(a) * b).astype(o_ref.dtype)


def _rms(x, g):
    return x * lax.rsqrt(jnp.mean(x * x, axis=-1, keepdims=True) + EPS) * g


def _mm_resid_kernel(*refs, n_parts):
    x_refs, w_refs = refs[:n_parts], refs[n_parts:2 * n_parts]
    r_ref, gate_ref, o_ref = refs[2 * n_parts:]
    acc = None
    for x_ref, w_ref in zip(x_refs, w_refs):
        t = jnp.dot(x_ref[...], w_ref[...], preferred_element_type=F32)
        acc = t if acc is None else acc + t
    o_ref[...] = r_ref[...] + gate_ref[...] * acc


def _mm_glu_norm_kernel(y_ref, w_ref, g_ref, o_ref):
    y = y_ref[...]
    z = jnp.dot(y.astype(BF16), w_ref[...], preferred_element_type=F32)
    o_ref[...] = _rms(y * jax.nn.sigmoid(z), g_ref[...]).astype(BF16)


def matmul(x, w, *, rows, tm, tn, out_dtype):
    k, n = w.shape
    est = 2 * (_nbytes((tm, k), x.dtype) + _nbytes((k, tn), w.dtype) + _nbytes((tm, tn), out_dtype))
    return pl.pallas_call(
        _mm_kernel,
        grid=(pl.cdiv(rows, tm), pl.cdiv(n, tn)),
        in_specs=[pl.BlockSpec((tm, k), lambda i, j: (i, 0)),
                  pl.BlockSpec((k, tn), lambda i, j: (0, j))],
        out_specs=pl.BlockSpec((tm, tn), lambda i, j: (i, j)),
        out_shape=jax.ShapeDtypeStruct((rows, n), out_dtype),
        compiler_params=_params(("parallel", "parallel"), est),
        name="matmul",
    )(x, w)


def _rope(x, cos, sin_lo, sin_hi):
    return (x * cos + pltpu.roll(x, HEAD_DIM - HEAD_DIM // 4, 1) * sin_lo
            + pltpu.roll(x, HEAD_DIM // 4, 1) * sin_hi)


def _mm_rope_kernel(x_ref, w_ref, cos_ref, slo_ref, shi_ref, *o_refs, n_rope_tiles, scale, both, sub):
    j = pl.program_id(1)
    tm, tn = o_refs[0].shape
    for r0 in range(0, tm, sub):
        acc = jnp.dot(x_ref[r0:r0 + sub, :], w_ref[...], preferred_element_type=F32)
        if scale != 1.0:
            acc = acc * scale
        cos, slo, shi = (r[r0:r0 + sub, :] for r in (cos_ref, slo_ref, shi_ref))
        rot = jnp.concatenate([_rope(acc[:, h * HEAD_DIM:(h + 1) * HEAD_DIM], cos, slo, shi)
                               for h in range(tn // HEAD_DIM)], axis=1)
        if both:
            o_refs[0][r0:r0 + sub, :] = rot.astype(BF16)
            o_refs[1][r0:r0 + sub, :] = acc.astype(BF16)
        else:
            o_refs[0][r0:r0 + sub, :] = jnp.where(j < n_rope_tiles, rot, acc).astype(BF16)


def matmul_rope(x, w, tabs, *, rows, rows_x, seq, tm, tn, n_rope_tiles, scale, both):
    k, n = w.shape
    per_seq = seq // tm

    def tab_map(i, j):
        return (jnp.where(i * tm < rows_x, i % per_seq, per_seq), 0)

    n_out = 2 if both else 1
    est = 2 * (_nbytes((tm, k), BF16) + _nbytes((k, tn), BF16) + n_out * _nbytes((tm, tn), BF16)
               + 3 * _nbytes((tm, HEAD_DIM), F32)) + 4 * _nbytes((tm, tn), F32)
    out = pl.pallas_call(
        functools.partial(_mm_rope_kernel, n_rope_tiles=n_rope_tiles, scale=scale, both=both, sub=min(tm, 256)),
        grid=(pl.cdiv(rows, tm), n // tn),
        in_specs=[pl.BlockSpec((tm, k), lambda i, j: (i, 0)),
                  pl.BlockSpec((k, tn), lambda i, j: (0, j))] + [pl.BlockSpec((tm, HEAD_DIM), tab_map)] * 3,
        out_specs=[pl.BlockSpec((tm, tn), lambda i, j: (i, j))] * n_out,
        out_shape=[jax.ShapeDtypeStruct((rows, n), BF16)] * n_out,
        compiler_params=_params(("parallel", "parallel"), est),
        name="matmul_rope",
    )(x, w, *tabs)
    return out if both else out[0]


def matmul_swiglu(x, wg, wu, *, rows, tm, tn):
    k, n = wg.shape
    est = 2 * (_nbytes((tm, k), BF16) + 2 * _nbytes((k, tn), BF16) + _nbytes((tm, tn), BF16)) \
        + 3 * _nbytes((tm, tn), F32)
    return pl.pallas_call(
        _mm_swiglu_kernel,
        grid=(pl.cdiv(rows, tm), pl.cdiv(n, tn)),
        in_specs=[pl.BlockSpec((tm, k), lambda i, j: (i, 0)),
                  pl.BlockSpec((k, tn), lambda i, j: (0, j)),
                  pl.BlockSpec((k, tn), lambda i, j: (0, j))],
        out_specs=pl.BlockSpec((tm, tn), lambda i, j: (i, j)),
        out_shape=jax.ShapeDtypeStruct((rows, n), BF16),
        compiler_params=_params(("parallel", "parallel"), est),
        name="matmul_swiglu",
    )(x, wg, wu)


def matmul_gated_residual(xw, resid, mod, gate_idx, *, rows, tm, tn, seq, batch, weight_outer):
    xs_, ws_ = [p[0] for p in xw], [p[1] for p in xw]
    n = ws_[0].shape[1]
    k = sum(w.shape[0] for w in ws_)
    nj = n // tn
    if weight_outer:
        grid = (nj, pl.cdiv(rows, tm))
        ij = lambda a, b: (b, a)
    else:
        grid = (pl.cdiv(rows, tm), nj)
        ij = lambda a, b: (a, b)

    def gate_map(a, b):
        i, j = ij(a, b)
        return (_mod_row(i, tm, seq, batch), 0, gate_idx * nj + j)

    est = 2 * (_nbytes((tm, k), BF16) + _nbytes((k, tn), BF16) + 2 * _nbytes((tm, tn), F32))
    return pl.pallas_call(
        functools.partial(_mm_resid_kernel, n_parts=len(xw)),
        grid=grid,
        in_specs=[pl.BlockSpec((tm, x.shape[1]), lambda a, b: (ij(a, b)[0], 0)) for x in xs_]
        + [pl.BlockSpec((w.shape[0], tn), lambda a, b: (0, ij(a, b)[1])) for w in ws_]
        + [pl.BlockSpec((tm, tn), lambda a, b: ij(a, b)),
           pl.BlockSpec((None, 1, tn), gate_map)],
        out_specs=pl.BlockSpec((tm, tn), lambda a, b: ij(a, b)),
        out_shape=jax.ShapeDtypeStruct((rows, n), F32),
        compiler_params=_params(("parallel", "parallel"), est),
        name="matmul_gated_residual",
    )(*xs_, *ws_, resid, mod)


def matmul_glu_norm(y, w, g, *, rows, tm):
    k, n = w.shape
    est = 2 * (_nbytes((tm, k), F32) + _nbytes((k, n), BF16) + _nbytes((tm, n), BF16)) + 4 * _nbytes((tm, n), F32)
    return pl.pallas_call(
        _mm_glu_norm_kernel,
        grid=(pl.cdiv(rows, tm),),
        in_specs=[pl.BlockSpec((tm, k), lambda i: (i, 0)),
                  pl.BlockSpec((k, n), lambda i: (0, 0)),
                  pl.BlockSpec((1, n), lambda i: (0, 0))],
        out_specs=pl.BlockSpec((tm, n), lambda i: (i, 0)),
        out_shape=jax.ShapeDtypeStruct((rows, n), BF16),
        compiler_params=_params(("parallel",), est),
        name="matmul_glu_norm",
    )(y, w, g.reshape(1, n))


def _nt_dot(a, b):
    return lax.dot_general(a, b, (((1,), (1,)), ((), ())), preferred_element_type=F32)


def _attn_kernel(sink_ref, qr_ref, qp_ref, kp_ref, k0_ref, kn_ref, vp_ref, v0_ref, vn_ref, kc_ref, vc_ref,
                 bias_ref, g_ref, o_ref, o_scr, *, n_blocks, ctx_queries, n_kv):
    n = pl.program_id(1)
    qw = Q_PER_KV * HEAD_DIM

    def stack_heads(ref, hh):
        return jnp.concatenate([ref[:, hh * qw + g * HEAD_DIM:hh * qw + (g + 1) * HEAD_DIM]
                                for g in range(Q_PER_KV)], axis=0)

    def head(ref, hh):
        return ref[:, hh * HEAD_DIM:(hh + 1) * HEAD_DIM]

    def run_head(window, hh):
        sink = jnp.concatenate(
            [jnp.full((BLOCK, 1), sink_ref[hh * Q_PER_KV + g], F32) for g in range(Q_PER_KV)], axis=0)
        s = _nt_dot(stack_heads(qp_ref, hh), head(kc_ref, hh))
        v = head(vc_ref, hh)
        if window:
            k_win = jnp.concatenate([head(kp_ref, hh), head(k0_ref, hh), head(kn_ref, hh)], axis=0)
            s_w = _nt_dot(stack_heads(qr_ref, hh), k_win) + bias_ref[...]
            s = jnp.concatenate([s_w, s], axis=1)
            v = jnp.concatenate([head(vp_ref, hh), head(v0_ref, hh), head(vn_ref, hh), v], axis=0)
        n_keys = s.shape[1]
        m = jnp.maximum(jnp.max(s, axis=1, keepdims=True), sink)
        m_b = jnp.broadcast_to(m, (Q_PER_KV * BLOCK, LANES))
        sink_b = jnp.broadcast_to(sink, (Q_PER_KV * BLOCK, LANES))
        p = jnp.exp2(s - jnp.concatenate([m_b] * (n_keys // LANES), axis=1)).astype(BF16)
        v_ext = jnp.concatenate([v, jnp.ones((n_keys, HEAD_DIM), BF16)], axis=1)
        o_ext = jnp.dot(p, v_ext, preferred_element_type=F32)
        o = o_ext[:, :HEAD_DIM] / (o_ext[:, HEAD_DIM:] + jnp.exp2(sink_b - m_b))
        for g in range(Q_PER_KV):
            o_scr[:, hh * qw + g * HEAD_DIM:hh * qw + (g + 1) * HEAD_DIM] = o[g * BLOCK:(g + 1) * BLOCK]

    def run(window):
        for hh in range(n_kv):
            run_head(window, hh)
        o_ref[...] = _rms(o_scr[...], g_ref[...]).astype(BF16)

    if ctx_queries:
        pl.when(n < n_blocks)(lambda: run(True))
        pl.when(n >= n_blocks)(lambda: run(False))
    else:
        run(True)


def _window_bias(n_blocks):
    rows = Q_PER_KV * BLOCK
    row = (jnp.arange(rows) % BLOCK)[:, None]
    col = jnp.arange(3 * BLOCK)[None, :]
    band = (col >= row) & (col <= row + 2 * BLOCK)
    variants = []
    for code in range(4):
        ok = band
        if code & 1:
            ok = ok & (col >= BLOCK)
        if code & 2:
            ok = ok & (col < 2 * BLOCK)
        variants.append(jnp.where(ok, 0.0, NEG_INF).astype(F32))
    return jnp.stack(variants)


def attention(q_rot, q_plain, kv, sink, g_out, *, batch, seq, n_ctx, n_kv, ctx_queries):
    nb = seq // BLOCK
    qb = n_ctx // BLOCK if ctx_queries else 0
    qw = n_kv * Q_PER_KV * HEAD_DIM
    kw = n_kv * HEAD_DIM
    rows_out = batch * seq + (batch * n_ctx if ctx_queries else 0)
    ctx_blk0 = batch * seq // n_ctx
    q_blk0 = batch * seq // BLOCK

    def q_map(b, n, s):
        return (jnp.where(n < nb, b * nb + n, q_blk0 + b * qb + (n - nb)), 0)

    def kv_spec(col, shift):
        return pl.BlockSpec((BLOCK, kw), lambda b, n, s: (b * nb + jnp.clip(n + shift, 0, nb - 1), col))

    def bias_map(b, n, s):
        return ((n == 0).astype(jnp.int32) + 2 * (n == nb - 1).astype(jnp.int32), 0, 0)

    in_specs = [pl.BlockSpec((BLOCK, qw), q_map), pl.BlockSpec((BLOCK, qw), q_map),
                kv_spec(0, -1), kv_spec(0, 0), kv_spec(0, 1),
                kv_spec(1, -1), kv_spec(1, 0), kv_spec(1, 1),
                pl.BlockSpec((n_ctx, kw), lambda b, n, s: (ctx_blk0 + b, 0)),
                pl.BlockSpec((n_ctx, kw), lambda b, n, s: (ctx_blk0 + b, 1)),
                pl.BlockSpec((None, Q_PER_KV * BLOCK, 3 * BLOCK), bias_map),
                pl.BlockSpec((1, qw), lambda b, n, s: (0, 0))]
    grid_spec = pltpu.PrefetchScalarGridSpec(
        num_scalar_prefetch=1, grid=(batch, nb + qb), in_specs=in_specs,
        out_specs=pl.BlockSpec((BLOCK, qw), q_map),
        scratch_shapes=[pltpu.VMEM((BLOCK, qw), F32)])
    return pl.pallas_call(
        functools.partial(_attn_kernel, n_blocks=nb, ctx_queries=ctx_queries, n_kv=n_kv),
        grid_spec=grid_spec,
        out_shape=jax.ShapeDtypeStruct((rows_out, qw), BF16),
        compiler_params=_params(("parallel", "parallel"), 16 * 1024 * 1024),
        name="attention",
    )(sink, q_rot, q_plain, *([kv] * 8), _window_bias(nb), g_out.reshape(1, qw))


def _ssm_tables(a_re, a_im, log_dt, b_re, b_im, c_re, c_im):
    hp = lax.Precision.HIGHEST
    t_len = SSM_CHUNK
    g_per = LANES // SSM_GROUP
    n_groups, n_state = a_re.shape[1:]
    n_lb = n_groups // g_per
    dt = jnp.exp(log_dt.astype(F32))[..., None]
    lam_re, lam_im = a_re.astype(F32), a_im.astype(F32)
    mag = jnp.exp(lam_re * dt)
    abar = lax.complex(mag * jnp.cos(lam_im * dt), mag * jnp.sin(lam_im * dt))
    lam = lax.complex(lam_re, lam_im)
    bbar = ((abar - 1.0) / lam)[..., None] * lax.complex(b_re.astype(F32), b_im.astype(F32))
    cmat = lax.complex(c_re.astype(F32), c_im.astype(F32))

    def powers(d, exps):
        e = jnp.asarray(exps, F32)[:, None, None]
        m = jnp.exp(lam_re[d] * dt[d] * e)
        ph = lam_im[d] * dt[d] * e
        return lax.complex(m * jnp.cos(ph), m * jnp.sin(ph))

    ar = list(range(t_len + 1))
    pw_f, pw_b = powers(0, ar), powers(1, ar)

    kf = jnp.real(jnp.einsum('gcp,tgp,gpi->tgci', cmat[0], pw_f[:t_len], bbar[0], precision=hp))
    kb = jnp.real(jnp.einsum('gcp,tgp,gpi->tgci', cmat[1], pw_b[:t_len], bbar[1], precision=hp))
    lag = jnp.arange(2 * t_len - 1) - (t_len - 1)
    k_lag = jnp.where((lag >= 0)[:, None, None, None], kf[jnp.clip(lag, 0)], 0.0) \
        + jnp.where((lag <= 0)[:, None, None, None], kb[jnp.clip(-lag, 0)], 0.0)
    k_lag = jnp.swapaxes(k_lag, -1, -2).reshape(2 * t_len - 1, n_lb, g_per, SSM_GROUP, SSM_GROUP)
    eye = jnp.eye(g_per, dtype=F32)
    k_lag = jnp.einsum('mngab,gh->mngahb', k_lag, eye).reshape(2 * t_len - 1, n_lb, LANES, LANES)

    rep = LANES // n_state

    def e_coef(d, pw_sel):
        pw_rep = jnp.concatenate([pw_sel] * rep, axis=-1)
        bb_rep = jnp.concatenate([jnp.swapaxes(bbar[d], -1, -2)] * rep, axis=-1)
        return pw_rep[:, :, None, :] * bb_rep[None]

    ef, eb = e_coef(0, powers(0, [t_len - 1 - j for j in range(t_len)])), e_coef(1, pw_b[:t_len])
    quarters = jnp.stack([jnp.real(ef), jnp.imag(ef), jnp.real(eb), jnp.imag(eb)], axis=-2)
    e_c = quarters.reshape(t_len, n_lb, LANES, 4 * LANES)

    def f_coef(d, pw_sel):
        cm = cmat[d].reshape(n_lb, g_per, SSM_GROUP, n_state).transpose(0, 3, 1, 2)
        pw = pw_sel.reshape(t_len, n_lb, g_per, n_state).transpose(1, 3, 0, 2)
        return cm[:, :, None, :, :] * pw[..., None]

    ff, fb = f_coef(0, pw_f[1:]), f_coef(1, powers(1, [t_len - l for l in range(t_len)]))
    f_c = jnp.stack([jnp.real(ff), -jnp.imag(ff), jnp.real(fb), -jnp.imag(fb)], axis=1)
    f_c = f_c.reshape(n_lb, 4 * n_state * t_len, LANES)

    def lay(v):
        return v.reshape(n_lb, g_per * n_state)

    a_chunk = jnp.concatenate([lay(jnp.real(pw_f[t_len])), lay(jnp.imag(pw_f[t_len])),
                               lay(jnp.real(pw_b[t_len])), lay(jnp.imag(pw_b[t_len]))], axis=-1)
    return k_lag.astype(BF16), e_c, f_c, a_chunk.reshape(1, -1)


def _chunk_inputs(x_ref, tmc):
    return [x_ref[pl.ds(j, tmc, stride=SSM_CHUNK), :] for j in range(SSM_CHUNK)]


def _ssm_local_kernel(x_ref, ec_ref, s_ref, e_scr, *, n_state):
    tmc = s_ref.shape[0]
    qw = e_scr.shape[1] // 4

    @pl.when(pl.program_id(1) == 0)
    def _():
        row_g = lax.broadcasted_iota(jnp.int32, (LANES, qw), 0) // SSM_GROUP
        col_g = lax.broadcasted_iota(jnp.int32, (LANES, qw), 1) // n_state
        same = row_g == col_g
        for j in range(SSM_CHUNK):
            for q in range(4):
                v = ec_ref[j, :, q * LANES:(q + 1) * LANES]
                tiled = jnp.concatenate([v] * (qw // LANES), axis=1)
                e_scr[j * LANES:(j + 1) * LANES, q * qw:(q + 1) * qw] = jnp.where(same, tiled, 0.0).astype(BF16)

    xs = jnp.concatenate([x.astype(BF16) for x in _chunk_inputs(x_ref, tmc)], axis=1)
    s_ref[...] = jnp.dot(xs, e_scr[...], preferred_element_type=F32)


def _ssm_scan_kernel(s_ref, a_ref, h_ref, *, batch, nc_x, nc_c):
    sw = s_ref.shape[1] // 4
    a = a_ref[...]
    afr, afi, abr, abi = (a[:, k * sw:(k + 1) * sw] for k in range(4))
    ctx0 = batch * nc_x

    def step(row_f, row_b, st):
        new = []
        for (row, ar, ai, off, (hr, hi)) in ((row_f, afr, afi, 0, st[0]), (row_b, abr, abi, 2 * sw, st[1])):
            h_ref[pl.ds(row, 1), off:off + sw] = hr
            h_ref[pl.ds(row, 1), off + sw:off + 2 * sw] = hi
            sr = s_ref[pl.ds(row, 1), off:off + sw]
            si = s_ref[pl.ds(row, 1), off + sw:off + 2 * sw]
            new.append((ar * hr - ai * hi + sr, ar * hi + ai * hr + si))
        return tuple(new)

    zero = jnp.zeros((1, sw), F32)
    for b in range(batch):
        st = ((zero, zero), (zero, zero))
        c_base = ctx0 + b * nc_c
        st = lax.fori_loop(0, nc_c, lambda t, s: step(c_base + t, c_base + nc_c - 1 - t, s), st)
        x_base = b * nc_x
        lax.fori_loop(0, nc_x, lambda t, s: step(x_base + t, x_base + nc_x - 1 - t, s), st)


def _ssm_out_kernel(x_ref, h_ref, kl_ref, fc_ref, d_ref, o_ref, w_scr, *, n_state):
    t = SSM_CHUNK
    tmc = h_ref.shape[0]
    sdim = h_ref.shape[1]
    qw = sdim // 4

    @pl.when(pl.program_id(1) == 0)
    def _():
        for j in range(t):
            for l in range(t):
                w_scr[j * LANES:(j + 1) * LANES, l * LANES:(l + 1) * LANES] = kl_ref[l - j + t - 1]
        lane_g = lax.broadcasted_iota(jnp.int32, (n_state, LANES), 1) // SSM_GROUP
        for q in range(4):
            for l in range(t):
                v = fc_ref[pl.ds(q * n_state * t + l, n_state, stride=t), :]
                for g in range(qw // n_state):
                    r0 = t * LANES + q * qw + g * n_state
                    w_scr[r0:r0 + n_state, l * LANES:(l + 1) * LANES] = \
                        jnp.where(lane_g == g, v, 0.0).astype(BF16)

    xs = _chunk_inputs(x_ref, tmc)
    lhs = jnp.concatenate([x.astype(BF16) for x in xs] + [h_ref[...].astype(BF16)], axis=1)
    y = jnp.dot(lhs, w_scr[...], preferred_element_type=F32)
    d = d_ref[...]
    for l in range(t):
        o_ref[pl.ds(l, tmc, stride=t), :] = jax.nn.gelu(y[:, l * LANES:(l + 1) * LANES] + d * xs[l])


def s5_mix(u, tables, d_skip, layer, *, n_state, batch, seq, n_ctx):
    k_lag, e_c, f_c, a_chunk = tables
    rows, ssm_width = u.shape
    t = SSM_CHUNK
    n_lb = ssm_width // LANES
    rc = rows // t
    sdim = 4 * (LANES // SSM_GROUP) * n_state
    tmc = max(m for m in range(SUBLANES, min(rc, 264) + 1, SUBLANES) if rc % m == 0)
    x_spec = pl.BlockSpec((tmc * t, LANES), lambda g, r: (r, g))
    x_bytes = _nbytes((tmc * t, LANES), F32)

    s_loc = pl.pallas_call(
        functools.partial(_ssm_local_kernel, n_state=n_state),
        grid=(n_lb, rc // tmc),
        in_specs=[x_spec, pl.BlockSpec((None, t, None, LANES, 4 * LANES), lambda g, r: (layer, 0, g, 0, 0))],
        out_specs=pl.BlockSpec((tmc, sdim), lambda g, r: (r, g)),
        out_shape=jax.ShapeDtypeStruct((rc, n_lb * sdim), F32),
        scratch_shapes=[pltpu.VMEM((t * LANES, sdim), BF16)],
        compiler_params=_params(("arbitrary", "arbitrary"),
                                2 * x_bytes + 2 * _nbytes((t, LANES, 4 * LANES), F32)
                                + _nbytes((t * LANES, sdim), BF16) + 4 * _nbytes((tmc, sdim), F32)),
        name="s5_local_states",
    )(u, e_c)

    h_in = pl.pallas_call(
        functools.partial(_ssm_scan_kernel, batch=batch, nc_x=seq // t, nc_c=n_ctx // t),
        grid=(n_lb,),
        in_specs=[pl.BlockSpec((rc, sdim), lambda g: (0, g)),
                  pl.BlockSpec((None, 1, sdim), lambda g: (layer, 0, g))],
        out_specs=pl.BlockSpec((rc, sdim), lambda g: (0, g)),
        out_shape=jax.ShapeDtypeStruct((rc, n_lb * sdim), F32),
        compiler_params=_params(("parallel",), 4 * _nbytes((rc, sdim), F32)),
        name="s5_chunk_scan",
    )(s_loc, a_chunk)

    return pl.pallas_call(
        functools.partial(_ssm_out_kernel, n_state=n_state),
        grid=(n_lb, rc // tmc),
        in_specs=[x_spec,
                  pl.BlockSpec((tmc, sdim), lambda g, r: (r, g)),
                  pl.BlockSpec((None, 2 * t - 1, None, LANES, LANES), lambda g, r: (layer, 0, g, 0, 0)),
                  pl.BlockSpec((None, None, 4 * n_state * t, LANES), lambda g, r: (layer, g, 0, 0)),
                  pl.BlockSpec((1, LANES), lambda g, r: (0, g))],
        out_specs=x_spec,
        out_shape=jax.ShapeDtypeStruct((rows, ssm_width), F32),
        scratch_shapes=[pltpu.VMEM((t * LANES + sdim, t * LANES), BF16)],
        compiler_params=_params(("arbitrary", "arbitrary"),
                                4 * x_bytes + 2 * _nbytes((tmc, sdim), F32)
                                + 2 * _nbytes((2 * t - 1, LANES, LANES), BF16)
                                + 2 * _nbytes((4 * n_state, t * LANES), F32)
                                + _nbytes((t * LANES + sdim, t * LANES), BF16) + 4 * _nbytes((tmc, sdim), F32)),
        name="s5_readout",
    )(u, h_in, k_lag, f_c, d_skip.reshape(1, ssm_width))


def _final_norm_kernel(x_ref, g_ref, o_ref):
    o_ref[...] = _rms(x_ref[...], g_ref[...])


def final_norm(x, g, *, rows):
    d = x.shape[1]
    tm = 256
    return pl.pallas_call(
        _final_norm_kernel,
        grid=(pl.cdiv(rows, tm),),
        in_specs=[pl.BlockSpec((tm, d), lambda i: (i, 0)), pl.BlockSpec((1, d), lambda i: (0, 0))],
        out_specs=pl.BlockSpec((tm, d), lambda i: (i, 0)),
        out_shape=jax.ShapeDtypeStruct((rows, d), F32),
        compiler_params=_params(("parallel",), 6 * _nbytes((tm, d), F32)),
        name="final_norm",
    )(x, g.reshape(1, d))


def _new_expert(te_ref, t):
    return jnp.logical_or(t == 0, te_ref[t] != te_ref[jnp.maximum(t - 1, 0)])


def _moe_up_kernel(te_ref, nu_ref, x_ref, wg_ref, wu_ref, o_ref, wg_scr, wu_scr):
    t = pl.program_id(1)

    @pl.when(_new_expert(te_ref, t))
    def _():
        wg_scr[...] = wg_ref[...].astype(BF16)
        wu_scr[...] = wu_ref[...].astype(BF16)

    @pl.when(t < nu_ref[0])
    def _():
        x = x_ref[...]
        a = jnp.dot(x, wg_scr[...], preferred_element_type=F32)
        b = jnp.dot(x, wu_scr[...], preferred_element_type=F32)
        o_ref[...] = (a * jax.nn.sigmoid(a) * b).astype(o_ref.dtype)

    @pl.when(t >= nu_ref[0])
    def _():
        o_ref[...] = jnp.zeros_like(o_ref)


def _moe_down_kernel(te_ref, nu_ref, h_ref, wd_ref, rw_ref, o_ref, wd_scr):
    t = pl.program_id(1)

    @pl.when(_new_expert(te_ref, t))
    def _():
        wd_scr[...] = wd_ref[...].astype(BF16)

    @pl.when(t < nu_ref[0])
    def _():
        acc = jnp.dot(h_ref[...], wd_scr[...], preferred_element_type=F32)
        o_ref[...] = (acc * rw_ref[...]).astype(o_ref.dtype)

    @pl.when(t >= nu_ref[0])
    def _():
        o_ref[...] = jnp.zeros_like(o_ref)


def moe_experts(xg, row_w, tile_expert, n_used, w_g, w_u, w_d, e_base, *, tmr, tn, tn_down):
    r_rows, d = xg.shape
    d_e = w_g.shape[2]
    n_tiles = r_rows // tmr
    est = 2 * (_nbytes((tmr, d), BF16) + 2 * _nbytes((d, tn), F32) + _nbytes((tmr, tn), BF16)) \
        + 2 * _nbytes((d, tn), BF16) + 3 * _nbytes((tmr, tn), F32)
    hidden = pl.pallas_call(
        _moe_up_kernel,
        grid_spec=pltpu.PrefetchScalarGridSpec(
            num_scalar_prefetch=2, grid=(d_e // tn, n_tiles),
            in_specs=[pl.BlockSpec((tmr, d), lambda j, t, te, nu: (t, 0)),
                      pl.BlockSpec((None, d, tn), lambda j, t, te, nu: (e_base + te[t], 0, j)),
                      pl.BlockSpec((None, d, tn), lambda j, t, te, nu: (e_base + te[t], 0, j))],
            out_specs=pl.BlockSpec((tmr, tn), lambda j, t, te, nu: (t, j)),
            scratch_shapes=[pltpu.VMEM((d, tn), BF16), pltpu.VMEM((d, tn), BF16)]),
        out_shape=jax.ShapeDtypeStruct((r_rows, d_e), BF16),
        compiler_params=_params(("arbitrary", "arbitrary"), est),
        name="moe_gate_up",
    )(tile_expert, n_used, xg, w_g, w_u)
    tn = tn_down
    est = 2 * (_nbytes((tmr, d_e), BF16) + _nbytes((d_e, tn), F32) + _nbytes((tmr, tn), BF16)) \
        + _nbytes((d_e, tn), BF16) + 2 * _nbytes((tmr, tn), F32)
    return pl.pallas_call(
        _moe_down_kernel,
        grid_spec=pltpu.PrefetchScalarGridSpec(
            num_scalar_prefetch=2, grid=(d // tn, n_tiles),
            in_specs=[pl.BlockSpec((tmr, d_e), lambda j, t, te, nu: (t, 0)),
                      pl.BlockSpec((None, d_e, tn), lambda j, t, te, nu: (e_base + te[t], 0, j)),
                      pl.BlockSpec((tmr, 1), lambda j, t, te, nu: (t, 0))],
            out_specs=pl.BlockSpec((tmr, tn), lambda j, t, te, nu: (t, j)),
            scratch_shapes=[pltpu.VMEM((d_e, tn), BF16)]),
        out_shape=jax.ShapeDtypeStruct((r_rows, d), BF16),
        compiler_params=_params(("arbitrary", "arbitrary"), est),
        name="moe_down",
    )(tile_expert, n_used, hidden, w_d, row_w)


def _moe_combine_kernel(x_ref, y1_ref, y2_ref, gate_ref, *rest, final):
    out = x_ref[...] + gate_ref[...] * (y1_ref[...].astype(F32) + y2_ref[...].astype(F32))
    if final:
        g_ref, o_ref = rest
        o_ref[...] = _rms(out, g_ref[...])
    else:
        (o_ref,) = rest
        o_ref[...] = out


def moe_route(logits, n_experts, tmr):
    n_tok = logits.shape[0]
    lg = logits[:, :n_experts]
    top_v, top_i = lax.top_k(lg, TOP_K)
    top_w = jax.nn.softmax(top_v, axis=-1)
    flat_e = top_i.reshape(-1)
    onehot = (flat_e[:, None] == jnp.arange(n_experts)[None, :]).astype(jnp.int32)
    rank = jnp.take_along_axis(jnp.cumsum(onehot, axis=0) - onehot, flat_e[:, None], axis=1)[:, 0]
    counts = jnp.sum(onehot, axis=0)
    tiles_per = (counts + tmr - 1) // tmr
    tile_start = jnp.cumsum(tiles_per) - tiles_per
    dest = tile_start[flat_e] * tmr + rank
    n_tiles = (n_tok * TOP_K) // tmr + n_experts
    r_rows = n_tiles * tmr
    slot_tok = (jnp.arange(n_tok * TOP_K, dtype=jnp.int32) // TOP_K).astype(F32)
    pad_tok = (jnp.arange(r_rows, dtype=jnp.int32) % n_tok).astype(F32)
    placed = jnp.stack([pad_tok, jnp.zeros((r_rows,), F32)], axis=1).at[dest].set(
        jnp.stack([slot_tok, top_w.reshape(-1)], axis=1))
    src_tok = placed[:, 0].astype(jnp.int32)
    row_w = placed[:, 1]
    tile_ids = jnp.arange(n_tiles, dtype=jnp.int32)
    tile_expert = jnp.sum((tile_ids[:, None] >= jnp.cumsum(tiles_per)[None, :]).astype(jnp.int32), axis=1)
    n_used = jnp.sum(tiles_per).astype(jnp.int32)
    tile_expert = jnp.where(tile_ids < n_used, tile_expert, tile_expert[jnp.maximum(n_used - 1, 0)])
    tile_expert = jnp.minimum(tile_expert, n_experts - 1).astype(jnp.int32)
    return src_tok, row_w.reshape(r_rows, 1), tile_expert, n_used.reshape(1), dest.reshape(n_tok, TOP_K)


def moe_ffn_residual(x, h, logits, mod, gate_idx, w_g, w_u, w_d, layer_j, *, rows, seq, batch, final_g=None):
    d = x.shape[1]
    n_moe, n_experts, _, d_e = w_g.shape
    tmr = 512 if rows * TOP_K >= 8 * 512 * n_experts else 128
    tn = 512 if d_e % 512 == 0 and d >= 4096 else 256
    src_tok, row_w, tile_expert, n_used, dest = moe_route(logits, n_experts, tmr)
    xg = jnp.take(h, src_tok, axis=0, mode="clip")
    y = moe_experts(xg, row_w, tile_expert, n_used,
                    w_g.reshape(n_moe * n_experts, d, d_e), w_u.reshape(n_moe * n_experts, d, d_e),
                    w_d.reshape(n_moe * n_experts, d_e, d), layer_j * n_experts, tmr=tmr, tn=tn,
                    tn_down=2 * tn)
    y1 = jnp.take(y, dest[:, 0], axis=0, mode="clip")
    y2 = jnp.take(y, dest[:, 1], axis=0, mode="clip")
    tm = 256
    final = final_g is not None
    in_specs = [pl.BlockSpec((tm, d), lambda i: (i, 0)),
                pl.BlockSpec((tm, d), lambda i: (i, 0)),
                pl.BlockSpec((tm, d), lambda i: (i, 0)),
                pl.BlockSpec((None, 1, d), lambda i: (_mod_row(i, tm, seq, batch), 0, gate_idx))]
    args = [x, y1, y2, mod]
    if final:
        in_specs.append(pl.BlockSpec((1, d), lambda i: (0, 0)))
        args.append(final_g.reshape(1, d))
    return pl.pallas_call(
        functools.partial(_moe_combine_kernel, final=final),
        grid=(rows // tm,),
        in_specs=in_specs,
        out_specs=pl.BlockSpec((tm, d), lambda i: (i, 0)),
        out_shape=jax.ShapeDtypeStruct((rows, d), F32),
        compiler_params=_params(("parallel",), 10 * _nbytes((tm, d), F32)),
        name="moe_combine",
    )(*args)


def _rope_tables(seq, pad_rows):
    rows = seq // GRID_W
    half = HEAD_DIM // 2
    row = jnp.broadcast_to(jnp.arange(rows)[:, None], (rows, GRID_W)).reshape(-1).astype(F32)
    col = jnp.broadcast_to(jnp.arange(GRID_W)[None, :], (rows, GRID_W)).reshape(-1).astype(F32)
    inv_freq = ROPE_BASE ** (-jnp.arange(0, half, 2, dtype=F32) / half)
    ang_r = row[:, None] * inv_freq
    ang_c = col[:, None] * inv_freq
    ang = jnp.concatenate([ang_r, ang_r, ang_c, ang_c], axis=-1)
    cos, sin = jnp.cos(ang), jnp.sin(ang)
    first = (jnp.arange(HEAD_DIM) % half) < half // 2
    ones = jnp.ones((pad_rows, HEAD_DIM), F32)
    zeros = jnp.zeros((pad_rows, HEAD_DIM), F32)
    return (jnp.concatenate([cos, ones]), jnp.concatenate([jnp.where(first, -sin, 0.0), zeros]),
            jnp.concatenate([jnp.where(first, 0.0, sin), zeros]))


def kernel(x, c, ctx, c_ctx, w_mod, b_mod, g_attn_norm, w_in, attn_sink, ssm_a_re, ssm_a_im, ssm_log_dt,
           ssm_b_re, ssm_b_im, ssm_c_re, ssm_c_im, ssm_d, w_glu, g_out_attn, g_out_ssm, w_out, g_ffn_norm,
           w_ff_gate, w_ff_up, w_ff_down, w_router, w_exp_gate, w_exp_up, w_exp_down, g_final):
    batch, seq, d = x.shape
    n_ctx = ctx.shape[1]
    depth = w_in.shape[0]
    n_q = attn_sink.shape[1]
    n_kv = n_q // Q_PER_KV
    attn_w = n_q * HEAD_DIM
    kv_w = n_kv * HEAD_DIM
    rows_x = batch * seq
    rows_all = rows_x + batch * n_ctx
    big = rows_x >= 8192
    tm_big = 1024 if big else 256
    tm_mid = 512 if big else 256
    tn = 512 if big else 256
    geo = dict(seq=seq, batch=batch)

    xs = jnp.concatenate([x.reshape(rows_x, d), ctx.reshape(batch * n_ctx, d)], axis=0)
    cond = jnp.concatenate([c, c_ctx[None, :]], axis=0)
    rope_tabs = _rope_tables(seq, tm_big)
    rope_geo = dict(rows=rows_all, rows_x=rows_x, seq=seq, tm=tm_big, tn=tn)
    ssm_tables = jax.vmap(_ssm_tables)(ssm_a_re, ssm_a_im, ssm_log_dt, ssm_b_re, ssm_b_im, ssm_c_re, ssm_c_im)
    normed = False

    for i in range(depth):
        last = i == depth - 1
        rows_out = rows_x if last else rows_all
        mod = ada_modulation(cond, w_mod, b_mod, i)
        h = norm_modulate(xs, g_attn_norm[i], mod, 0, 1, rows=rows_all, **geo)
        u_col0 = attn_w + 2 * kv_w
        q_rot, q_plain = matmul_rope(h, w_in[i, :, :attn_w].astype(BF16), rope_tabs, n_rope_tiles=attn_w // tn,
                                     scale=HEAD_DIM ** -0.5 * LOG2E, both=True, **rope_geo)
        kv = matmul_rope(h, w_in[i, :, attn_w:u_col0].astype(BF16), rope_tabs, n_rope_tiles=kv_w // tn,
                         scale=1.0, both=False, **rope_geo)
        u = matmul(h, w_in[i, :, u_col0:].astype(BF16), rows=rows_all, tm=tm_big, tn=tn, out_dtype=F32)
        o_attn = attention(q_rot, q_plain, kv, attn_sink[i] * LOG2E, g_out_attn[i], batch=batch, seq=seq,
                           n_ctx=n_ctx, n_kv=n_kv, ctx_queries=not last)
        y = s5_mix(u, ssm_tables, ssm_d[i], i, n_state=ssm_a_re.shape[-1], batch=batch, seq=seq, n_ctx=n_ctx)
        o_ssm = matmul_glu_norm(y, w_glu[i].astype(BF16), g_out_ssm[i], rows=rows_out, tm=tm_big)
        xs = matmul_gated_residual([(o_attn, w_out[i, :attn_w].astype(BF16)),
                                    (o_ssm, w_out[i, attn_w:].astype(BF16))], xs, mod, 2,
                                   rows=rows_out, tm=tm_big, tn=tn, weight_outer=False, **geo)
        j = i // 2
        if i % 2 == 0:
            h2 = norm_modulate(xs, g_ffn_norm[i], mod, 3, 4, rows=rows_out, **geo)
            hid = matmul_swiglu(h2, w_ff_gate[j].astype(BF16), w_ff_up[j].astype(BF16),
                                rows=rows_out, tm=tm_big, tn=tn)
            xs = matmul_gated_residual([(hid, w_ff_down[j].astype(BF16))], xs, mod, 5, rows=rows_out, tm=tm_mid,
                                       tn=tn, weight_outer=True, **geo)
        else:
            n_e = w_router.shape[2]
            wr = jnp.zeros((d, LANES), F32).at[:, :n_e].set(w_router[j])
            h2, logits = norm_modulate(xs, g_ffn_norm[i], mod, 3, 4, rows=rows_out, w_router=wr, **geo)
            xs = moe_ffn_residual(xs, h2, logits, mod, 5, w_exp_gate, w_exp_up, w_exp_down, j,
                                  rows=rows_out, seq=seq, batch=batch, final_g=g_final if last else None)
            normed = last
    if not normed:
        xs = final_norm(xs, g_final, rows=rows_x)
    return xs.reshape(batch, seq, d)
```

```python
import functools
import math

import jax
import jax.numpy as jnp
from jax import lax
from jax.experimental import pallas as pl
from jax.experimental.pallas import tpu as pltpu

F32 = jnp.float32
BF16 = jnp.bfloat16

HEAD_DIM = 128
BLOCK = 128
Q_PER_KV = 3
GRID_W = 64
ROPE_BASE = 10000.0
SSM_GROUP = 16
N_MOD = 6
TOP_K = 2
EPS = 1e-6
NEG_INF = -1e30
LOG2E = math.log2(math.e)

LANES = 128
SUBLANES = 8
VMEM_BYTES_V7X = 64 * 1024 * 1024
VMEM_CAP = VMEM_BYTES_V7X - 8 * 1024 * 1024

SSM_CHUNK = 16


def _params(sem, est_bytes):
    limit = int(min(VMEM_CAP, max(32 * 1024 * 1024, est_bytes * 5 // 4)))
    return pltpu.CompilerParams(dimension_semantics=sem, vmem_limit_bytes=limit)


def _nbytes(shape, dtype):
    return math.prod(shape) * jnp.dtype(dtype).itemsize


def _mod_row(i, tm, seq, batch):
    return jnp.minimum((i * tm) // seq, batch)


def _mod_kernel(cb_ref, w_ref, b_ref, o_ref, *, n_rows, tn):
    k_dim = w_ref.shape[0]
    n_chunks = tn // LANES

    def body(kb, accs):
        k0 = pl.multiple_of(kb * SUBLANES, SUBLANES)
        wv = w_ref[pl.ds(k0, SUBLANES), :]
        new = []
        for r in range(n_rows):
            cv = cb_ref[r, pl.ds(k0, SUBLANES), :]
            sv = cv * jax.nn.sigmoid(cv)
            for c in range(n_chunks):
                new.append(accs[r * n_chunks + c] + wv[:, c * LANES:(c + 1) * LANES] * sv)
        return tuple(new)

    init = tuple(jnp.zeros((SUBLANES, LANES), F32) for _ in range(n_rows * n_chunks))
    accs = lax.fori_loop(0, k_dim // SUBLANES, body, init, unroll=4)
    o_ref[...] = jnp.zeros_like(o_ref)
    for r in range(n_rows):
        row = jnp.concatenate(
            [jnp.sum(accs[r * n_chunks + c], axis=0, keepdims=True) for c in range(n_chunks)], axis=1)
        o_ref[r:r + 1, :] = row + b_ref[...]


def ada_modulation(cond_rows, w_mod, b_mod, layer):
    n_rows, d = cond_rows.shape
    depth, _, n = w_mod.shape
    tn = 1024
    cb =jnp.broadcast_to(cond_rows[:, :, None], (n_rows, d, LANES))
    est = 2 * (_nbytes((d, tn), F32) + _nbytes((n_rows, d, LANES), F32))
    out = pl.pallas_call(
        functools.partial(_mod_kernel, n_rows=n_rows, tn=tn),
        grid=(n // tn,),
        in_specs=[pl.BlockSpec((n_rows, d, LANES), lambda j: (0, 0, 0)),
                  pl.BlockSpec((None, d, tn), lambda j: (layer, 0, j)),
                  pl.BlockSpec((None, 1, tn), lambda j: (layer, 0, j))],
        out_specs=pl.BlockSpec((SUBLANES, tn), lambda j: (0, j)),
        out_shape=jax.ShapeDtypeStruct((SUBLANES, n), F32),
        compiler_params=_params(("parallel",), est),
        name="ada_modulation",
    )(cb, w_mod, b_mod.reshape(depth, 1, n))
    return out.reshape(SUBLANES, 1, n)


def _split_bf16(v):
    hi = v.astype(BF16)
    return hi, (v - hi.astype(F32)).astype(BF16)


def _normmod_kernel(x_ref, g_ref, shift_ref, scale_ref, *rest, with_router):
    xf = x_ref[...]
    y = xf * lax.rsqrt(jnp.mean(xf * xf, axis=-1, keepdims=True) + EPS)
    h = (y * g_ref[...]) * (1.0 + scale_ref[...]) + shift_ref[...]
    if with_router:
        wr_ref, o_ref, lg_ref = rest
        hs = _split_bf16(h)
        ws = _split_bf16(wr_ref[...])
        acc = None
        for a, b in ((0, 0), (0, 1), (1, 0)):
            t = jnp.dot(hs[a], ws[b], preferred_element_type=F32)
            acc = t if acc is None else acc + t
        lg_ref[...] = acc
    else:
        (o_ref,) = rest
    o_ref[...] = h.astype(BF16)


def norm_modulate(x, g, mod, shift_idx, scale_idx, *, rows, seq, batch, w_router=None):
    d = x.shape[1]
    tm = 256
    with_router = w_router is not None
    in_specs = [pl.BlockSpec((tm, d), lambda i: (i, 0)),
                pl.BlockSpec((1, d), lambda i: (0, 0)),
                pl.BlockSpec((None, 1, d), lambda i: (_mod_row(i, tm, seq, batch), 0, shift_idx)),
                pl.BlockSpec((None, 1, d), lambda i: (_mod_row(i, tm, seq, batch), 0, scale_idx))]
    args = [x, g.reshape(1, d), mod, mod]
    out_specs = [pl.BlockSpec((tm, d), lambda i: (i, 0))]
    out_shape = [jax.ShapeDtypeStruct((rows, d), BF16)]
    if with_router:
        in_specs.append(pl.BlockSpec((d, LANES), lambda i: (0, 0)))
        args.append(w_router)
        out_specs.append(pl.BlockSpec((tm, LANES), lambda i: (i, 0)))
        out_shape.append(jax.ShapeDtypeStruct((rows, LANES), F32))
    est = 2 * (_nbytes((tm, d), F32) + _nbytes((tm, d), BF16)) + 8 * _nbytes((tm, d), F32)
    outs = pl.pallas_call(
        functools.partial(_normmod_kernel, with_router=with_router),
        grid=(pl.cdiv(rows, tm),),
        in_specs=in_specs, out_specs=out_specs, out_shape=out_shape,
        compiler_params=_params(("parallel",), est),
        name="norm_modulate_router" if with_router else "norm_modulate",
    )(*args)
    return outs if with_router else outs[0]


def _mm_kernel(x_ref, w_ref, o_ref):
    o_ref[...] = jnp.dot(x_ref[...], w_ref[...], preferred_element_type=F32).astype(o_ref.dtype)


def _mm_swiglu_kernel(x_ref, wg_ref, wu_ref, o_ref):
    x = x_ref[...]
    a = jnp.dot(x, wg_ref[...], preferred_element_type=F32)
    b = jnp.dot(x, wu_ref[...], preferred_element_type=F32)
    o_ref[...] = (a * jax.nn.sigmoid(a) * b).astype(o_ref.dtype)


def _rms(x, g):
    return x * lax.rsqrt(jnp.mean(x * x, axis=-1, keepdims=True) + EPS) * g


def _mm_resid_kernel(*refs, n_parts):
    x_refs, w_refs = refs[:n_parts], refs[n_parts:2 * n_parts]
    r_ref, gate_ref, o_ref = refs[2 * n_parts:]
    acc = None
    for x_ref, w_ref in zip(x_refs, w_refs):
        t = jnp.dot(x_ref[...], w_ref[...], preferred_element_type=F32)
        acc = t if acc is None else acc + t
    o_ref[...] = r_ref[...] + gate_ref[...] * acc


def _mm_glu_norm_kernel(y_ref, w_ref, g_ref, o_ref):
    y = y_ref[...]
    z = jnp.dot(y.astype(BF16), w_ref[...], preferred_element_type=F32)
    o_ref[...] = _rms(y * jax.nn.sigmoid(z), g_ref[...]).astype(BF16)


def matmul(x, w, *, rows, tm, tn, out_dtype):
    k, n = w.shape
    est = 2 * (_nbytes((tm, k), x.dtype) + _nbytes((k, tn), w.dtype) + _nbytes((tm, tn), out_dtype))
    return pl.pallas_call(
        _mm_kernel,
        grid=(pl.cdiv(rows, tm), pl.cdiv(n, tn)),
        in_specs=[pl.BlockSpec((tm, k), lambda i, j: (i, 0)),
                  pl.BlockSpec((k, tn), lambda i, j: (0, j))],
        out_specs=pl.BlockSpec((tm, tn), lambda i, j: (i, j)),
        out_shape=jax.ShapeDtypeStruct((rows, n), out_dtype),
        compiler_params=_params(("parallel", "parallel"), est),
        name="matmul",
    )(x, w)


def _rope(x, cos, sin_lo, sin_hi):
    return (x * cos + pltpu.roll(x, HEAD_DIM - HEAD_DIM // 4, 1) * sin_lo
            + pltpu.roll(x, HEAD_DIM // 4, 1) * sin_hi)


def _mm_rope_kernel(x_ref, w_ref, cos_ref, slo_ref, shi_ref, *o_refs, n_rope_tiles, scale, both, sub):
    j = pl.program_id(1)
    tm, tn = o_refs[0].shape
    for r0 in range(0, tm, sub):
        acc = jnp.dot(x_ref[r0:r0 + sub, :], w_ref[...], preferred_element_type=F32)
        if scale != 1.0:
            acc = acc * scale
        cos, slo, shi = (r[r0:r0 + sub, :] for r in (cos_ref, slo_ref, shi_ref))
        rot = jnp.concatenate([_rope(acc[:, h * HEAD_DIM:(h + 1) * HEAD_DIM], cos, slo, shi)
                               for h in range(tn // HEAD_DIM)], axis=1)
        if both:
            o_refs[0][r0:r0 + sub, :] = rot.astype(BF16)
            o_refs[1][r0:r0 + sub, :] = acc.astype(BF16)
        else:
            o_refs[0][r0:r0 + sub, :] = jnp.where(j < n_rope_tiles, rot, acc).astype(BF16)


def matmul_rope(x, w, tabs, *, rows, rows_x, seq, tm, tn, n_rope_tiles, scale, both):
    k, n = w.shape
    per_seq = seq // tm

    def tab_map(i, j):
        return (jnp.where(i * tm < rows_x, i % per_seq, per_seq), 0)

    n_out = 2 if both else 1
    est = 2 * (_nbytes((tm, k), BF16) + _nbytes((k, tn), BF16) + n_out * _nbytes((tm, tn), BF16)
               + 3 * _nbytes((tm, HEAD_DIM), F32)) + 4 * _nbytes((tm, tn), F32)
    out = pl.pallas_call(
        functools.partial(_mm_rope_kernel, n_rope_tiles=n_rope_tiles, scale=scale, both=both, sub=min(tm, 256)),
        grid=(pl.cdiv(rows, tm), n // tn),
        in_specs=[pl.BlockSpec((tm, k), lambda i, j: (i, 0)),
                  pl.BlockSpec((k, tn), lambda i, j: (0, j))] + [pl.BlockSpec((tm, HEAD_DIM), tab_map)] * 3,
        out_specs=[pl.BlockSpec((tm, tn), lambda i, j: (i, j))] * n_out,
        out_shape=[jax.ShapeDtypeStruct((rows, n), BF16)] * n_out,
        compiler_params=_params(("parallel", "parallel"), est),
        name="matmul_rope",
    )(x, w, *tabs)
    return out if both else out[0]


def matmul_swiglu(x, wg, wu, *, rows, tm, tn):
    k, n = wg.shape
    est = 2 * (_nbytes((tm, k), BF16) + 2 * _nbytes((k, tn), BF16) + _nbytes((tm, tn), BF16)) \
        + 3 * _nbytes((tm, tn), F32)
    return pl.pallas_call(
        _mm_swiglu_kernel,
        grid=(pl.cdiv(rows, tm), pl.cdiv(n, tn)),
        in_specs=[pl.BlockSpec((tm, k), lambda i, j: (i, 0)),
                  pl.BlockSpec((k, tn), lambda i, j: (0, j)),
                  pl.BlockSpec((k, tn), lambda i, j: (0, j))],
        out_specs=pl.BlockSpec((tm, tn), lambda i, j: (i, j)),
        out_shape=jax.ShapeDtypeStruct((rows, n), BF16),
        compiler_params=_params(("parallel", "parallel"), est),
        name="matmul_swiglu",
    )(x, wg, wu)


def matmul_gated_residual(xw, resid, mod, gate_idx, *, rows, tm, tn, seq, batch, weight_outer):
    xs_, ws_ = [p[0] for p in xw], [p[1] for p in xw]
    n = ws_[0].shape[1]
    k = sum(w.shape[0] for w in ws_)
    nj = n // tn
    if weight_outer:
        grid = (nj, pl.cdiv(rows, tm))
        ij = lambda a, b: (b, a)
    else:
        grid = (pl.cdiv(rows, tm), nj)
        ij = lambda a, b: (a, b)

    def gate_map(a, b):
        i, j = ij(a, b)
        return (_mod_row(i, tm, seq, batch), 0, gate_idx * nj + j)

    est = 2 * (_nbytes((tm, k), BF16) + _nbytes((k, tn), BF16) + 2 * _nbytes((tm, tn), F32))
    return pl.pallas_call(
        functools.partial(_mm_resid_kernel, n_parts=len(xw)),
        grid=grid,
        in_specs=[pl.BlockSpec((tm, x.shape[1]), lambda a, b: (ij(a, b)[0], 0)) for x in xs_]
        + [pl.BlockSpec((w.shape[0], tn), lambda a, b: (0, ij(a, b)[1])) for w in ws_]
        + [pl.BlockSpec((tm, tn), lambda a, b: ij(a, b)),
           pl.BlockSpec((None, 1, tn), gate_map)],
        out_specs=pl.BlockSpec((tm, tn), lambda a, b: ij(a, b)),
        out_shape=jax.ShapeDtypeStruct((rows, n), F32),
        compiler_params=_params(("parallel", "parallel"), est),
        name="matmul_gated_residual",
    )(*xs_, *ws_, resid, mod)


def matmul_glu_norm(y, w, g, *, rows, tm):
    k, n = w.shape
    est = 2 * (_nbytes((tm, k), F32) + _nbytes((k, n), BF16) + _nbytes((tm, n), BF16)) + 4 * _nbytes((tm, n), F32)
    return pl.pallas_call(
        _mm_glu_norm_kernel,
        grid=(pl.cdiv(rows, tm),),
        in_specs=[pl.BlockSpec((tm, k), lambda i: (i, 0)),
                  pl.BlockSpec((k, n), lambda i: (0, 0)),
                  pl.BlockSpec((1, n), lambda i: (0, 0))],
        out_specs=pl.BlockSpec((tm, n), lambda i: (i, 0)),
        out_shape=jax.ShapeDtypeStruct((rows, n), BF16),
        compiler_params=_params(("parallel",), est),
        name="matmul_glu_norm",
    )(y, w, g.reshape(1, n))


def _nt_dot(a, b):
    return lax.dot_general(a, b, (((1,), (1,)), ((), ())), preferred_element_type=F32)


def _attn_kernel(sink_ref, qr_ref, qp_ref, kp_ref, k0_ref, kn_ref, vp_ref, v0_ref, vn_ref, kc_ref, vc_ref,
                 bias_ref, g_ref, o_ref, o_scr, *, n_blocks, ctx_queries, n_kv):
    n = pl.program_id(1)
    qw = Q_PER_KV * HEAD_DIM

    def stack_heads(ref, hh):
        return jnp.concatenate([ref[:, hh * qw + g * HEAD_DIM:hh * qw + (g + 1) * HEAD_DIM]
                                for g in range(Q_PER_KV)], axis=0)

    def head(ref, hh):
        return ref[:, hh * HEAD_DIM:(hh + 1) * HEAD_DIM]

    def run_head(window, hh):
        sink = jnp.concatenate(
            [jnp.full((BLOCK, 1), sink_ref[hh * Q_PER_KV + g], F32) for g in range(Q_PER_KV)], axis=0)
        s = _nt_dot(stack_heads(qp_ref, hh), head(kc_ref, hh))
        v = head(vc_ref, hh)
        if window:
            k_win = jnp.concatenate([head(kp_ref, hh), head(k0_ref, hh), head(kn_ref, hh)], axis=0)
            s_w = _nt_dot(stack_heads(qr_ref, hh), k_win) + bias_ref[...]
            s = jnp.concatenate([s_w, s], axis=1)
            v = jnp.concatenate([head(vp_ref, hh), head(v0_ref, hh), head(vn_ref, hh), v], axis=0)
        n_keys = s.shape[1]
        m = jnp.maximum(jnp.max(s, axis=1, keepdims=True), sink)
        m_b = jnp.broadcast_to(m, (Q_PER_KV * BLOCK, LANES))
        sink_b = jnp.broadcast_to(sink, (Q_PER_KV * BLOCK, LANES))
        p = jnp.exp2(s - jnp.concatenate([m_b] * (n_keys // LANES), axis=1)).astype(BF16)
        v_ext = jnp.concatenate([v, jnp.ones((n_keys, HEAD_DIM), BF16)], axis=1)
        o_ext = jnp.dot(p, v_ext, preferred_element_type=F32)
        o = o_ext[:, :HEAD_DIM] / (o_ext[:, HEAD_DIM:] + jnp.exp2(sink_b - m_b))
        for g in range(Q_PER_KV):
            o_scr[:, hh * qw + g * HEAD_DIM:hh * qw + (g + 1) * HEAD_DIM] = o[g * BLOCK:(g + 1) * BLOCK]

    def run(window):
        for hh in range(n_kv):
            run_head(window, hh)
        o_ref[...] = _rms(o_scr[...], g_ref[...]).astype(BF16)

    if ctx_queries:
        pl.when(n < n_blocks)(lambda: run(True))
        pl.when(n >= n_blocks)(lambda: run(False))
    else:
        run(True)


def _window_bias(n_blocks):
    rows = Q_PER_KV * BLOCK
    row = (jnp.arange(rows) % BLOCK)[:, None]
    col = jnp.arange(3 * BLOCK)[None, :]
    band = (col >= row) & (col <= row + 2 * BLOCK)
    variants = []
    for code in range(4):
        ok = band
        if code & 1:
            ok = ok & (col >= BLOCK)
        if code & 2:
            ok = ok & (col < 2 * BLOCK)
        variants.append(jnp.where(ok, 0.0, NEG_INF).astype(F32))
    return jnp.stack(variants)


def attention(q_rot, q_plain, kv, sink, g_out, *, batch, seq, n_ctx, n_kv, ctx_queries):
    nb = seq // BLOCK
    qb = n_ctx // BLOCK if ctx_queries else 0
    qw = n_kv * Q_PER_KV * HEAD_DIM
    kw = n_kv * HEAD_DIM
    rows_out = batch * seq + (batch * n_ctx if ctx_queries else 0)
    ctx_blk0 = batch * seq // n_ctx
    q_blk0 = batch * seq // BLOCK

    def q_map(b, n, s):
        return (jnp.where(n < nb, b * nb + n, q_blk0 + b * qb + (n - nb)), 0)

    def kv_spec(col, shift):
        return pl.BlockSpec((BLOCK, kw), lambda b, n, s: (b * nb + jnp.clip(n + shift, 0, nb - 1), col))

    def bias_map(b, n, s):
        return ((n == 0).astype(jnp.int32) + 2 * (n == nb - 1).astype(jnp.int32), 0, 0)

    in_specs = [pl.BlockSpec((BLOCK, qw), q_map), pl.BlockSpec((BLOCK, qw), q_map),
                kv_spec(0, -1), kv_spec(0, 0), kv_spec(0, 1),
                kv_spec(1, -1), kv_spec(1, 0), kv_spec(1, 1),
                pl.BlockSpec((n_ctx, kw), lambda b, n, s: (ctx_blk0 + b, 0)),
                pl.BlockSpec((n_ctx, kw), lambda b, n, s: (ctx_blk0 + b, 1)),
                pl.BlockSpec((None, Q_PER_KV * BLOCK, 3 * BLOCK), bias_map),
                pl.BlockSpec((1, qw), lambda b, n, s: (0, 0))]
    grid_spec = pltpu.PrefetchScalarGridSpec(
        num_scalar_prefetch=1, grid=(batch, nb + qb), in_specs=in_specs,
        out_specs=pl.BlockSpec((BLOCK, qw), q_map),
        scratch_shapes=[pltpu.VMEM((BLOCK, qw), F32)])
    return pl.pallas_call(
        functools.partial(_attn_kernel, n_blocks=nb, ctx_queries=ctx_queries, n_kv=n_kv),
        grid_spec=grid_spec,
        out_shape=jax.ShapeDtypeStruct((rows_out, qw), BF16),
        compiler_params=_params(("parallel", "parallel"), 16 * 1024 * 1024),
        name="attention",
    )(sink, q_rot, q_plain, *([kv] * 8), _window_bias(nb), g_out.reshape(1, qw))


def _ssm_tables(a_re, a_im, log_dt, b_re, b_im, c_re, c_im):
    hp = lax.Precision.HIGHEST
    t_len = SSM_CHUNK
    g_per = LANES // SSM_GROUP
    n_groups, n_state = a_re.shape[1:]
    n_lb = n_groups // g_per
    dt = jnp.exp(log_dt.astype(F32))[..., None]
    lam_re, lam_im = a_re.astype(F32), a_im.astype(F32)
    mag = jnp.exp(lam_re * dt)
    abar = lax.complex(mag * jnp.cos(lam_im * dt), mag * jnp.sin(lam_im * dt))
    lam = lax.complex(lam_re, lam_im)
    bbar = ((abar - 1.0) / lam)[..., None] * lax.complex(b_re.astype(F32), b_im.astype(F32))
    cmat = lax.complex(c_re.astype(F32), c_im.astype(F32))

    def powers(d, exps):
        e = jnp.asarray(exps, F32)[:, None, None]
        m = jnp.exp(lam_re[d] * dt[d] * e)
        ph = lam_im[d] * dt[d] * e
        return lax.complex(m * jnp.cos(ph), m * jnp.sin(ph))

    ar = list(range(t_len + 1))
    pw_f, pw_b = powers(0, ar), powers(1, ar)

    kf = jnp.real(jnp.einsum('gcp,tgp,gpi->tgci', cmat[0], pw_f[:t_len], bbar[0], precision=hp))
    kb = jnp.real(jnp.einsum('gcp,tgp,gpi->tgci', cmat[1], pw_b[:t_len], bbar[1], precision=hp))
    lag = jnp.arange(2 * t_len - 1) - (t_len - 1)
    k_lag = jnp.where((lag >= 0)[:, None, None, None], kf[jnp.clip(lag, 0)], 0.0) \
        + jnp.where((lag <= 0)[:, None, None, None], kb[jnp.clip(-lag, 0)], 0.0)
    k_lag = jnp.swapaxes(k_lag, -1, -2).reshape(2 * t_len - 1, n_lb, g_per, SSM_GROUP, SSM_GROUP)
    eye = jnp.eye(g_per, dtype=F32)
    k_lag = jnp.einsum('mngab,gh->mngahb', k_lag, eye).reshape(2 * t_len - 1, n_lb, LANES, LANES)

    rep = LANES // n_state

    def e_coef(d, pw_sel):
        pw_rows = jnp.concatenate([pw_sel] * rep, axis=-1).reshape(t_len, n_lb, g_per, 1, LANES)
        pw_rows = jnp.broadcast_to(pw_rows, (t_len, n_lb, g_per, SSM_GROUP, LANES)).reshape(t_len, n_lb, LANES, LANES)
        bb_rows = jnp.concatenate([jnp.swapaxes(bbar[d], -1, -2)] * rep, axis=-1).reshape(n_lb, LANES, LANES)
        return pw_rows * bb_rows[None]

    ef, eb = e_coef(0, powers(0, [t_len - 1 - j for j in range(t_len)])), e_coef(1, pw_b[:t_len])
    e_c = jnp.concatenate([jnp.real(ef), jnp.imag(ef), jnp.real(eb), jnp.imag(eb)], axis=-1)

    def f_coef(d, pw_sel):
        cm = cmat[d].reshape(n_lb, g_per, SSM_GROUP, n_state).transpose(0, 3, 1, 2)
        cm = cm.reshape(n_lb, n_state, 1, LANES)
        pw = pw_sel.reshape(t_len, n_lb, g_per, n_state).transpose(1, 3, 0, 2)
        pw = jnp.repeat(pw, SSM_GROUP, axis=-1)
        return cm * pw

    ff, fb = f_coef(0, pw_f[1:]), f_coef(1, powers(1, [t_len - l for l in range(t_len)]))
    f_c = jnp.stack([jnp.real(ff), -jnp.imag(ff), jnp.real(fb), -jnp.imag(fb)], axis=1)
    f_c = f_c.reshape(n_lb, 4 * n_state * t_len, LANES)

    def lay(v):
        return v.reshape(n_lb, g_per * n_state)

    a_chunk = jnp.concatenate([lay(jnp.real(pw_f[t_len])), lay(jnp.imag(pw_f[t_len])),
                               lay(jnp.real(pw_b[t_len])), lay(jnp.imag(pw_b[t_len]))], axis=-1)
    return k_lag.astype(BF16), e_c, f_c, a_chunk.reshape(1, -1)


def _chunk_inputs(x_ref, tmc):
    return [x_ref[pl.ds(j, tmc, stride=SSM_CHUNK), :] for j in range(SSM_CHUNK)]


def _ssm_local_kernel(x_ref, ec_ref, s_ref, e_scr, *, n_state):
    tmc = s_ref.shape[0]
    qw = e_scr.shape[1] // 4

    @pl.when(pl.program_id(1) == 0)
    def _():
        row_g = lax.broadcasted_iota(jnp.int32, (LANES, qw), 0) // SSM_GROUP
        col_g = lax.broadcasted_iota(jnp.int32, (LANES, qw), 1) // n_state
        same = row_g == col_g
        for j in range(SSM_CHUNK):
            for q in range(4):
                v = ec_ref[j, :, q * LANES:(q + 1) * LANES]
                tiled = jnp.concatenate([v] * (qw // LANES), axis=1)
                e_scr[j * LANES:(j + 1) * LANES, q * qw:(q + 1) * qw] = jnp.where(same, tiled, 0.0).astype(BF16)

    xs = jnp.concatenate([x.astype(BF16) for x in _chunk_inputs(x_ref, tmc)], axis=1)
    s_ref[...] = jnp.dot(xs, e_scr[...], preferred_element_type=F32)


def _ssm_scan_kernel(s_ref, a_ref, h_ref, *, batch, nc_x, nc_c):
    sw = s_ref.shape[1] // 4
    a = a_ref[...]
    afr, afi, abr, abi = (a[:, k * sw:(k + 1) * sw] for k in range(4))
    ctx0 = batch * nc_x

    def step(row_f, row_b, st):
        new = []
        for (row, ar, ai, off, (hr, hi)) in ((row_f, afr, afi, 0, st[0]), (row_b, abr, abi, 2 * sw, st[1])):
            h_ref[pl.ds(row, 1), off:off + sw] = hr
            h_ref[pl.ds(row, 1), off + sw:off + 2 * sw] = hi
            sr = s_ref[pl.ds(row, 1), off:off + sw]
            si = s_ref[pl.ds(row, 1), off + sw:off + 2 * sw]
            new.append((ar * hr - ai * hi + sr, ar * hi + ai * hr + si))
        return tuple(new)

    zero = jnp.zeros((1, sw), F32)
    for b in range(batch):
        st = ((zero, zero), (zero, zero))
        c_base = ctx0 + b * nc_c
        st = lax.fori_loop(0, nc_c, lambda t, s: step(c_base + t, c_base + nc_c - 1 - t, s), st)
        x_base = b * nc_x
        lax.fori_loop(0, nc_x, lambda t, s: step(x_base + t, x_base + nc_x - 1 - t, s), st)


def _ssm_out_kernel(x_ref, h_ref, kl_ref, fc_ref, d_ref, o_ref, w_scr, *, n_state):
    t = SSM_CHUNK
    tmc = h_ref.shape[0]
    sdim = h_ref.shape[1]
    qw = sdim // 4

    @pl.when(pl.program_id(1) == 0)
    def _():
        for j in range(t):
            for l in range(t):
                w_scr[j * LANES:(j + 1) * LANES, l * LANES:(l + 1) * LANES] = kl_ref[l - j + t - 1]
        lane_g = lax.broadcasted_iota(jnp.int32, (n_state, LANES), 1) // SSM_GROUP
        for q in range(4):
            for l in range(t):
                v = fc_ref[pl.ds(q * n_state * t + l, n_state, stride=t), :]
                for g in range(qw // n_state):
                    r0 = t * LANES + q * qw + g * n_state
                    w_scr[r0:r0 + n_state, l * LANES:(l + 1) * LANES] = \
                        jnp.where(lane_g == g, v, 0.0).astype(BF16)

    xs = _chunk_inputs(x_ref, tmc)
    lhs = jnp.concatenate([x.astype(BF16) for x in xs] + [h_ref[...].astype(BF16)], axis=1)
    y = jnp.dot(lhs, w_scr[...], preferred_element_type=F32)
    d = d_ref[...]
    for l in range(t):
        o_ref[pl.ds(l, tmc, stride=t), :] = jax.nn.gelu(y[:, l * LANES:(l + 1) * LANES] + d * xs[l])


def s5_mix(u, tables, d_skip, layer, *, n_state, batch, seq, n_ctx):
    k_lag, e_c, f_c, a_chunk = tables
    rows, ssm_width = u.shape
    t = SSM_CHUNK
    n_lb = ssm_width // LANES
    rc = rows // t
    sdim = 4 * (LANES // SSM_GROUP) * n_state
    tmc = max(m for m in range(SUBLANES, min(rc, 264) + 1, SUBLANES) if rc % m == 0)
    x_spec = pl.BlockSpec((tmc * t, LANES), lambda g, r: (r, g))
    x_bytes = _nbytes((tmc * t, LANES), F32)

    s_loc = pl.pallas_call(
        functools.partial(_ssm_local_kernel, n_state=n_state),
        grid=(n_lb, rc // tmc),
        in_specs=[x_spec, pl.BlockSpec((None, t, None, LANES, 4 * LANES), lambda g, r: (layer, 0, g, 0, 0))],
        out_specs=pl.BlockSpec((tmc, sdim), lambda g, r: (r, g)),
        out_shape=jax.ShapeDtypeStruct((rc, n_lb * sdim), F32),
        scratch_shapes=[pltpu.VMEM((t * LANES, sdim), BF16)],
        compiler_params=_params(("arbitrary", "arbitrary"),
                                2 * x_bytes + 2 * _nbytes((t, LANES, 4 * LANES), F32)
                                + _nbytes((t * LANES, sdim), BF16) + 4 * _nbytes((tmc, sdim), F32)),
        name="s5_local_states",
    )(u, e_c)

    h_in = pl.pallas_call(
        functools.partial(_ssm_scan_kernel, batch=batch, nc_x=seq // t, nc_c=n_ctx // t),
        grid=(n_lb,),
        in_specs=[pl.BlockSpec((rc, sdim), lambda g: (0, g)),
                  pl.BlockSpec((None, 1, sdim), lambda g: (layer, 0, g))],
        out_specs=pl.BlockSpec((rc, sdim), lambda g: (0, g)),
        out_shape=jax.ShapeDtypeStruct((rc, n_lb * sdim), F32),
        compiler_params=_params(("parallel",), 4 * _nbytes((rc, sdim), F32)),
        name="s5_chunk_scan",
    )(s_loc, a_chunk)

    return pl.pallas_call(
        functools.partial(_ssm_out_kernel, n_state=n_state),
        grid=(n_lb, rc // tmc),
        in_specs=[x_spec,
                  pl.BlockSpec((tmc, sdim), lambda g, r: (r, g)),
                  pl.BlockSpec((None, 2 * t - 1, None, LANES, LANES), lambda g, r: (layer, 0, g, 0, 0)),
                  pl.BlockSpec((None, None, 4 * n_state * t, LANES), lambda g, r: (layer, g, 0, 0)),
                  pl.BlockSpec((1, LANES), lambda g, r: (0, g))],
        out_specs=x_spec,
        out_shape=jax.ShapeDtypeStruct((rows, ssm_width), F32),
        scratch_shapes=[pltpu.VMEM((t * LANES + sdim, t * LANES), BF16)],
        compiler_params=_params(("arbitrary", "arbitrary"),
                                4 * x_bytes + 2 * _nbytes((tmc, sdim), F32)
                                + 2 * _nbytes((2 * t - 1, LANES, LANES), BF16)
                                + 2 * _nbytes((4 * n_state, t * LANES), F32)
                                + _nbytes((t * LANES + sdim, t * LANES), BF16) + 4 * _nbytes((tmc, sdim), F32)),
        name="s5_readout",
    )(u, h_in, k_lag, f_c, d_skip.reshape(1, ssm_width))


def _final_norm_kernel(x_ref, g_ref, o_ref):
    o_ref[...] = _rms(x_ref[...], g_ref[...])


def final_norm(x, g, *, rows):
    d = x.shape[1]
    tm = 256
    return pl.pallas_call(
        _final_norm_kernel,
        grid=(pl.cdiv(rows, tm),),
        in_specs=[pl.BlockSpec((tm, d), lambda i: (i, 0)), pl.BlockSpec((1, d), lambda i: (0, 0))],
        out_specs=pl.BlockSpec((tm, d), lambda i: (i, 0)),
        out_shape=jax.ShapeDtypeStruct((rows, d), F32),
        compiler_params=_params(("parallel",), 6 * _nbytes((tm, d), F32)),
        name="final_norm",
    )(x, g.reshape(1, d))


def _new_expert(te_ref, t):
    return jnp.logical_or(t == 0, te_ref[t] != te_ref[jnp.maximum(t - 1, 0)])


def _moe_up_kernel(te_ref, nu_ref, x_ref, wg_ref, wu_ref, o_ref, wg_scr, wu_scr):
    t = pl.program_id(1)

    @pl.when(_new_expert(te_ref, t))
    def _():
        wg_scr[...] = wg_ref[...].astype(BF16)
        wu_scr[...] = wu_ref[...].astype(BF16)

    @pl.when(t < nu_ref[0])
    def _():
        x = x_ref[...]
        a = jnp.dot(x, wg_scr[...], preferred_element_type=F32)
        b = jnp.dot(x, wu_scr[...], preferred_element_type=F32)
        o_ref[...] = (a * jax.nn.sigmoid(a) * b).astype(o_ref.dtype)

    @pl.when(t >= nu_ref[0])
    def _():
        o_ref[...] = jnp.zeros_like(o_ref)


def _moe_down_kernel(te_ref, nu_ref, h_ref, wd_ref, rw_ref, o_ref, wd_scr):
    t = pl.program_id(1)

    @pl.when(_new_expert(te_ref, t))
    def _():
        wd_scr[...] = wd_ref[...].astype(BF16)

    @pl.when(t < nu_ref[0])
    def _():
        acc = jnp.dot(h_ref[...], wd_scr[...], preferred_element_type=F32)
        o_ref[...] = (acc * rw_ref[...]).astype(o_ref.dtype)

    @pl.when(t >= nu_ref[0])
    def _():
        o_ref[...] = jnp.zeros_like(o_ref)


def moe_experts(xg, row_w, tile_expert, n_used, w_g, w_u, w_d, e_base, *, tmr, tn, tn_down):
    r_rows, d = xg.shape
    d_e = w_g.shape[2]
    n_tiles = r_rows // tmr
    est = 2 * (_nbytes((tmr, d), BF16) + 2 * _nbytes((d, tn), F32) + _nbytes((tmr, tn), BF16)) \
        + 2 * _nbytes((d, tn), BF16) + 3 * _nbytes((tmr, tn), F32)
    hidden = pl.pallas_call(
        _moe_up_kernel,
        grid_spec=pltpu.PrefetchScalarGridSpec(
            num_scalar_prefetch=2, grid=(d_e // tn, n_tiles),
            in_specs=[pl.BlockSpec((tmr, d), lambda j, t, te, nu: (t, 0)),
                      pl.BlockSpec((None, d, tn), lambda j, t, te, nu: (e_base + te[t], 0, j)),
                      pl.BlockSpec((None, d, tn), lambda j, t, te, nu: (e_base + te[t], 0, j))],
            out_specs=pl.BlockSpec((tmr, tn), lambda j, t, te, nu: (t, j)),
            scratch_shapes=[pltpu.VMEM((d, tn), BF16), pltpu.VMEM((d, tn), BF16)]),
        out_shape=jax.ShapeDtypeStruct((r_rows, d_e), BF16),
        compiler_params=_params(("arbitrary", "arbitrary"), est),
        name="moe_gate_up",
    )(tile_expert, n_used, xg, w_g, w_u)
    tn = tn_down
    est = 2 * (_nbytes((tmr, d_e), BF16) + _nbytes((d_e, tn), F32) + _nbytes((tmr, tn), BF16)) \
        + _nbytes((d_e, tn), BF16) + 2 * _nbytes((tmr, tn), F32)
    return pl.pallas_call(
        _moe_down_kernel,
        grid_spec=pltpu.PrefetchScalarGridSpec(
            num_scalar_prefetch=2, grid=(d // tn, n_tiles),
            in_specs=[pl.BlockSpec((tmr, d_e), lambda j, t, te, nu: (t, 0)),
                      pl.BlockSpec((None, d_e, tn), lambda j, t, te, nu: (e_base + te[t], 0, j)),
                      pl.BlockSpec((tmr, 1), lambda j, t, te, nu: (t, 0))],
            out_specs=pl.BlockSpec((tmr, tn), lambda j, t, te, nu: (t, j)),
            scratch_shapes=[pltpu.VMEM((d_e, tn), BF16)]),
        out_shape=jax.ShapeDtypeStruct((r_rows, d), BF16),
        compiler_params=_params(("arbitrary", "arbitrary"), est),
        name="moe_down",
    )(tile_expert, n_used, hidden, w_d, row_w)


def _moe_combine_kernel(x_ref, y1_ref, y2_ref, gate_ref, *rest, final):
    out = x_ref[...] + gate_ref[...] * (y1_ref[...].astype(F32) + y2_ref[...].astype(F32))
    if final:
        g_ref, o_ref = rest
        o_ref[...] = _rms(out, g_ref[...])
    else:
        (o_ref,) = rest
        o_ref[...] = out


def moe_route(logits, n_experts, tmr):
    n_tok = logits.shape[0]
    lg = logits[:, :n_experts]
    top_v, top_i = lax.top_k(lg, TOP_K)
    top_w = jax.nn.softmax(top_v, axis=-1)
    flat_e = top_i.reshape(-1)
    onehot = (flat_e[:, None] == jnp.arange(n_experts)[None, :]).astype(jnp.int32)
    rank = jnp.take_along_axis(jnp.cumsum(onehot, axis=0) - onehot, flat_e[:, None], axis=1)[:, 0]
    counts = jnp.sum(onehot, axis=0)
    tiles_per = (counts + tmr - 1) // tmr
    tile_start = jnp.cumsum(tiles_per) - tiles_per
    dest = tile_start[flat_e] * tmr + rank
    n_tiles = (n_tok * TOP_K) // tmr + n_experts
    r_rows = n_tiles * tmr
    slot_tok = (jnp.arange(n_tok * TOP_K, dtype=jnp.int32) // TOP_K).astype(F32)
    pad_tok = (jnp.arange(r_rows, dtype=jnp.int32) % n_tok).astype(F32)
    placed = jnp.stack([pad_tok, jnp.zeros((r_rows,), F32)], axis=1).at[dest].set(
        jnp.stack([slot_tok, top_w.reshape(-1)], axis=1))
    src_tok = placed[:, 0].astype(jnp.int32)
    row_w = placed[:, 1]
    tile_ids = jnp.arange(n_tiles, dtype=jnp.int32)
    tile_expert = jnp.sum((tile_ids[:, None] >= jnp.cumsum(tiles_per)[None, :]).astype(jnp.int32), axis=1)
    n_used = jnp.sum(tiles_per).astype(jnp.int32)
    tile_expert = jnp.where(tile_ids < n_used, tile_expert, tile_expert[jnp.maximum(n_used - 1, 0)])
    tile_expert = jnp.minimum(tile_expert, n_experts - 1).astype(jnp.int32)
    return src_tok, row_w.reshape(r_rows, 1), tile_expert, n_used.reshape(1), dest.reshape(n_tok, TOP_K)


def moe_ffn_residual(x, h, logits, mod, gate_idx, w_g, w_u, w_d, layer_j, *, rows, seq, batch, final_g=None):
    d = x.shape[1]
    n_moe, n_experts, _, d_e = w_g.shape
    tmr = 512 if rows * TOP_K >= 8 * 512 * n_experts else 128
    tn = 512 if d_e % 512 == 0 and d >= 4096 else 256
    src_tok, row_w, tile_expert, n_used, dest = moe_route(logits, n_experts, tmr)
    xg = jnp.take(h, src_tok, axis=0, mode="clip")
    y = moe_experts(xg, row_w, tile_expert, n_used,
                    w_g.reshape(n_moe * n_experts, d, d_e), w_u.reshape(n_moe * n_experts, d, d_e),
                    w_d.reshape(n_moe * n_experts, d_e, d), layer_j * n_experts, tmr=tmr, tn=tn,
                    tn_down=2 * tn)
    y1 = jnp.take(y, dest[:, 0], axis=0, mode="clip")
    y2 = jnp.take(y, dest[:, 1], axis=0, mode="clip")
    tm = 256
    final = final_g is not None
    in_specs = [pl.BlockSpec((tm, d), lambda i: (i, 0)),
                pl.BlockSpec((tm, d), lambda i: (i, 0)),
                pl.BlockSpec((tm, d), lambda i: (i, 0)),
                pl.BlockSpec((None, 1, d), lambda i: (_mod_row(i, tm, seq, batch), 0, gate_idx))]
    args = [x, y1, y2, mod]
    if final:
        in_specs.append(pl.BlockSpec((1, d), lambda i: (0, 0)))
        args.append(final_g.reshape(1, d))
    return pl.pallas_call(
        functools.partial(_moe_combine_kernel, final=final),
        grid=(rows // tm,),
        in_specs=in_specs,
        out_specs=pl.BlockSpec((tm, d), lambda i: (i, 0)),
        out_shape=jax.ShapeDtypeStruct((rows, d), F32),
        compiler_params=_params(("parallel",), 10 * _nbytes((tm, d), F32)),
        name="moe_combine",
    )(*args)


def _rope_tables(seq, pad_rows):
    rows = seq // GRID_W
    half = HEAD_DIM // 2
    row = jnp.broadcast_to(jnp.arange(rows)[:, None], (rows, GRID_W)).reshape(-1).astype(F32)
    col = jnp.broadcast_to(jnp.arange(GRID_W)[None, :], (rows, GRID_W)).reshape(-1).astype(F32)
    inv_freq = ROPE_BASE ** (-jnp.arange(0, half, 2, dtype=F32) / half)
    ang_r = row[:, None] * inv_freq
    ang_c = col[:, None] * inv_freq
    ang = jnp.concatenate([ang_r, ang_r, ang_c, ang_c], axis=-1)
    cos, sin = jnp.cos(ang), jnp.sin(ang)
    first = (jnp.arange(HEAD_DIM) % half) < half // 2
    ones = jnp.ones((pad_rows, HEAD_DIM), F32)
    zeros = jnp.zeros((pad_rows, HEAD_DIM), F32)
    return (jnp.concatenate([cos, ones]), jnp.concatenate([jnp.where(first, -sin, 0.0), zeros]),
            jnp.concatenate([jnp.where(first, 0.0, sin), zeros]))


def kernel(x, c, ctx, c_ctx, w_mod, b_mod, g_attn_norm, w_in, attn_sink, ssm_a_re, ssm_a_im, ssm_log_dt,
           ssm_b_re, ssm_b_im, ssm_c_re, ssm_c_im, ssm_d, w_glu, g_out_attn, g_out_ssm, w_out, g_ffn_norm,
           w_ff_gate, w_ff_up, w_ff_down, w_router, w_exp_gate, w_exp_up, w_exp_down, g_final):
    batch, seq, d = x.shape
    n_ctx = ctx.shape[1]
    depth = w_in.shape[0]
    n_q = attn_sink.shape[1]
    n_kv = n_q // Q_PER_KV
    attn_w = n_q * HEAD_DIM
    kv_w = n_kv * HEAD_DIM
    rows_x = batch * seq
    rows_all = rows_x + batch * n_ctx
    big = rows_x >= 8192
    tm_big = 1024 if big else 256
    tm_mid = 512 if big else 256
    tn = 512 if big else 256
    geo = dict(seq=seq, batch=batch)

    xs = jnp.concatenate([x.reshape(rows_x, d), ctx.reshape(batch * n_ctx, d)], axis=0)
    cond = jnp.concatenate([c, c_ctx[None, :]], axis=0)
    rope_tabs = _rope_tables(seq, tm_big)
    rope_geo = dict(rows=rows_all, rows_x=rows_x, seq=seq, tm=tm_big, tn=tn)
    ssm_tables = jax.vmap(_ssm_tables)(ssm_a_re, ssm_a_im, ssm_log_dt, ssm_b_re, ssm_b_im, ssm_c_re, ssm_c_im)
    normed = False

    for i in range(depth):
        last = i == depth - 1
        rows_out = rows_x if last else rows_all
        mod = ada_modulation(cond, w_mod, b_mod, i)
        h = norm_modulate(xs, g_attn_norm[i], mod, 0, 1, rows=rows_all, **geo)
        u_col0 = attn_w + 2 * kv_w
        q_rot, q_plain = matmul_rope(h, w_in[i, :, :attn_w].astype(BF16), rope_tabs, n_rope_tiles=attn_w // tn,
                                     scale=HEAD_DIM ** -0.5 * LOG2E, both=True, **rope_geo)
        kv = matmul_rope(h, w_in[i, :, attn_w:u_col0].astype(BF16), rope_tabs, n_rope_tiles=kv_w // tn,
                         scale=1.0, both=False, **rope_geo)
        u = matmul(h, w_in[i, :, u_col0:].astype(BF16), rows=rows_all, tm=tm_big, tn=tn, out_dtype=F32)
        o_attn = attention(q_rot, q_plain, kv, attn_sink[i] * LOG2E, g_out_attn[i], batch=batch, seq=seq,
                           n_ctx=n_ctx, n_kv=n_kv, ctx_queries=not last)
        y = s5_mix(u, ssm_tables, ssm_d[i], i, n_state=ssm_a_re.shape[-1], batch=batch, seq=seq, n_ctx=n_ctx)
        o_ssm = matmul_glu_norm(y, w_glu[i].astype(BF16), g_out_ssm[i], rows=rows_out, tm=tm_big)
        xs = matmul_gated_residual([(o_attn, w_out[i, :attn_w].astype(BF16)),
                                    (o_ssm, w_out[i, attn_w:].astype(BF16))], xs, mod, 2,
                                   rows=rows_out, tm=tm_big, tn=tn, weight_outer=False, **geo)
        j = i // 2
        if i % 2 == 0:
            h2 = norm_modulate(xs, g_ffn_norm[i], mod, 3, 4, rows=rows_out, **geo)
            hid = matmul_swiglu(h2, w_ff_gate[j].astype(BF16), w_ff_up[j].astype(BF16),
                                rows=rows_out, tm=tm_big, tn=tn)
            xs = matmul_gated_residual([(hid, w_ff_down[j].astype(BF16))], xs, mod, 5, rows=rows_out, tm=tm_mid,
                                       tn=tn, weight_outer=True, **geo)
        else:
            n_e = w_router.shape[2]
            wr = jnp.zeros((d, LANES), F32).at[:, :n_e].set(w_router[j])
            h2, logits = norm_modulate(xs, g_ffn_norm[i], mod, 3, 4, rows=rows_out, w_router=wr, **geo)
            xs = moe_ffn_residual(xs, h2, logits, mod, 5, w_exp_gate, w_exp_up, w_exp_down, j,
                                  rows=rows_out, seq=seq, batch=batch, final_g=g_final if last else None)
            normed = last
    if not normed:
        xs = final_norm(xs, g_final, rows=rows_x)
    return xs.reshape(batch, seq, d)
```

```python
import functools
import math

import jax
import jax.numpy as jnp
from jax import lax
from jax.experimental import pallas as pl
from jax.experimental.pallas import tpu as pltpu

F32 = jnp.float32
BF16 = jnp.bfloat16

HEAD_DIM = 128
BLOCK = 128
Q_PER_KV = 3
GRID_W = 64
ROPE_BASE = 10000.0
SSM_GROUP = 16
N_MOD = 6
TOP_K = 2
EPS = 1e-6
NEG_INF = -1e30
LOG2E = math.log2(math.e)

LANES = 128
SUBLANES = 8
VMEM_BYTES_V7X = 64 * 1024 * 1024
VMEM_CAP = VMEM_BYTES_V7X - 8 * 1024 * 1024

SSM_CHUNK = 16


def _params(sem, est_bytes):
    limit = int(min(VMEM_CAP, max(32 * 1024 * 1024, est_bytes * 5 // 4)))
    return pltpu.CompilerParams(dimension_semantics=sem, vmem_limit_bytes=limit)


def _nbytes(shape, dtype):
    return math.prod(shape) * jnp.dtype(dtype).itemsize


def _mod_row(i, tm, seq, batch):
    return jnp.minimum((i * tm) // seq, batch)


def _mod_kernel(cb_ref, w_ref, b_ref, o_ref, *, n_rows, tn):
    k_dim = w_ref.shape[0]
    n_chunks = tn // LANES

    def body(kb, accs):
        k0 = pl.multiple_of(kb * SUBLANES, SUBLANES)
        wv = w_ref[pl.ds(k0, SUBLANES), :]
        new = []
        for r in range(n_rows):
            cv = cb_ref[r, pl.ds(k0, SUBLANES), :]
            sv = cv * jax.nn.sigmoid(cv)
            for c in range(n_chunks):
                new.append(accs[r * n_chunks + c] + wv[:, c * LANES:(c + 1) * LANES] * sv)
        return tuple(new)

    init = tuple(jnp.zeros((SUBLANES, LANES), F32) for _ in range(n_rows * n_chunks))
    accs = lax.fori_loop(0, k_dim // SUBLANES, body, init, unroll=4)
    o_ref[...] = jnp.zeros_like(o_ref)
    for r in range(n_rows):
        row = jnp.concatenate(
            [jnp.sum(accs[r * n_chunks + c], axis=0, keepdims=True) for c in range(n_chunks)], axis=1)
        o_ref[r:r + 1, :] = row + b_ref[...]


def ada_modulation(cond_rows, w_mod, b_mod, layer):
    n_rows, d = cond_rows.shape
    depth, _, n = w_mod.shape
    tn = 1024
    cb =jnp.broadcast_to(cond_rows[:, :, None], (n_rows, d, LANES))
    est = 2 * (_nbytes((d, tn), F32) + _nbytes((n_rows, d, LANES), F32))
    out = pl.pallas_call(
        functools.partial(_mod_kernel, n_rows=n_rows, tn=tn),
        grid=(n // tn,),
        in_specs=[pl.BlockSpec((n_rows, d, LANES), lambda j: (0, 0, 0)),
                  pl.BlockSpec((None, d, tn), lambda j: (layer, 0, j)),
                  pl.BlockSpec((None, 1, tn), lambda j: (layer, 0, j))],
        out_specs=pl.BlockSpec((SUBLANES, tn), lambda j: (0, j)),
        out_shape=jax.ShapeDtypeStruct((SUBLANES, n), F32),
        compiler_params=_params(("parallel",), est),
        name="ada_modulation",
    )(cb, w_mod, b_mod.reshape(depth, 1, n))
    return out.reshape(SUBLANES, 1, n)


def _split_bf16(v):
    hi = v.astype(BF16)
    return hi, (v - hi.astype(F32)).astype(BF16)


def _normmod_kernel(x_ref, g_ref, shift_ref, scale_ref, *rest, with_router):
    xf = x_ref[...]
    y = xf * lax.rsqrt(jnp.mean(xf * xf, axis=-1, keepdims=True) + EPS)
    h = (y * g_ref[...]) * (1.0 + scale_ref[...]) + shift_ref[...]
    if with_router:
        wr_ref, o_ref, lg_ref = rest
        hs = _split_bf16(h)
        ws = _split_bf16(wr_ref[...])
        acc = None
        for a, b in ((0, 0), (0, 1), (1, 0)):
            t = jnp.dot(hs[a], ws[b], preferred_element_type=F32)
            acc = t if acc is None else acc + t
        lg_ref[...] = acc
    else:
        (o_ref,) = rest
    o_ref[...] = h.astype(BF16)


def norm_modulate(x, g, mod, shift_idx, scale_idx, *, rows, seq, batch, w_router=None):
    d = x.shape[1]
    tm = 256
    with_router = w_router is not None
    in_specs = [pl.BlockSpec((tm, d), lambda i: (i, 0)),
                pl.BlockSpec((1, d), lambda i: (0, 0)),
                pl.BlockSpec((None, 1, d), lambda i: (_mod_row(i, tm, seq, batch), 0, shift_idx)),
                pl.BlockSpec((None, 1, d), lambda i: (_mod_row(i, tm, seq, batch), 0, scale_idx))]
    args = [x, g.reshape(1, d), mod, mod]
    out_specs = [pl.BlockSpec((tm, d), lambda i: (i, 0))]
    out_shape = [jax.ShapeDtypeStruct((rows, d), BF16)]
    if with_router:
        in_specs.append(pl.BlockSpec((d, LANES), lambda i: (0, 0)))
        args.append(w_router)
        out_specs.append(pl.BlockSpec((tm, LANES), lambda i: (i, 0)))
        out_shape.append(jax.ShapeDtypeStruct((rows, LANES), F32))
    est = 2 * (_nbytes((tm, d), F32) + _nbytes((tm, d), BF16)) + 8 * _nbytes((tm, d), F32)
    outs = pl.pallas_call(
        functools.partial(_normmod_kernel, with_router=with_router),
        grid=(pl.cdiv(rows, tm),),
        in_specs=in_specs, out_specs=out_specs, out_shape=out_shape,
        compiler_params=_params(("parallel",), est),
        name="norm_modulate_router" if with_router else "norm_modulate",
    )(*args)
    return outs if with_router else outs[0]


def _mm_kernel(x_ref, w_ref, o_ref):
    o_ref[...] = jnp.dot(x_ref[...], w_ref[...], preferred_element_type=F32).astype(o_ref.dtype)


def _mm_swiglu_kernel(x_ref, wg_ref, wu_ref, o_ref):
    x = x_ref[...]
    a = jnp.dot(x, wg_ref[...], preferred_element_type=F32)
    b = jnp.dot(x, wu_ref[...], preferred_element_type=F32)
    o_ref[...] = (a * jax.nn.sigmoid(a) * b).astype(o_ref.dtype)


def _rms(x, g):
    return x * lax.rsqrt(jnp.mean(x * x, axis=-1, keepdims=True) + EPS) * g


def _mm_resid_kernel(*refs, n_parts):
    x_refs, w_refs = refs[:n_parts], refs[n_parts:2 * n_parts]
    r_ref, gate_ref, o_ref = refs[2 * n_parts:]
    acc = None
    for x_ref, w_ref in zip(x_refs, w_refs):
        t = jnp.dot(x_ref[...], w_ref[...], preferred_element_type=F32)
        acc = t if acc is None else acc + t
    o_ref[...] = r_ref[...] + gate_ref[...] * acc


def _mm_glu_norm_kernel(y_ref, w_ref, g_ref, o_ref):
    y = y_ref[...]
    z = jnp.dot(y.astype(BF16), w_ref[...], preferred_element_type=F32)
    o_ref[...] = _rms(y * jax.nn.sigmoid(z), g_ref[...]).astype(BF16)


def matmul(x, w, *, rows, tm, tn, out_dtype):
    k, n = w.shape
    est = 2 * (_nbytes((tm, k), x.dtype) + _nbytes((k, tn), w.dtype) + _nbytes((tm, tn), out_dtype))
    return pl.pallas_call(
        _mm_kernel,
        grid=(pl.cdiv(rows, tm), pl.cdiv(n, tn)),
        in_specs=[pl.BlockSpec((tm, k), lambda i, j: (i, 0)),
                  pl.BlockSpec((k, tn), lambda i, j: (0, j))],
        out_specs=pl.BlockSpec((tm, tn), lambda i, j: (i, j)),
        out_shape=jax.ShapeDtypeStruct((rows, n), out_dtype),
        compiler_params=_params(("parallel", "parallel"), est),
        name="matmul",
    )(x, w)


def _rope(x, cos, sin_lo, sin_hi):
    return (x * cos + pltpu.roll(x, HEAD_DIM - HEAD_DIM // 4, 1) * sin_lo
            + pltpu.roll(x, HEAD_DIM // 4, 1) * sin_hi)


def _mm_rope_kernel(x_ref, w_ref, cos_ref, slo_ref, shi_ref, *o_refs, n_rope_tiles, scale, both, sub):
    j = pl.program_id(1)
    tm, tn = o_refs[0].shape
    for r0 in range(0, tm, sub):
        acc = jnp.dot(x_ref[r0:r0 + sub, :], w_ref[...], preferred_element_type=F32)
        if scale != 1.0:
            acc = acc * scale
        cos, slo, shi = (r[r0:r0 + sub, :] for r in (cos_ref, slo_ref, shi_ref))
        rot = jnp.concatenate([_rope(acc[:, h * HEAD_DIM:(h + 1) * HEAD_DIM], cos, slo, shi)
                               for h in range(tn // HEAD_DIM)], axis=1)
        if both:
            o_refs[0][r0:r0 + sub, :] = rot.astype(BF16)
            o_refs[1][r0:r0 + sub, :] = acc.astype(BF16)
        else:
            o_refs[0][r0:r0 + sub, :] = jnp.where(j < n_rope_tiles, rot, acc).astype(BF16)


def matmul_rope(x, w, tabs, *, rows, rows_x, seq, tm, tn, n_rope_tiles, scale, both):
    k, n = w.shape
    per_seq = seq // tm

    def tab_map(i, j):
        return (jnp.where(i * tm < rows_x, i % per_seq, per_seq), 0)

    n_out = 2 if both else 1
    est = 2 * (_nbytes((tm, k), BF16) + _nbytes((k, tn), BF16) + n_out * _nbytes((tm, tn), BF16)
               + 3 * _nbytes((tm, HEAD_DIM), F32)) + 4 * _nbytes((tm, tn), F32)
    out = pl.pallas_call(
        functools.partial(_mm_rope_kernel, n_rope_tiles=n_rope_tiles, scale=scale, both=both, sub=min(tm, 128)),
        grid=(pl.cdiv(rows, tm), n // tn),
        in_specs=[pl.BlockSpec((tm, k), lambda i, j: (i, 0)),
                  pl.BlockSpec((k, tn), lambda i, j: (0, j))] + [pl.BlockSpec((tm, HEAD_DIM), tab_map)] * 3,
        out_specs=[pl.BlockSpec((tm, tn), lambda i, j: (i, j))] * n_out,
        out_shape=[jax.ShapeDtypeStruct((rows, n), BF16)] * n_out,
        compiler_params=_params(("parallel", "parallel"), est),
        name="matmul_rope",
    )(x, w, *tabs)
    return out if both else out[0]


def matmul_swiglu(x, wg, wu, *, rows, tm, tn):
    k, n = wg.shape
    est = 2 * (_nbytes((tm, k), BF16) + 2 * _nbytes((k, tn), BF16) + _nbytes((tm, tn), BF16)) \
        + 3 * _nbytes((tm, tn), F32)
    return pl.pallas_call(
        _mm_swiglu_kernel,
        grid=(pl.cdiv(rows, tm), pl.cdiv(n, tn)),
        in_specs=[pl.BlockSpec((tm, k), lambda i, j: (i, 0)),
                  pl.BlockSpec((k, tn), lambda i, j: (0, j)),
                  pl.BlockSpec((k, tn), lambda i, j: (0, j))],
        out_specs=pl.BlockSpec((tm, tn), lambda i, j: (i, j)),
        out_shape=jax.ShapeDtypeStruct((rows, n), BF16),
        compiler_params=_params(("parallel", "parallel"), est),
        name="matmul_swiglu",
    )(x, wg, wu)


def matmul_gated_residual(xw, resid, mod, gate_idx, *, rows, tm, tn, seq, batch, weight_outer):
    xs_, ws_ = [p[0] for p in xw], [p[1] for p in xw]
    n = ws_[0].shape[1]
    k = sum(w.shape[0] for w in ws_)
    nj = n // tn
    if weight_outer:
        grid = (nj, pl.cdiv(rows, tm))
        ij = lambda a, b: (b, a)
    else:
        grid = (pl.cdiv(rows, tm), nj)
        ij = lambda a, b: (a, b)

    def gate_map(a, b):
        i, j = ij(a, b)
        return (_mod_row(i, tm, seq, batch), 0, gate_idx * nj + j)

    est = 2 * (_nbytes((tm, k), BF16) + _nbytes((k, tn), BF16) + 2 * _nbytes((tm, tn), F32))
    return pl.pallas_call(
        functools.partial(_mm_resid_kernel, n_parts=len(xw)),
        grid=grid,
        in_specs=[pl.BlockSpec((tm, x.shape[1]), lambda a, b: (ij(a, b)[0], 0)) for x in xs_]
        + [pl.BlockSpec((w.shape[0], tn), lambda a, b: (0, ij(a, b)[1])) for w in ws_]
        + [pl.BlockSpec((tm, tn), lambda a, b: ij(a, b)),
           pl.BlockSpec((None, 1, tn), gate_map)],
        out_specs=pl.BlockSpec((tm, tn), lambda a, b: ij(a, b)),
        out_shape=jax.ShapeDtypeStruct((rows, n), F32),
        compiler_params=_params(("parallel", "parallel"), est),
        name="matmul_gated_residual",
    )(*xs_, *ws_, resid, mod)


def matmul_glu_norm(y, w, g, *, rows, tm):
    k, n = w.shape
    est = 2 * (_nbytes((tm, k), F32) + _nbytes((k, n), BF16) + _nbytes((tm, n), BF16)) + 4 * _nbytes((tm, n), F32)
    return pl.pallas_call(
        _mm_glu_norm_kernel,
        grid=(pl.cdiv(rows, tm),),
        in_specs=[pl.BlockSpec((tm, k), lambda i: (i, 0)),
                  pl.BlockSpec((k, n), lambda i: (0, 0)),
                  pl.BlockSpec((1, n), lambda i: (0, 0))],
        out_specs=pl.BlockSpec((tm, n), lambda i: (i, 0)),
        out_shape=jax.ShapeDtypeStruct((rows, n), BF16),
        compiler_params=_params(("parallel",), est),
        name="matmul_glu_norm",
    )(y, w, g.reshape(1, n))


def _nt_dot(a, b):
    return lax.dot_general(a, b, (((1,), (1,)), ((), ())), preferred_element_type=F32)


def _attn_kernel(sink_ref, qr_ref, qp_ref, kp_ref, k0_ref, kn_ref, vp_ref, v0_ref, vn_ref, kc_ref, vc_ref,
                 bias_ref, g_ref, o_ref, o_scr, *, n_blocks, ctx_queries, n_kv):
    n = pl.program_id(1)
    qw = Q_PER_KV * HEAD_DIM

    def stack_heads(ref, hh):
        return jnp.concatenate([ref[:, hh * qw + g * HEAD_DIM:hh * qw + (g + 1) * HEAD_DIM]
                                for g in range(Q_PER_KV)], axis=0)

    def head(ref, hh):
        return ref[:, hh * HEAD_DIM:(hh + 1) * HEAD_DIM]

    def run_head(window, hh):
        sink = jnp.concatenate(
            [jnp.full((BLOCK, 1), sink_ref[hh * Q_PER_KV + g], F32) for g in range(Q_PER_KV)], axis=0)
        s = _nt_dot(stack_heads(qp_ref, hh), head(kc_ref, hh))
        v = head(vc_ref, hh)
        if window:
            k_win = jnp.concatenate([head(kp_ref, hh), head(k0_ref, hh), head(kn_ref, hh)], axis=0)
            s_w = _nt_dot(stack_heads(qr_ref, hh), k_win) + bias_ref[...]
            s = jnp.concatenate([s_w, s], axis=1)
            v = jnp.concatenate([head(vp_ref, hh), head(v0_ref, hh), head(vn_ref, hh), v], axis=0)
        n_keys = s.shape[1]
        m = jnp.maximum(jnp.max(s, axis=1, keepdims=True), sink)
        m_b = jnp.broadcast_to(m, (Q_PER_KV * BLOCK, LANES))
        sink_b = jnp.broadcast_to(sink, (Q_PER_KV * BLOCK, LANES))
        p = jnp.exp2(s - jnp.concatenate([m_b] * (n_keys // LANES), axis=1)).astype(BF16)
        v_ext = jnp.concatenate([v, jnp.ones((n_keys, HEAD_DIM), BF16)], axis=1)
        o_ext = jnp.dot(p, v_ext, preferred_element_type=F32)
        o = o_ext[:, :HEAD_DIM] / (o_ext[:, HEAD_DIM:] + jnp.exp2(sink_b - m_b))
        for g in range(Q_PER_KV):
            o_scr[:, hh * qw + g * HEAD_DIM:hh * qw + (g + 1) * HEAD_DIM] = o[g * BLOCK:(g + 1) * BLOCK]

    def run(window):
        for hh in range(n_kv):
            run_head(window, hh)
        o_ref[...] = _rms(o_scr[...], g_ref[...]).astype(BF16)

    if ctx_queries:
        pl.when(n < n_blocks)(lambda: run(True))
        pl.when(n >= n_blocks)(lambda: run(False))
    else:
        run(True)


def _window_bias(n_blocks):
    rows = Q_PER_KV * BLOCK
    row = (jnp.arange(rows) % BLOCK)[:, None]
    col = jnp.arange(3 * BLOCK)[None, :]
    band = (col >= row) & (col <= row + 2 * BLOCK)
    variants = []
    for code in range(4):
        ok = band
        if code & 1:
            ok = ok & (col >= BLOCK)
        if code & 2:
            ok = ok & (col < 2 * BLOCK)
        variants.append(jnp.where(ok, 0.0, NEG_INF).astype(F32))
    return jnp.stack(variants)


def attention(q_rot, q_plain, kv, sink, g_out, *, batch, seq, n_ctx, n_kv, ctx_queries):
    nb = seq // BLOCK
    qb = n_ctx // BLOCK if ctx_queries else 0
    qw = n_kv * Q_PER_KV * HEAD_DIM
    kw = n_kv * HEAD_DIM
    rows_out = batch * seq + (batch * n_ctx if ctx_queries else 0)
    ctx_blk0 = batch * seq // n_ctx
    q_blk0 = batch * seq // BLOCK

    def q_map(b, n, s):
        return (jnp.where(n < nb, b * nb + n, q_blk0 + b * qb + (n - nb)), 0)

    def kv_spec(col, shift):
        return pl.BlockSpec((BLOCK, kw), lambda b, n, s: (b * nb + jnp.clip(n + shift, 0, nb - 1), col))

    def bias_map(b, n, s):
        return ((n == 0).astype(jnp.int32) + 2 * (n == nb - 1).astype(jnp.int32), 0, 0)

    in_specs = [pl.BlockSpec((BLOCK, qw), q_map), pl.BlockSpec((BLOCK, qw), q_map),
                kv_spec(0, -1), kv_spec(0, 0), kv_spec(0, 1),
                kv_spec(1, -1), kv_spec(1, 0), kv_spec(1, 1),
                pl.BlockSpec((n_ctx, kw), lambda b, n, s: (ctx_blk0 + b, 0)),
                pl.BlockSpec((n_ctx, kw), lambda b, n, s: (ctx_blk0 + b, 1)),
                pl.BlockSpec((None, Q_PER_KV * BLOCK, 3 * BLOCK), bias_map),
                pl.BlockSpec((1, qw), lambda b, n, s: (0, 0))]
    grid_spec = pltpu.PrefetchScalarGridSpec(
        num_scalar_prefetch=1, grid=(batch, nb + qb), in_specs=in_specs,
        out_specs=pl.BlockSpec((BLOCK, qw), q_map),
        scratch_shapes=[pltpu.VMEM((BLOCK, qw), F32)])
    return pl.pallas_call(
        functools.partial(_attn_kernel, n_blocks=nb, ctx_queries=ctx_queries, n_kv=n_kv),
        grid_spec=grid_spec,
        out_shape=jax.ShapeDtypeStruct((rows_out, qw), BF16),
        compiler_params=_params(("parallel", "parallel"), 16 * 1024 * 1024),
        name="attention",
    )(sink, q_rot, q_plain, *([kv] * 8), _window_bias(nb), g_out.reshape(1, qw))


def _ssm_tables(a_re, a_im, log_dt, b_re, b_im, c_re, c_im):
    hp = lax.Precision.HIGHEST
    t_len = SSM_CHUNK
    g_per = LANES // SSM_GROUP
    n_groups, n_state = a_re.shape[1:]
    n_lb = n_groups // g_per
    dt = jnp.exp(log_dt.astype(F32))[..., None]
    lam_re, lam_im = a_re.astype(F32), a_im.astype(F32)
    mag = jnp.exp(lam_re * dt)
    abar = lax.complex(mag * jnp.cos(lam_im * dt), mag * jnp.sin(lam_im * dt))
    lam = lax.complex(lam_re, lam_im)
    bbar = ((abar - 1.0) / lam)[..., None] * lax.complex(b_re.astype(F32), b_im.astype(F32))
    cmat = lax.complex(c_re.astype(F32), c_im.astype(F32))

    def powers(d, exps):
        e = jnp.asarray(exps, F32)[:, None, None]
        m = jnp.exp(lam_re[d] * dt[d] * e)
        ph = lam_im[d] * dt[d] * e
        return lax.complex(m * jnp.cos(ph), m * jnp.sin(ph))

    ar = list(range(t_len + 1))
    pw_f, pw_b = powers(0, ar), powers(1, ar)

    kf = jnp.real(jnp.einsum('gcp,tgp,gpi->tgci', cmat[0], pw_f[:t_len], bbar[0], precision=hp))
    kb = jnp.real(jnp.einsum('gcp,tgp,gpi->tgci', cmat[1], pw_b[:t_len], bbar[1], precision=hp))
    lag = jnp.arange(2 * t_len - 1) - (t_len - 1)
    k_lag = jnp.where((lag >= 0)[:, None, None, None], kf[jnp.clip(lag, 0)], 0.0) \
        + jnp.where((lag <= 0)[:, None, None, None], kb[jnp.clip(-lag, 0)], 0.0)
    k_lag = jnp.swapaxes(k_lag, -1, -2).reshape(2 * t_len - 1, n_lb, g_per, SSM_GROUP, SSM_GROUP)
    eye = jnp.eye(g_per, dtype=F32)
    k_lag = jnp.einsum('mngab,gh->mngahb', k_lag, eye).reshape(2 * t_len - 1, n_lb, LANES, LANES)

    rep = LANES // n_state

    def e_coef(d, pw_sel):
        pw_rows = jnp.concatenate([pw_sel] * rep, axis=-1).reshape(t_len, n_lb, g_per, 1, LANES)
        pw_rows = jnp.broadcast_to(pw_rows, (t_len, n_lb, g_per, SSM_GROUP, LANES)).reshape(t_len, n_lb, LANES, LANES)
        bb_rows = jnp.concatenate([jnp.swapaxes(bbar[d], -1, -2)] * rep, axis=-1).reshape(n_lb, LANES, LANES)
        return pw_rows * bb_rows[None]

    ef, eb = e_coef(0, powers(0, [t_len - 1 - j for j in range(t_len)])), e_coef(1, pw_b[:t_len])
    e_c = jnp.concatenate([jnp.real(ef), jnp.imag(ef), jnp.real(eb), jnp.imag(eb)], axis=-1)

    def f_coef(d, pw_sel):
        cm = cmat[d].reshape(n_lb, g_per, SSM_GROUP, n_state).transpose(0, 3, 1, 2)
        cm = cm.reshape(n_lb, n_state, 1, LANES)
        pw = pw_sel.reshape(t_len, n_lb, g_per, n_state).transpose(1, 3, 0, 2)
        pw = jnp.repeat(pw, SSM_GROUP, axis=-1)
        return cm * pw

    ff, fb = f_coef(0, pw_f[1:]), f_coef(1, powers(1, [t_len - l for l in range(t_len)]))
    f_c = jnp.stack([jnp.real(ff), -jnp.imag(ff), jnp.real(fb), -jnp.imag(fb)], axis=1)
    f_c = f_c.reshape(n_lb, 4 * n_state * t_len, LANES)

    def lay(v):
        return v.reshape(n_lb, g_per * n_state)

    a_chunk = jnp.concatenate([lay(jnp.real(pw_f[t_len])), lay(jnp.imag(pw_f[t_len])),
                               lay(jnp.real(pw_b[t_len])), lay(jnp.imag(pw_b[t_len]))], axis=-1)
    return k_lag.astype(BF16), e_c, f_c, a_chunk.reshape(1, -1)


def _chunk_inputs(x_ref, tmc):
    return [x_ref[pl.ds(j, tmc, stride=SSM_CHUNK), :] for j in range(SSM_CHUNK)]


def _ssm_local_kernel(x_ref, ec_ref, s_ref, e_scr, *, n_state):
    tmc = s_ref.shape[0]
    qw = e_scr.shape[1] // 4

    @pl.when(pl.program_id(1) == 0)
    def _():
        row_g = lax.broadcasted_iota(jnp.int32, (LANES, qw), 0) // SSM_GROUP
        col_g = lax.broadcasted_iota(jnp.int32, (LANES, qw), 1) // n_state
        same = row_g == col_g
        for j in range(SSM_CHUNK):
            for q in range(4):
                v = ec_ref[j, :, q * LANES:(q + 1) * LANES]
                tiled = jnp.concatenate([v] * (qw // LANES), axis=1)
                e_scr[j * LANES:(j + 1) * LANES, q * qw:(q + 1) * qw] = jnp.where(same, tiled, 0.0).astype(BF16)

    xs = jnp.concatenate([x.astype(BF16) for x in _chunk_inputs(x_ref, tmc)], axis=1)
    s_ref[...] = jnp.dot(xs, e_scr[...], preferred_element_type=F32)


def _ssm_scan_kernel(s_ref, a_ref, h_ref, *, batch, nc_x, nc_c):
    sw = s_ref.shape[1] // 4
    a = a_ref[...]
    afr, afi, abr, abi = (a[:, k * sw:(k + 1) * sw] for k in range(4))
    ctx0 = batch * nc_x

    def step(row_f, row_b, st):
        new = []
        for (row, ar, ai, off, (hr, hi)) in ((row_f, afr, afi, 0, st[0]), (row_b, abr, abi, 2 * sw, st[1])):
            h_ref[pl.ds(row, 1), off:off + sw] = hr
            h_ref[pl.ds(row, 1), off + sw:off + 2 * sw] = hi
            sr = s_ref[pl.ds(row, 1), off:off + sw]
            si = s_ref[pl.ds(row, 1), off + sw:off + 2 * sw]
            new.append((ar * hr - ai * hi + sr, ar * hi + ai * hr + si))
        return tuple(new)

    zero = jnp.zeros((1, sw), F32)
    for b in range(batch):
        st = ((zero, zero), (zero, zero))
        c_base = ctx0 + b * nc_c
        st = lax.fori_loop(0, nc_c, lambda t, s: step(c_base + t, c_base + nc_c - 1 - t, s), st)
        x_base = b * nc_x
        lax.fori_loop(0, nc_x, lambda t, s: step(x_base + t, x_base + nc_x - 1 - t, s), st)


def _ssm_out_kernel(x_ref, h_ref, kl_ref, fc_ref, d_ref, o_ref, w_scr, *, n_state):
    t = SSM_CHUNK
    tmc = h_ref.shape[0]
    sdim = h_ref.shape[1]
    qw = sdim // 4

    @pl.when(pl.program_id(1) == 0)
    def _():
        for j in range(t):
            for l in range(t):
                w_scr[j * LANES:(j + 1) * LANES, l * LANES:(l + 1) * LANES] = kl_ref[l - j + t - 1]
        lane_g = lax.broadcasted_iota(jnp.int32, (n_state, LANES), 1) // SSM_GROUP
        for q in range(4):
            for l in range(t):
                v = fc_ref[pl.ds(q * n_state * t + l, n_state, stride=t), :]
                for g in range(qw // n_state):
                    r0 = t * LANES + q * qw + g * n_state
                    w_scr[r0:r0 + n_state, l * LANES:(l + 1) * LANES] = \
                        jnp.where(lane_g == g, v, 0.0).astype(BF16)

    xs = _chunk_inputs(x_ref, tmc)
    lhs = jnp.concatenate([x.astype(BF16) for x in xs] + [h_ref[...].astype(BF16)], axis=1)
    y = jnp.dot(lhs, w_scr[...], preferred_element_type=F32)
    d = d_ref[...]
    for l in range(t):
        o_ref[pl.ds(l, tmc, stride=t), :] = jax.nn.gelu(y[:, l * LANES:(l + 1) * LANES] + d * xs[l])


def s5_mix(u, tables, d_skip, layer, *, n_state, batch, seq, n_ctx):
    k_lag, e_c, f_c, a_chunk = tables
    rows, ssm_width = u.shape
    t = SSM_CHUNK
    n_lb = ssm_width // LANES
    rc = rows // t
    sdim = 4 * (LANES // SSM_GROUP) * n_state
    tmc = max(m for m in range(SUBLANES, min(rc, 264) + 1, SUBLANES) if rc % m == 0)
    x_spec = pl.BlockSpec((tmc * t, LANES), lambda g, r: (r, g))
    x_bytes = _nbytes((tmc * t, LANES), F32)

    s_loc = pl.pallas_call(
        functools.partial(_ssm_local_kernel, n_state=n_state),
        grid=(n_lb, rc // tmc),
        in_specs=[x_spec, pl.BlockSpec((None, t, None, LANES, 4 * LANES), lambda g, r: (layer, 0, g, 0, 0))],
        out_specs=pl.BlockSpec((tmc, sdim), lambda g, r: (r, g)),
        out_shape=jax.ShapeDtypeStruct((rc, n_lb * sdim), F32),
        scratch_shapes=[pltpu.VMEM((t * LANES, sdim), BF16)],
        compiler_params=_params(("arbitrary", "arbitrary"),
                                2 * x_bytes + 2 * _nbytes((t, LANES, 4 * LANES), F32)
                                + _nbytes((t * LANES, sdim), BF16) + 4 * _nbytes((tmc, sdim), F32)),
        name="s5_local_states",
    )(u, e_c)

    h_in = pl.pallas_call(
        functools.partial(_ssm_scan_kernel, batch=batch, nc_x=seq // t, nc_c=n_ctx // t),
        grid=(n_lb,),
        in_specs=[pl.BlockSpec((rc, sdim), lambda g: (0, g)),
                  pl.BlockSpec((None, 1, sdim), lambda g: (layer, 0, g))],
        out_specs=pl.BlockSpec((rc, sdim), lambda g: (0, g)),
        out_shape=jax.ShapeDtypeStruct((rc, n_lb * sdim), F32),
        compiler_params=_params(("parallel",), 4 * _nbytes((rc, sdim), F32)),
        name="s5_chunk_scan",
    )(s_loc, a_chunk)

    return pl.pallas_call(
        functools.partial(_ssm_out_kernel, n_state=n_state),
        grid=(n_lb, rc // tmc),
        in_specs=[x_spec,
                  pl.BlockSpec((tmc, sdim), lambda g, r: (r, g)),
                  pl.BlockSpec((None, 2 * t - 1, None, LANES, LANES), lambda g, r: (layer, 0, g, 0, 0)),
                  pl.BlockSpec((None, None, 4 * n_state * t, LANES), lambda g, r: (layer, g, 0, 0)),
                  pl.BlockSpec((1, LANES), lambda g, r: (0, g))],
        out_specs=x_spec,
        out_shape=jax.ShapeDtypeStruct((rows, ssm_width), F32),
        scratch_shapes=[pltpu.VMEM((t * LANES + sdim, t * LANES), BF16)],
        compiler_params=_params(("arbitrary", "arbitrary"),
                                4 * x_bytes + 2 * _nbytes((tmc, sdim), F32)
                                + 2 * _nbytes((2 * t - 1, LANES, LANES), BF16)
                                + 2 * _nbytes((4 * n_state, t * LANES), F32)
                                + _nbytes((t * LANES + sdim, t * LANES), BF16) + 4 * _nbytes((tmc, sdim), F32)),
        name="s5_readout",
    )(u, h_in, k_lag, f_c, d_skip.reshape(1, ssm_width))


def _final_norm_kernel(x_ref, g_ref, o_ref):
    o_ref[...] = _rms(x_ref[...], g_ref[...])


def final_norm(x, g, *, rows):
    d = x.shape[1]
    tm = 256
    return pl.pallas_call(
        _final_norm_kernel,
        grid=(pl.cdiv(rows, tm),),
        in_specs=[pl.BlockSpec((tm, d), lambda i: (i, 0)), pl.BlockSpec((1, d), lambda i: (0, 0))],
        out_specs=pl.BlockSpec((tm, d), lambda i: (i, 0)),
        out_shape=jax.ShapeDtypeStruct((rows, d), F32),
        compiler_params=_params(("parallel",), 6 * _nbytes((tm, d), F32)),
        name="final_norm",
    )(x, g.reshape(1, d))


def _new_expert(te_ref, t):
    return jnp.logical_or(t == 0, te_ref[t] != te_ref[jnp.maximum(t - 1, 0)])


def _moe_up_kernel(te_ref, nu_ref, x_ref, wg_ref, wu_ref, o_ref, wg_scr, wu_scr):
    t = pl.program_id(1)

    @pl.when(_new_expert(te_ref, t))
    def _():
        wg_scr[...] = wg_ref[...].astype(BF16)
        wu_scr[...] = wu_ref[...].astype(BF16)

    @pl.when(t < nu_ref[0])
    def _():
        x = x_ref[...]
        a = jnp.dot(x, wg_scr[...], preferred_element_type=F32)
        b = jnp.dot(x, wu_scr[...], preferred_element_type=F32)
        o_ref[...] = (a * jax.nn.sigmoid(a) * b).astype(o_ref.dtype)

    @pl.when(t >= nu_ref[0])
    def _():
        o_ref[...] = jnp.zeros_like(o_ref)


def _moe_down_kernel(te_ref, nu_ref, h_ref, wd_ref, rw_ref, o_ref, wd_scr):
    t = pl.program_id(1)

    @pl.when(_new_expert(te_ref, t))
    def _():
        wd_scr[...] = wd_ref[...].astype(BF16)

    @pl.when(t < nu_ref[0])
    def _():
        acc = jnp.dot(h_ref[...], wd_scr[...], preferred_element_type=F32)
        o_ref[...] = (acc * rw_ref[...]).astype(o_ref.dtype)

    @pl.when(t >= nu_ref[0])
    def _():
        o_ref[...] = jnp.zeros_like(o_ref)


def moe_experts(xg, row_w, tile_expert, n_used, w_g, w_u, w_d, e_base, *, tmr, tn, tn_down):
    r_rows, d = xg.shape
    d_e = w_g.shape[2]
    n_tiles = r_rows // tmr
    est = 2 * (_nbytes((tmr, d), BF16) + 2 * _nbytes((d, tn), F32) + _nbytes((tmr, tn), BF16)) \
        + 2 * _nbytes((d, tn), BF16) + 3 * _nbytes((tmr, tn), F32)
    hidden = pl.pallas_call(
        _moe_up_kernel,
        grid_spec=pltpu.PrefetchScalarGridSpec(
            num_scalar_prefetch=2, grid=(d_e // tn, n_tiles),
            in_specs=[pl.BlockSpec((tmr, d), lambda j, t, te, nu: (t, 0)),
                      pl.BlockSpec((None, d, tn), lambda j, t, te, nu: (e_base + te[t], 0, j)),
                      pl.BlockSpec((None, d, tn), lambda j, t, te, nu: (e_base + te[t], 0, j))],
            out_specs=pl.BlockSpec((tmr, tn), lambda j, t, te, nu: (t, j)),
            scratch_shapes=[pltpu.VMEM((d, tn), BF16), pltpu.VMEM((d, tn), BF16)]),
        out_shape=jax.ShapeDtypeStruct((r_rows, d_e), BF16),
        compiler_params=_params(("arbitrary", "arbitrary"), est),
        name="moe_gate_up",
    )(tile_expert, n_used, xg, w_g, w_u)
    tn = tn_down
    est = 2 * (_nbytes((tmr, d_e), BF16) + _nbytes((d_e, tn), F32) + _nbytes((tmr, tn), BF16)) \
        + _nbytes((d_e, tn), BF16) + 2 * _nbytes((tmr, tn), F32)
    return pl.pallas_call(
        _moe_down_kernel,
        grid_spec=pltpu.PrefetchScalarGridSpec(
            num_scalar_prefetch=2, grid=(d // tn, n_tiles),
            in_specs=[pl.BlockSpec((tmr, d_e), lambda j, t, te, nu: (t, 0)),
                      pl.BlockSpec((None, d_e, tn), lambda j, t, te, nu: (e_base + te[t], 0, j)),
                      pl.BlockSpec((tmr, 1), lambda j, t, te, nu: (t, 0))],
            out_specs=pl.BlockSpec((tmr, tn), lambda j, t, te, nu: (t, j)),
            scratch_shapes=[pltpu.VMEM((d_e, tn), BF16)]),
        out_shape=jax.ShapeDtypeStruct((r_rows, d), BF16),
        compiler_params=_params(("arbitrary", "arbitrary"), est),
        name="moe_down",
    )(tile_expert, n_used, hidden, w_d, row_w)


def _moe_combine_kernel(x_ref, y1_ref, y2_ref, gate_ref, *rest, final):
    out = x_ref[...] + gate_ref[...] * (y1_ref[...].astype(F32) + y2_ref[...].astype(F32))
    if final:
        g_ref, o_ref = rest
        o_ref[...] = _rms(out, g_ref[...])
    else:
        (o_ref,) = rest
        o_ref[...] = out


def moe_route(logits, n_experts, tmr):
    n_tok = logits.shape[0]
    lg = logits[:, :n_experts]
    top_v, top_i = lax.top_k(lg, TOP_K)
    top_w = jax.nn.softmax(top_v, axis=-1)
    flat_e = top_i.reshape(-1)
    onehot = (flat_e[:, None] == jnp.arange(n_experts)[None, :]).astype(jnp.int32)
    rank = jnp.take_along_axis(jnp.cumsum(onehot, axis=0) - onehot, flat_e[:, None], axis=1)[:, 0]
    counts = jnp.sum(onehot, axis=0)
    tiles_per = (counts + tmr - 1) // tmr
    tile_start = jnp.cumsum(tiles_per) - tiles_per
    dest = tile_start[flat_e] * tmr + rank
    n_tiles = (n_tok * TOP_K) // tmr + n_experts
    r_rows = n_tiles * tmr
    slot_tok = (jnp.arange(n_tok * TOP_K, dtype=jnp.int32) // TOP_K).astype(F32)
    pad_tok = (jnp.arange(r_rows, dtype=jnp.int32) % n_tok).astype(F32)
    placed = jnp.stack([pad_tok, jnp.zeros((r_rows,), F32)], axis=1).at[dest].set(
        jnp.stack([slot_tok, top_w.reshape(-1)], axis=1))
    src_tok = placed[:, 0].astype(jnp.int32)
    row_w = placed[:, 1]
    tile_ids = jnp.arange(n_tiles, dtype=jnp.int32)
    tile_expert = jnp.sum((tile_ids[:, None] >= jnp.cumsum(tiles_per)[None, :]).astype(jnp.int32), axis=1)
    n_used = jnp.sum(tiles_per).astype(jnp.int32)
    tile_expert = jnp.where(tile_ids < n_used, tile_expert, tile_expert[jnp.maximum(n_used - 1, 0)])
    tile_expert = jnp.minimum(tile_expert, n_experts - 1).astype(jnp.int32)
    return src_tok, row_w.reshape(r_rows, 1), tile_expert, n_used.reshape(1), dest.reshape(n_tok, TOP_K)


def moe_ffn_residual(x, h, logits, mod, gate_idx, w_g, w_u, w_d, layer_j, *, rows, seq, batch, final_g=None):
    d = x.shape[1]
    n_moe, n_experts, _, d_e = w_g.shape
    tmr = 512 if rows * TOP_K >= 8 * 512 * n_experts else 128
    tn = 512 if d_e % 512 == 0 and d >= 4096 else 256
    src_tok, row_w, tile_expert, n_used, dest = moe_route(logits, n_experts, tmr)
    xg = jnp.take(h, src_tok, axis=0, mode="clip")
    y = moe_experts(xg, row_w, tile_expert, n_used,
                    w_g.reshape(n_moe * n_experts, d, d_e), w_u.reshape(n_moe * n_experts, d, d_e),
                    w_d.reshape(n_moe * n_experts, d_e, d), layer_j * n_experts, tmr=tmr, tn=tn,
                    tn_down=2 * tn)
    y1 = jnp.take(y, dest[:, 0], axis=0, mode="clip")
    y2 = jnp.take(y, dest[:, 1], axis=0, mode="clip")
    tm = 256
    final = final_g is not None
    in_specs = [pl.BlockSpec((tm, d), lambda i: (i, 0)),
                pl.BlockSpec((tm, d), lambda i: (i, 0)),
                pl.BlockSpec((tm, d), lambda i: (i, 0)),
                pl.BlockSpec((None, 1, d), lambda i: (_mod_row(i, tm, seq, batch), 0, gate_idx))]
    args = [x, y1, y2, mod]
    if final:
        in_specs.append(pl.BlockSpec((1, d), lambda i: (0, 0)))
        args.append(final_g.reshape(1, d))
    return pl.pallas_call(
        functools.partial(_moe_combine_kernel, final=final),
        grid=(rows // tm,),
        in_specs=in_specs,
        out_specs=pl.BlockSpec((tm, d), lambda i: (i, 0)),
        out_shape=jax.ShapeDtypeStruct((rows, d), F32),
        compiler_params=_params(("parallel",), 10 * _nbytes((tm, d), F32)),
        name="moe_combine",
    )(*args)


def _rope_tables(seq, pad_rows):
    rows = seq // GRID_W
    half = HEAD_DIM // 2
    row = jnp.broadcast_to(jnp.arange(rows)[:, None], (rows, GRID_W)).reshape(-1).astype(F32)
    col = jnp.broadcast_to(jnp.arange(GRID_W)[None, :], (rows, GRID_W)).reshape(-1).astype(F32)
    inv_freq = ROPE_BASE ** (-jnp.arange(0, half, 2, dtype=F32) / half)
    ang_r = row[:, None] * inv_freq
    ang_c = col[:, None] * inv_freq
    ang = jnp.concatenate([ang_r, ang_r, ang_c, ang_c], axis=-1)
    cos, sin = jnp.cos(ang), jnp.sin(ang)
    first = (jnp.arange(HEAD_DIM) % half) < half // 2
    ones = jnp.ones((pad_rows, HEAD_DIM), F32)
    zeros = jnp.zeros((pad_rows, HEAD_DIM), F32)
    return (jnp.concatenate([cos, ones]), jnp.concatenate([jnp.where(first, -sin, 0.0), zeros]),
            jnp.concatenate([jnp.where(first, 0.0, sin), zeros]))


def kernel(x, c, ctx, c_ctx, w_mod, b_mod, g_attn_norm, w_in, attn_sink, ssm_a_re, ssm_a_im, ssm_log_dt,
           ssm_b_re, ssm_b_im, ssm_c_re, ssm_c_im, ssm_d, w_glu, g_out_attn, g_out_ssm, w_out, g_ffn_norm,
           w_ff_gate, w_ff_up, w_ff_down, w_router, w_exp_gate, w_exp_up, w_exp_down, g_final):
    batch, seq, d = x.shape
    n_ctx = ctx.shape[1]
    depth = w_in.shape[0]
    n_q = attn_sink.shape[1]
    n_kv = n_q // Q_PER_KV
    attn_w = n_q * HEAD_DIM
    kv_w = n_kv * HEAD_DIM
    rows_x = batch * seq
    rows_all = rows_x + batch * n_ctx
    big = rows_x >= 8192
    tm_big = 1024 if big else 256
    tm_mid = 512 if big else 256
    tn = 512 if big else 256
    tn_wide = 1024 if big else 256
    geo = dict(seq=seq, batch=batch)

    xs = jnp.concatenate([x.reshape(rows_x, d), ctx.reshape(batch * n_ctx, d)], axis=0)
    cond = jnp.concatenate([c, c_ctx[None, :]], axis=0)
    rope_tabs = _rope_tables(seq, tm_big)
    rope_geo = dict(rows=rows_all, rows_x=rows_x, seq=seq, tm=tm_big, tn=tn_wide)
    ssm_tables = jax.vmap(_ssm_tables)(ssm_a_re, ssm_a_im, ssm_log_dt, ssm_b_re, ssm_b_im, ssm_c_re, ssm_c_im)
    normed = False

    for i in range(depth):
        last = i == depth - 1
        rows_out = rows_x if last else rows_all
        mod = ada_modulation(cond, w_mod, b_mod, i)
        h = norm_modulate(xs, g_attn_norm[i], mod, 0, 1, rows=rows_all, **geo)
        u_col0 = attn_w + 2 * kv_w
        q_rot, q_plain = matmul_rope(h, w_in[i, :, :attn_w].astype(BF16), rope_tabs,
                                     n_rope_tiles=attn_w // tn_wide, scale=HEAD_DIM ** -0.5 * LOG2E, both=True,
                                     **rope_geo)
        kv = matmul_rope(h, w_in[i, :, attn_w:u_col0].astype(BF16), rope_tabs, n_rope_tiles=kv_w // tn_wide,
                         scale=1.0, both=False, **rope_geo)
        u = matmul(h, w_in[i, :, u_col0:].astype(BF16), rows=rows_all, tm=tm_big, tn=tn_wide, out_dtype=F32)
        o_attn = attention(q_rot, q_plain, kv, attn_sink[i] * LOG2E, g_out_attn[i], batch=batch, seq=seq,
                           n_ctx=n_ctx, n_kv=n_kv, ctx_queries=not last)
        y = s5_mix(u, ssm_tables, ssm_d[i], i, n_state=ssm_a_re.shape[-1], batch=batch, seq=seq, n_ctx=n_ctx)
        o_ssm = matmul_glu_norm(y, w_glu[i].astype(BF16), g_out_ssm[i], rows=rows_out, tm=tm_big)
        xs = matmul_gated_residual([(o_attn, w_out[i, :attn_w].astype(BF16)),
                                    (o_ssm, w_out[i, attn_w:].astype(BF16))], xs, mod, 2,
                                   rows=rows_out, tm=tm_big, tn=tn, weight_outer=False, **geo)
        j = i // 2
        if i % 2 == 0:
            h2 = norm_modulate(xs, g_ffn_norm[i], mod, 3, 4, rows=rows_out, **geo)
            hid = matmul_swiglu(h2, w_ff_gate[j].astype(BF16), w_ff_up[j].astype(BF16),
                                rows=rows_out, tm=tm_big, tn=tn)
            xs = matmul_gated_residual([(hid, w_ff_down[j].astype(BF16))], xs, mod, 5, rows=rows_out, tm=tm_mid,
                                       tn=tn, weight_outer=True, **geo)
        else:
            n_e = w_router.shape[2]
            wr = jnp.zeros((d, LANES), F32).at[:, :n_e].set(w_router[j])
            h2, logits = norm_modulate(xs, g_ffn_norm[i], mod, 3, 4, rows=rows_out, w_router=wr, **geo)
            xs = moe_ffn_residual(xs, h2, logits, mod, 5, w_exp_gate, w_exp_up, w_exp_down, j,
                                  rows=rows_out, seq=seq, batch=batch, final_g=g_final if last else None)
            normed = last
    if not normed:
        xs = final_norm(xs, g_final, rows=rows_x)
    return xs.reshape(batch, seq, d)
```
